```python
import math
import jax, jax.numpy as jnp
from jax import lax
import numpy as np

D_MODEL = 1024
BATCH = 2
SEQ = 8192
DEPTH = 4

N_MIXERS = 3
N_A = (DEPTH + 2) // 3
N_B = (DEPTH + 1) // 3
N_C = DEPTH // 3
D_FF = ((8 * D_MODEL // 3 + 127) // 128) * 128
NORM_EPS = 1e-6
SC_WIDTH = 3
HY_SHORT = 3
HY_EMB = 33
HY_BANDS = (HY_EMB - 1) // 2
HY_ORDER = 64
HY_TARGET = 1e-2
HY_FAST_PCT = 0.3
HY_SLOW_PCT = 1.5
HY_MAX_DECAY = math.log(HY_TARGET) / HY_FAST_PCT
HY_MIN_DECAY = math.log(HY_TARGET) / HY_SLOW_PCT
GD_HEADS = 8
GD_DK = D_MODEL // GD_HEADS
GD_DV = D_MODEL // GD_HEADS
GD_CONV = 3
GD_CHUNK = 64
GD_QKV = 2 * GD_HEADS * GD_DK + GD_HEADS * GD_DV

kernel_name = 'hybrid_shortconv_hyena_gdn_encoder'


def rms_norm(x, g):
    xf = x.astype(jnp.float32)
    y = xf * lax.rsqrt(jnp.mean(xf * xf, axis=-1, keepdims=True) + NORM_EPS)
    return (y * g.astype(jnp.float32)).astype(x.dtype)


def centred_dwconv(x, w):
    K = w.shape[0]
    p = K // 2
    L = x.shape[1]
    xp = jnp.pad(x, ((0, 0), (p, p), (0, 0)))
    return sum(xp[:, j:j + L] * w[j] for j in range(K))


def swiglu(x, w_in, w_out):
    g, u = jnp.split(x @ w_in, 2, axis=-1)
    return (jax.nn.silu(g) * u) @ w_out


def short_conv_mixer(x, w_in, conv_w, w_out):
    b, c, h = jnp.split(x @ w_in, 3, axis=-1)
    return (b * centred_dwconv(c * h, conv_w)) @ w_out


def hyena_filters(L, w1, b1, w2, b2, w3, freq):
    f32 = jnp.float32
    t = jnp.linspace(0.0, 1.0, L, dtype=f32)[:, None]
    w = 2.0 * math.pi * jnp.arange(L, dtype=f32)[:, None] / L
    f = jnp.linspace(1e-4, HY_BANDS - 1, HY_BANDS, dtype=f32)[None, :]
    z = jnp.concatenate([t, jnp.cos(f * w), -jnp.sin(f * w)], axis=-1)
    fr = freq.astype(f32)
    h = jnp.sin(fr * (z @ w1.astype(f32) + b1.astype(f32)))
    h = jnp.sin(fr * (h @ w2.astype(f32) + b2.astype(f32)))
    h = h @ w3.astype(f32)
    D = h.shape[-1] // 2
    deltas = jnp.linspace(HY_MIN_DECAY, HY_MAX_DECAY, D, dtype=f32)
    h = h * jnp.exp(-t * jnp.abs(jnp.concatenate([deltas, deltas])))[None] [0]
    return h[:, :D], h[:, D:]


def two_sided_fft_conv(u, h_f, h_b):
    L = u.shape[1]
    zero = jnp.zeros((1, h_f.shape[1]), h_f.dtype)
    h_circ = jnp.concatenate([h_f, zero, h_b[1:][::-1]], axis=0)
    uf = jnp.fft.rfft(u, n=2 * L, axis=1)
    hf = jnp.fft.rfft(h_circ, axis=0)
    return jnp.fft.irfft(uf * hf[None], n=2 * L, axis=1)[:, :L]


def hyena_mixer(x, w_in, b_in, conv_w, conv_b, f_w1, f_b1, f_w2, f_b2, f_w3, f_freq, d_bias, w_out, b_out):
    L = x.shape[1]
    u = centred_dwconv(x @ w_in + b_in, conv_w) + conv_b
    x0, x1, v = jnp.split(u, 3, axis=-1)
    h_f, h_b = hyena_filters(L, f_w1, f_b1, f_w2, f_b2, f_w3, f_freq)
    v = (v * x1).astype(jnp.float32)
    y = two_sided_fft_conv(v, h_f, h_b) + v * d_bias.astype(jnp.float32)
    return (y.astype(x.dtype) * x0) @ w_out + b_out


def chunk_gated_delta_rule(q, k, v, g, beta):
    Bt, L, H, dk = q.shape
    dv = v.shape[-1]
    C = GD_CHUNK
    N = L // C

    def blocks(t):
        t = t.reshape((Bt, N, C, H) + t.shape[3:])
        return jnp.moveaxis(t, 3, 1)

    q, k, v, g, beta = blocks(q), blocks(k), blocks(v), blocks(g), blocks(beta)
    gc = jnp.cumsum(g, axis=-1)
    idx = jnp.arange(C)
    incl = idx[:, None] >= idx[None, :]
    strict = idx[:, None] > idx[None, :]
    decay = jnp.exp(jnp.where(incl, gc[..., :, None] - gc[..., None, :], -jnp.inf))
    kb = k * beta[..., None]
    vb = v * beta[..., None]
    lower = jnp.where(strict, jnp.einsum('bhncd,bhnsd->bhncs', kb, k) * decay, 0.0)
    eye = jnp.eye(C, dtype=q.dtype)
    rhs = jnp.concatenate([vb, kb * jnp.exp(gc)[..., None]], axis=-1)
    sol = lax.linalg.triangular_solve(lower + eye, rhs, left_side=True, lower=True, unit_diagonal=True)
    u, w = sol[..., :dv], sol[..., dv:]
    attn = jnp.einsum('bhncd,bhnsd->bhncs', q, k) * decay
    q_dec = q * jnp.exp(gc)[..., None]
    k_dec = k * jnp.exp(gc[..., -1:] - gc)[..., None]
    g_tot = jnp.exp(gc[..., -1])

    def step(S, inp):
        u_n, w_n, a_n, qd_n, kd_n, gt_n = inp
        v_new = u_n - jnp.einsum('bhcd,bhde->bhce', w_n, S)
        o_n = jnp.einsum('bhcd,bhde->bhce', qd_n, S) + jnp.einsum('bhcs,bhse->bhce', a_n, v_new)
        S = S * gt_n[..., None, None] + jnp.einsum('bhcd,bhce->bhde', kd_n, v_new)
        return S, o_n

    xs = tuple(jnp.moveaxis(t, 2, 0) for t in (u, w, attn, q_dec, k_dec, g_tot))
    S0 = jnp.zeros((Bt, H, dk, dv), jnp.float32)
    _, o = lax.scan(step, S0, xs)
    o = jnp.moveaxis(o, 0, 2)
    return jnp.moveaxis(o, 1, 3).reshape(Bt, L, H, dv)


def l2norm(t):
    return t * lax.rsqrt(jnp.sum(t * t, axis=-1, keepdims=True) + 1e-6)


def gated_deltanet_mixer(x, w_in, conv_w, a_log, dt_bias, norm_g, w_out):
    B, L, _ = x.shape
    H = GD_HEADS
    f32 = jnp.float32
    proj = x @ w_in
    qkv, z, a, b = jnp.split(proj, [GD_QKV, GD_QKV + H * GD_DV, GD_QKV + H * GD_DV + 2 * H], axis=-1)
    qkv = jax.nn.silu(centred_dwconv(qkv, conv_w)).astype(f32)
    q, k, v = jnp.split(qkv, [H * GD_DK, 2 * H * GD_DK], axis=-1)
    q = l2norm(q.reshape(B, L, H, GD_DK)) * (GD_DK ** -0.5)
    k = l2norm(k.reshape(B, L, H, GD_DK))
    v = v.reshape(B, L, H, GD_DV)
    a = a.astype(f32).reshape(B, L, 2, H)
    b = b.astype(f32).reshape(B, L, 2, H)
    g = -jnp.exp(a_log.astype(f32)) * jax.nn.softplus(a + dt_bias.astype(f32))
    beta = jax.nn.sigmoid(b)
    flip = lambda t: jnp.flip(t, axis=1)
    q2 = jnp.concatenate([q, flip(q)], axis=0)
    k2 = jnp.concatenate([k, flip(k)], axis=0)
    v2 = jnp.concatenate([v, flip(v)], axis=0)
    g2 = jnp.concatenate([g[:, :, 0], flip(g[:, :, 1])], axis=0)
    b2 = jnp.concatenate([beta[:, :, 0], flip(beta[:, :, 1])], axis=0)
    o = chunk_gated_delta_rule(q2, k2, v2, g2, b2)
    o = o[:B] + flip(o[B:])
    o = rms_norm(o, norm_g) * jax.nn.silu(z.astype(f32).reshape(B, L, H, GD_DV))
    return o.reshape(B, L, H * GD_DV).astype(x.dtype) @ w_out


def setup_inputs(seed: int = 0) -> dict:
    key = jax.random.key(seed)
    ks = jax.random.split(key, 32)
    f32 = jnp.float32
    D, F, H = D_MODEL, D_FF, GD_HEADS

    def dense(k, shape, fan_in, scale=1.0):
        return jax.random.normal(k, shape, f32) * (scale * fan_in ** -0.5)

    def small(k, shape, scale=0.01):
        return jax.random.normal(k, shape, f32) * scale

    dt = jnp.exp(jax.random.uniform(ks[25], (N_C, 2, H), f32, math.log(1e-3), math.log(1e-1)))
    return {
        'x': jax.random.normal(ks[0], (BATCH, SEQ, D), f32),
        'norms': 1.0 + small(ks[1], (DEPTH, 3, D)),
        'final_norm': 1.0 + small(ks[2], (D,)),
        'ffn_w_in': dense(ks[3], (DEPTH, 2, D, 2 * F), D),
        'ffn_w_out': dense(ks[4], (DEPTH, 2, F, D), F),
        'sc_w_in': dense(ks[5], (N_A, D, 3 * D), D),
        'sc_conv': dense(ks[6], (N_A, SC_WIDTH, D), SC_WIDTH),
        'sc_w_out': dense(ks[7], (N_A, D, D), D),
        'hy_w_in': dense(ks[8], (N_B, D, 3 * D), D),
        'hy_b_in': small(ks[9], (N_B, 3 * D)),
        'hy_conv': dense(ks[10], (N_B, HY_SHORT, 3 * D), HY_SHORT),
        'hy_conv_b': small(ks[11], (N_B, 3 * D)),
        'hy_f_w1': dense(ks[12], (N_B, HY_EMB, HY_ORDER), HY_EMB),
        'hy_f_b1': small(ks[13], (N_B, HY_ORDER), 0.1),
        'hy_f_w2': dense(ks[14], (N_B, HY_ORDER, HY_ORDER), HY_ORDER),
        'hy_f_b2': small(ks[15], (N_B, HY_ORDER), 0.1),
        'hy_f_w3': dense(ks[16], (N_B, HY_ORDER, 2 * D), HY_ORDER, 0.03),
        'hy_f_freq': 1.0 + small(ks[17], (N_B, HY_ORDER)),
        'hy_d': small(ks[18], (N_B, D), 0.5),
        'hy_w_out': dense(ks[19], (N_B, D, D), D),
        'hy_b_out': small(ks[20], (N_B, D)),
        'gd_w_in': dense(ks[21], (N_C, D, GD_QKV + H * GD_DV + 4 * H), D),
        'gd_conv': dense(ks[22], (N_C, GD_CONV, GD_QKV), GD_CONV),
        'gd_a_log': jnp.log(jax.random.uniform(ks[23], (N_C, 2, H), f32, 1.0, 16.0)),
        'gd_dt_bias': dt + jnp.log(-jnp.expm1(-dt)),
        'gd_norm': 1.0 + small(ks[24], (N_C, GD_DV)),
        'gd_w_out': dense(ks[26], (N_C, H * GD_DV, D), H * GD_DV),
    }


def reference(x, norms, final_norm, ffn_w_in, ffn_w_out,
              sc_w_in, sc_conv, sc_w_out,
              hy_w_in, hy_b_in, hy_conv, hy_conv_b, hy_f_w1, hy_f_b1, hy_f_w2, hy_f_b2,
              hy_f_w3, hy_f_freq, hy_d, hy_w_out, hy_b_out,
              gd_w_in, gd_conv, gd_a_log, gd_dt_bias, gd_norm, gd_w_out):
    for i in range(DEPTH):
        m, j = i % N_MIXERS, i // N_MIXERS
        x = x + 0.5 * swiglu(rms_norm(x, norms[i, 0]), ffn_w_in[i, 0], ffn_w_out[i, 0])
        hn = rms_norm(x, norms[i, 1])
        if m == 0:
            y = short_conv_mixer(hn, sc_w_in[j], sc_conv[j], sc_w_out[j])
        elif m == 1:
            y = hyena_mixer(hn, hy_w_in[j], hy_b_in[j], hy_conv[j], hy_conv_b[j],
                            hy_f_w1[j], hy_f_b1[j], hy_f_w2[j], hy_f_b2[j], hy_f_w3[j],
                            hy_f_freq[j], hy_d[j], hy_w_out[j], hy_b_out[j])
        else:
            y = gated_deltanet_mixer(hn, gd_w_in[j], gd_conv[j], gd_a_log[j], gd_dt_bias[j],
                                     gd_norm[j], gd_w_out[j])
        x = x + y
        x = x + 0.5 * swiglu(rms_norm(x, norms[i, 2]), ffn_w_in[i, 1], ffn_w_out[i, 1])
    return rms_norm(x, final_norm)
```

```python
import functools
import math

import jax
import jax.numpy as jnp
from jax import lax
from jax.experimental import pallas as pl
from jax.experimental.pallas import tpu as pltpu

F32 = jnp.float32
BF16 = jnp.bfloat16

NORM_EPS = 1e-6
N_MIXERS = 3
HY_EMB = 33
HY_BANDS = (HY_EMB - 1) // 2
HY_MAX_DECAY = math.log(1e-2) / 0.3
HY_MIN_DECAY = math.log(1e-2) / 1.5
GD_HEADS = 8
GD_CHUNK = 64

V7X_VMEM_LIMIT_BYTES = 56 * 1024 * 1024
FFN_ROW_TILE = 512
FFN_HIDDEN_TILE = 256


def _ffn_kernel(x_ref, gain_ref, win_ref, wout_ref, o_ref, hid_ref, *, n_hidden_tiles, hidden_tile):
    x = x_ref[...]
    ms = jnp.mean(x * x, axis=-1, keepdims=True)
    xn = (x * lax.rsqrt(ms + NORM_EPS) * gain_ref[...]).astype(BF16)
    for j in range(n_hidden_tiles):
        gu = jnp.dot(xn, win_ref[j], preferred_element_type=F32)
        g = gu[:, :hidden_tile]
        u = gu[:, hidden_tile:]
        hid_ref[:, j * hidden_tile:(j + 1) * hidden_tile] = (g * jax.nn.sigmoid(g) * u).astype(BF16)
    y = jnp.dot(hid_ref[...], wout_ref[...], preferred_element_type=F32)
    o_ref[...] = x + 0.5 * y


def _prep_ffn_weights(w_in, w_out):
    d, f2 = w_in.shape
    f = f2 // 2
    tf = FFN_HIDDEN_TILE
    nf = f // tf
    wg = w_in[:, :f].reshape(d, nf, tf)
    wu = w_in[:, f:].reshape(d, nf, tf)
    w = jnp.concatenate([wg, wu], axis=-1)
    return jnp.transpose(w, (1, 0, 2)).astype(BF16), w_out.astype(BF16)


def _ffn_block(x2d, gain, w_in, w_out):
    t, d = x2d.shape
    win_b, wout_b = _prep_ffn_weights(w_in, w_out)
    nf, _, tf2 = win_b.shape
    f = wout_b.shape[0]
    tm = FFN_ROW_TILE
    kern = functools.partial(_ffn_kernel, n_hidden_tiles=nf, hidden_tile=tf2 // 2)
    resident = dict(pipeline_mode=pl.Buffered(1))
    return pl.pallas_call(
        kern,
        grid=(t // tm,),
        in_specs=[
            pl.BlockSpec((tm, d), lambda i: (i, 0)),
            pl.BlockSpec((1, d), lambda i: (0, 0)),
            pl.BlockSpec((nf, d, tf2), lambda i: (0, 0, 0), **resident),
            pl.BlockSpec((f, d), lambda i: (0, 0), **resident),
        ],
        out_specs=pl.BlockSpec((tm, d), lambda i: (i, 0)),
        out_shape=jax.ShapeDtypeStruct((t, d), F32),
        scratch_shapes=[pltpu.VMEM((tm, f), BF16)],
        compiler_params=pltpu.CompilerParams(
            dimension_semantics=("arbitrary",), vmem_limit_bytes=V7X_VMEM_LIMIT_BYTES),
        name="ffn_block",
    )(x2d, gain.reshape(1, d), win_b, wout_b)


def _rms_norm(x, g):
    y = x * lax.rsqrt(jnp.mean(x * x, axis=-1, keepdims=True) + NORM_EPS)
    return y * g


def _dwconv(x, w):
    k = w.shape[0]
    p = k // 2
    l = x.shape[1]
    xp = jnp.pad(x, ((0, 0), (p, p), (0, 0)))
    return sum(xp[:, j:j + l] * w[j] for j in range(k))


def _short_conv_mixer(x, w_in, conv_w, w_out):
    b, c, h = jnp.split(x @ w_in, 3, axis=-1)
    return (b * _dwconv(c * h, conv_w)) @ w_out


def _hyena_filters(l, w1, b1, w2, b2, w3, freq):
    t = jnp.linspace(0.0, 1.0, l, dtype=F32)[:, None]
    w = 2.0 * math.pi * jnp.arange(l, dtype=F32)[:, None] / l
    f = jnp.linspace(1e-4, HY_BANDS - 1, HY_BANDS, dtype=F32)[None, :]
    z = jnp.concatenate([t, jnp.cos(f * w), -jnp.sin(f * w)], axis=-1)
    h = jnp.sin(freq * (z @ w1 + b1))
    h = jnp.sin(freq * (h @ w2 + b2))
    h = h @ w3
    d = h.shape[-1] // 2
    deltas = jnp.linspace(HY_MIN_DECAY, HY_MAX_DECAY, d, dtype=F32)
    h = h * jnp.exp(-t * jnp.abs(jnp.concatenate([deltas, deltas])))
    return h[:, :d], h[:, d:]


def _two_sided_fft_conv(u, h_f, h_b):
    l = u.shape[1]
    zero = jnp.zeros((1, h_f.shape[1]), h_f.dtype)
    h_circ = jnp.concatenate([h_f, zero, h_b[1:][::-1]], axis=0)
    uf = jnp.fft.rfft(u, n=2 * l, axis=1)
    hf = jnp.fft.rfft(h_circ, axis=0)
    return jnp.fft.irfft(uf * hf[None], n=2 * l, axis=1)[:, :l]


def _hyena_mixer(x, w_in, b_in, conv_w, conv_b, f_w1, f_b1, f_w2, f_b2, f_w3, f_freq, d_bias, w_out, b_out):
    l = x.shape[1]
    u = _dwconv(x @ w_in + b_in, conv_w) + conv_b
    x0, x1, v = jnp.split(u, 3, axis=-1)
    h_f, h_b = _hyena_filters(l, f_w1, f_b1, f_w2, f_b2, f_w3, f_freq)
    v = v * x1
    y = _two_sided_fft_conv(v, h_f, h_b) + v * d_bias
    return (y * x0) @ w_out + b_out


def _chunk_gated_delta_rule(q, k, v, g, beta):
    bt, l, h, dk = q.shape
    dv = v.shape[-1]
    c = GD_CHUNK
    n = l // c

    def blocks(t):
        t = t.reshape((bt, n, c, h) + t.shape[3:])
        return jnp.moveaxis(t, 3, 1)

    q, k, v, g, beta = blocks(q), blocks(k), blocks(v), blocks(g), blocks(beta)
    gc = jnp.cumsum(g, axis=-1)
    idx = jnp.arange(c)
    incl = idx[:, None] >= idx[None, :]
    strict = idx[:, None] > idx[None, :]
    decay = jnp.exp(jnp.where(incl, gc[..., :, None] - gc[..., None, :], -jnp.inf))
    kb = k * beta[..., None]
    vb = v * beta[..., None]
    lower = jnp.where(strict, jnp.einsum('bhncd,bhnsd->bhncs', kb, k) * decay, 0.0)
    eye = jnp.eye(c, dtype=q.dtype)
    rhs = jnp.concatenate([vb, kb * jnp.exp(gc)[..., None]], axis=-1)
    sol = lax.linalg.triangular_solve(lower + eye, rhs, left_side=True, lower=True, unit_diagonal=True)
    u, w = sol[..., :dv], sol[..., dv:]
    attn = jnp.einsum('bhncd,bhnsd->bhncs', q, k) * decay
    q_dec = q * jnp.exp(gc)[..., None]
    k_dec = k * jnp.exp(gc[..., -1:] - gc)[..., None]
    g_tot = jnp.exp(gc[..., -1])

    def step(s, inp):
        u_n, w_n, a_n, qd_n, kd_n, gt_n = inp
        v_new = u_n - jnp.einsum('bhcd,bhde->bhce', w_n, s)
        o_n = jnp.einsum('bhcd,bhde->bhce', qd_n, s) + jnp.einsum('bhcs,bhse->bhce', a_n, v_new)
        s = s * gt_n[..., None, None] + jnp.einsum('bhcd,bhce->bhde', kd_n, v_new)
        return s, o_n

    xs = tuple(jnp.moveaxis(t, 2, 0) for t in (u, w, attn, q_dec, k_dec, g_tot))
    s0 = jnp.zeros((bt, h, dk, dv), F32)
    _, o = lax.scan(step, s0, xs)
    o = jnp.moveaxis(o, 0, 2)
    return jnp.moveaxis(o, 1, 3).reshape(bt, l, h, dv)


def _l2norm(t):
    return t * lax.rsqrt(jnp.sum(t * t, axis=-1, keepdims=True) + 1e-6)


def _gated_deltanet_mixer(x, w_in, conv_w, a_log, dt_bias, norm_g, w_out):
    b_, l, d = x.shape
    h = GD_HEADS
    dk = d // h
    dv = d // h
    qkv_w = 2 * h * dk + h * dv
    proj = x @ w_in
    qkv, z, a, b = jnp.split(proj, [qkv_w, qkv_w + h * dv, qkv_w + h * dv + 2 * h], axis=-1)
    qkv = jax.nn.silu(_dwconv(qkv, conv_w))
    q, k, v = jnp.split(qkv, [h * dk, 2 * h * dk], axis=-1)
    q = _l2norm(q.reshape(b_, l, h, dk)) * (dk ** -0.5)
    k = _l2norm(k.reshape(b_, l, h, dk))
    v = v.reshape(b_, l, h, dv)
    a = a.reshape(b_, l, 2, h)
    b = b.reshape(b_, l, 2, h)
    g = -jnp.exp(a_log) * jax.nn.softplus(a + dt_bias)
    beta = jax.nn.sigmoid(b)
    flip = lambda t: jnp.flip(t, axis=1)
    q2 = jnp.concatenate([q, flip(q)], axis=0)
    k2 = jnp.concatenate([k, flip(k)], axis=0)
    v2 = jnp.concatenate([v, flip(v)], axis=0)
    g2 = jnp.concatenate([g[:, :, 0], flip(g[:, :, 1])], axis=0)
    b2 = jnp.concatenate([beta[:, :, 0], flip(beta[:, :, 1])], axis=0)
    o = _chunk_gated_delta_rule(q2, k2, v2, g2, b2)
    o = o[:b_] + flip(o[b_:])
    o = _rms_norm(o, norm_g) * jax.nn.silu(z.reshape(b_, l, h, dv))
    return o.reshape(b_, l, h * dv) @ w_out


def kernel(x, norms, final_norm, ffn_w_in, ffn_w_out, sc_w_in, sc_conv, sc_w_out, hy_w_in, hy_b_in, hy_conv, hy_conv_b, hy_f_w1, hy_f_b1, hy_f_w2, hy_f_b2, hy_f_w3, hy_f_freq, hy_d, hy_w_out, hy_b_out, gd_w_in, gd_conv, gd_a_log, gd_dt_bias, gd_norm, gd_w_out):
    b, l, d = x.shape
    depth = norms.shape[0]

    def ffn(xx, i, k):
        return _ffn_block(xx.reshape(b * l, d), norms[i, 2 * k], ffn_w_in[i, k], ffn_w_out[i, k]).reshape(b, l, d)

    for i in range(depth):
        m, j = i % N_MIXERS, i // N_MIXERS
        x = ffn(x, i, 0)
        hn = _rms_norm(x, norms[i, 1])
        if m == 0:
            y = _short_conv_mixer(hn, sc_w_in[j], sc_conv[j], sc_w_out[j])
        elif m == 1:
            y = _hyena_mixer(hn, hy_w_in[j], hy_b_in[j], hy_conv[j], hy_conv_b[j],
                             hy_f_w1[j], hy_f_b1[j], hy_f_w2[j], hy_f_b2[j], hy_f_w3[j],
                             hy_f_freq[j], hy_d[j], hy_w_out[j], hy_b_out[j])
        else:
            y = _gated_deltanet_mixer(hn, gd_w_in[j], gd_conv[j], gd_a_log[j], gd_dt_bias[j],
                                      gd_norm[j], gd_w_out[j])
        x = x + y
        x = ffn(x, i, 1)
    return _rms_norm(x, final_norm)
```

```python
import functools
import math

import jax
import jax.numpy as jnp
from jax import lax
from jax.experimental import pallas as pl
from jax.experimental.pallas import tpu as pltpu

F32 = jnp.float32
BF16 = jnp.bfloat16

NORM_EPS = 1e-6
N_MIXERS = 3
HY_EMB = 33
HY_BANDS = (HY_EMB - 1) // 2
HY_MAX_DECAY = math.log(1e-2) / 0.3
HY_MIN_DECAY = math.log(1e-2) / 1.5
GD_HEADS = 8
GD_CHUNK = 64

V7X_VMEM_LIMIT_BYTES = 56 * 1024 * 1024
FFN_ROW_TILE = 512
FFN_HIDDEN_TILE = 256


def _ffn_kernel(x_ref, gain_ref, win_ref, wout_ref, o_ref, hid_ref, *, n_hidden_tiles, hidden_tile):
    x = x_ref[...]
    ms = jnp.mean(x * x, axis=-1, keepdims=True)
    xn = (x * lax.rsqrt(ms + NORM_EPS) * gain_ref[...]).astype(BF16)
    for j in range(n_hidden_tiles):
        gu = jnp.dot(xn, win_ref[j], preferred_element_type=F32)
        g = gu[:, :hidden_tile]
        u = gu[:, hidden_tile:]
        hid_ref[:, j * hidden_tile:(j + 1) * hidden_tile] = (g * jax.nn.sigmoid(g) * u).astype(BF16)
    y = jnp.dot(hid_ref[...], wout_ref[...], preferred_element_type=F32)
    o_ref[...] = x + 0.5 * y


def _prep_ffn_weights(w_in, w_out):
    d, f2 = w_in.shape
    f = f2 // 2
    tf = FFN_HIDDEN_TILE
    nf = f // tf
    wg = w_in[:, :f].reshape(d, nf, tf)
    wu = w_in[:, f:].reshape(d, nf, tf)
    w = jnp.concatenate([wg, wu], axis=-1)
    return jnp.transpose(w, (1, 0, 2)).astype(BF16), w_out.astype(BF16)


def _ffn_block(x2d, gain, w_in, w_out):
    t, d = x2d.shape
    win_b, wout_b = _prep_ffn_weights(w_in, w_out)
    nf, _, tf2 = win_b.shape
    f = wout_b.shape[0]
    tm = FFN_ROW_TILE
    kern = functools.partial(_ffn_kernel, n_hidden_tiles=nf, hidden_tile=tf2 // 2)
    resident = dict(pipeline_mode=pl.Buffered(1))
    return pl.pallas_call(
        kern,
        grid=(t // tm,),
        in_specs=[
            pl.BlockSpec((tm, d), lambda i: (i, 0)),
            pl.BlockSpec((1, d), lambda i: (0, 0)),
            pl.BlockSpec((nf, d, tf2), lambda i: (0, 0, 0), **resident),
            pl.BlockSpec((f, d), lambda i: (0, 0), **resident),
        ],
        out_specs=pl.BlockSpec((tm, d), lambda i: (i, 0)),
        out_shape=jax.ShapeDtypeStruct((t, d), F32),
        scratch_shapes=[pltpu.VMEM((tm, f), BF16)],
        compiler_params=pltpu.CompilerParams(
            dimension_semantics=("arbitrary",), vmem_limit_bytes=V7X_VMEM_LIMIT_BYTES),
        name="ffn_block",
    )(x2d, gain.reshape(1, d), win_b, wout_b)


MIX_ROW_TILE = 512
HALO = 8


def _rms_rows(x, gain):
    ms = jnp.mean(x * x, axis=-1, keepdims=True)
    return x * lax.rsqrt(ms + NORM_EPS) * gain


def _halo_specs(tm, d, seq_len):
    per = tm // HALO
    last_blk = seq_len // HALO - 1
    return [
        pl.BlockSpec((None, HALO, d), lambda b, i: (b, jnp.maximum(i * per - 1, 0), 0)),
        pl.BlockSpec((None, tm, d), lambda b, i: (b, i, 0)),
        pl.BlockSpec((None, HALO, d), lambda b, i: (b, jnp.minimum((i + 1) * per, last_blk), 0)),
    ]


def _store_with_halo(s_ref, main, halo, tm):
    i = pl.program_id(1)
    last = pl.num_programs(1) - 1
    s_ref[0:HALO, :] = jnp.where(i > 0, halo[:HALO], 0.0)
    s_ref[HALO:HALO + tm, :] = main
    s_ref[HALO + tm:2 * HALO + tm, :] = jnp.where(i < last, halo[HALO:], 0.0)


def _conv3_from(s_ref, cw, tm):
    return (cw[0:1] * s_ref[HALO - 1:HALO - 1 + tm, :] + cw[1:2] * s_ref[HALO:HALO + tm, :]
            + cw[2:3] * s_ref[HALO + 1:HALO + 1 + tm, :])


_RESIDENT = dict(pipeline_mode=pl.Buffered(1))


def _mixer_params():
    return pltpu.CompilerParams(dimension_semantics=("arbitrary", "arbitrary"),
                                vmem_limit_bytes=V7X_VMEM_LIMIT_BYTES)


def _sc_mixer_kernel(xp_ref, x_ref, xn_ref, gain_ref, win_ref, cw_ref, wout_ref, o_ref, ch_ref, *, tm, d):
    gain = gain_ref[...]
    x = x_ref[...]
    h = _rms_rows(x, gain).astype(BF16)
    c_main = jnp.dot(h, win_ref[:, d:2 * d], preferred_element_type=F32)
    h_main = jnp.dot(h, win_ref[:, 2 * d:], preferred_element_type=F32)
    hh = _rms_rows(jnp.concatenate([xp_ref[...], xn_ref[...]], axis=0), gain).astype(BF16)
    c_halo = jnp.dot(hh, win_ref[:, d:2 * d], preferred_element_type=F32)
    h_halo = jnp.dot(hh, win_ref[:, 2 * d:], preferred_element_type=F32)
    _store_with_halo(ch_ref, c_main * h_main, c_halo * h_halo, tm)
    conv = _conv3_from(ch_ref, cw_ref[...], tm)
    b_main = jnp.dot(h, win_ref[:, :d], preferred_element_type=F32)
    y = jnp.dot((b_main * conv).astype(BF16), wout_ref[...], preferred_element_type=F32)
    o_ref[...] = x + y


def _sc_mixer_block(x, gain, w_in, conv_w, w_out):
    b, l, d = x.shape
    tm = MIX_ROW_TILE
    kern = functools.partial(_sc_mixer_kernel, tm=tm, d=d)
    return pl.pallas_call(
        kern,
        grid=(b, l // tm),
        in_specs=_halo_specs(tm, d, l) + [
            pl.BlockSpec((1, d), lambda bb, i: (0, 0)),
            pl.BlockSpec((d, 3 * d), lambda bb, i: (0, 0), **_RESIDENT),
            pl.BlockSpec((3, d), lambda bb, i: (0, 0)),
            pl.BlockSpec((d, d), lambda bb, i: (0, 0), **_RESIDENT),
        ],
        out_specs=pl.BlockSpec((None, tm, d), lambda bb, i: (bb, i, 0)),
        out_shape=jax.ShapeDtypeStruct((b, l, d), F32),
        scratch_shapes=[pltpu.VMEM((tm + 2 * HALO, d), F32)],
        compiler_params=_mixer_params(),
        name="sc_mixer",
    )(x, x, x, gain.reshape(1, d), w_in.astype(BF16), conv_w, w_out.astype(BF16))


def _hy_in_kernel(xp_ref, x_ref, xn_ref, gain_ref, win_ref, bin_ref, cw_ref, cb_ref, x0_ref, vxt_ref, s_ref,
                  *, tm, d):
    gain = gain_ref[...]
    h = _rms_rows(x_ref[...], gain).astype(BF16)
    hh = _rms_rows(jnp.concatenate([xp_ref[...], xn_ref[...]], axis=0), gain).astype(BF16)
    parts = []
    for j in range(3):
        cols = slice(j * d, (j + 1) * d)
        bias = bin_ref[:, cols]
        main = jnp.dot(h, win_ref[:, cols], preferred_element_type=F32) + bias
        halo = jnp.dot(hh, win_ref[:, cols], preferred_element_type=F32) + bias
        _store_with_halo(s_ref, main, halo, tm)
        parts.append(_conv3_from(s_ref, cw_ref[:, cols], tm) + cb_ref[:, cols])
    x0_ref[...] = parts[0]
    vxt_ref[...] = (parts[2] * parts[1]).T


def _hy_in_block(x, gain, w_in, b_in, conv_w, conv_b):
    b, l, d = x.shape
    tm = MIX_ROW_TILE
    kern = functools.partial(_hy_in_kernel, tm=tm, d=d)
    return pl.pallas_call(
        kern,
        grid=(b, l // tm),
        in_specs=_halo_specs(tm, d, l) + [
            pl.BlockSpec((1, d), lambda bb, i: (0, 0)),
            pl.BlockSpec((d, 3 * d), lambda bb, i: (0, 0), **_RESIDENT),
            pl.BlockSpec((1, 3 * d), lambda bb, i: (0, 0)),
            pl.BlockSpec((3, 3 * d), lambda bb, i: (0, 0)),
            pl.BlockSpec((1, 3 * d), lambda bb, i: (0, 0)),
        ],
        out_specs=[pl.BlockSpec((None, tm, d), lambda bb, i: (bb, i, 0)),
                   pl.BlockSpec((None, d, tm), lambda bb, i: (bb, 0, i))],
        out_shape=[jax.ShapeDtypeStruct((b, l, d), F32), jax.ShapeDtypeStruct((b, d, l), F32)],
        scratch_shapes=[pltpu.VMEM((tm + 2 * HALO, d), F32)],
        compiler_params=_mixer_params(),
        name="hyena_in",
    )(x, x, x, gain.reshape(1, d), w_in.astype(BF16), b_in.reshape(1, 3 * d), conv_w, conv_b.reshape(1, 3 * d))


HY_FEAT_PAD = 64
HY_FILTER_TILE = 2048


def _hy_filter_kernel(zt_ref, w1t_ref, b1_ref, w2t_ref, b2_ref, freq_ref, w3t_ref, delta_ref, hc_ref):
    hi = lax.Precision.HIGHEST
    z = zt_ref[...]
    freq = freq_ref[...]
    a1 = jnp.dot(w1t_ref[...], z, precision=hi, preferred_element_type=F32) + b1_ref[...]
    h1 = jnp.sin(freq * a1)
    a2 = jnp.dot(w2t_ref[...], h1, precision=hi, preferred_element_type=F32) + b2_ref[...]
    h2 = jnp.sin(freq * a2)
    h = jnp.dot(w3t_ref[...], h2, precision=hi, preferred_element_type=F32)
    t_row = z[0:1, :]
    mask_row = z[HY_EMB:HY_EMB + 1, :]
    hc_ref[...] = h * jnp.exp(-t_row * delta_ref[...]) * mask_row


def _hy_filter_features(l):
    p = jnp.arange(2 * l)
    pos = jnp.where(p < l, p, 2 * l - p)
    valid = (p != l).astype(F32)
    pos = jnp.minimum(pos, l - 1)
    t = jnp.linspace(0.0, 1.0, l, dtype=F32)[pos][None, :]
    w = (2.0 * math.pi * jnp.arange(l, dtype=F32) / l)[pos][None, :]
    f = jnp.linspace(1e-4, HY_BANDS - 1, HY_BANDS, dtype=F32)[:, None]
    z = jnp.concatenate([t, jnp.cos(f * w), -jnp.sin(f * w), valid[None, :]], axis=0)
    return jnp.pad(z, ((0, HY_FEAT_PAD - z.shape[0]), (0, 0)))


def _hy_filter_block(l, w1, b1, w2, b2, w3, freq):
    order = w1.shape[1]
    d = w3.shape[1] // 2
    tl = min(HY_FILTER_TILE, l)
    half = l // tl
    zt = _hy_filter_features(l)
    w1t = jnp.pad(w1.T, ((0, 0), (0, HY_FEAT_PAD - w1.shape[0])))
    w3t = jnp.stack([w3[:, :d].T, w3[:, d:].T])
    delta = jnp.abs(jnp.linspace(HY_MIN_DECAY, HY_MAX_DECAY, d, dtype=F32)).reshape(d, 1)
    col = lambda v: v.reshape(order, 1)
    full = lambda shape: pl.BlockSpec(shape, lambda i: (0,) * len(shape))
    return pl.pallas_call(
        _hy_filter_kernel,
        grid=(2 * half,),
        in_specs=[pl.BlockSpec((HY_FEAT_PAD, tl), lambda i: (0, i)),
                  full((order, HY_FEAT_PAD)), full((order, 1)), full((order, order)), full((order, 1)),
                  full((order, 1)),
                  pl.BlockSpec((None, d, order), lambda i: (i // half, 0, 0)),
                  full((d, 1))],
        out_specs=pl.BlockSpec((d, tl), lambda i: (0, i)),
        out_shape=jax.ShapeDtypeStruct((d, 2 * l), F32),
        compiler_params=pltpu.CompilerParams(dimension_semantics=("arbitrary",),
                                             vmem_limit_bytes=V7X_VMEM_LIMIT_BYTES),
        name="hyena_filter",
    )(zt, w1t, col(b1), w2.T, col(b2), col(freq), w3t, delta)


HY_CONV_CH_TILE = 16


def _dft_tables(n1, n2):
    import numpy as np
    n = n1 * n2
    i1, i2 = np.arange(n1), np.arange(n2)
    a1 = 2.0 * np.pi * np.outer(i1, i1) / n1
    a2 = 2.0 * np.pi * np.outer(i2, i2) / n2
    c1, s1, c2, s2 = np.cos(a1), np.sin(a1), np.cos(a2), np.sin(a2)
    tw = 2.0 * np.pi * np.outer(i2, i1) / n
    f_real = np.concatenate([c1, -s1], axis=1)
    f_cplx = np.block([[c2, -s2], [s2, c2]])
    i_cplx = np.block([[c2, s2], [-s2, c2]])
    i_real = np.concatenate([c1, -s1], axis=0) / n
    as_bf = lambda a: jnp.asarray(a, dtype=F32).astype(BF16)
    as_f = lambda a: jnp.asarray(a, dtype=F32)
    return (as_bf(f_real), as_bf(f_cplx), as_bf(i_cplx), as_bf(i_real),
            as_f(np.cos(tw)), as_f(np.sin(tw)), as_f(np.cos(tw.T)), as_f(np.sin(tw.T)))


def _swap(a):
    return jnp.swapaxes(a, 1, 2)


def _hy_conv_kernel(ut_ref, hc_ref, dbias_ref, freal_ref, fcplx_ref, icplx_ref, ireal_ref, twc_ref, tws_ref,
                    twct_ref, twst_ref, yt_ref, hspec_ref, *, ct, n1, n2):
    rows = ct * n2
    twc = twc_ref[...]
    tws = tws_ref[...]

    def mm(a, m_ref):
        return jnp.dot(a.astype(BF16), m_ref[...], preferred_element_type=F32)

    def forward(x):
        p = mm(_swap(x).reshape(rows, n1), freal_ref)
        pr = p[:, :n1].reshape(ct, n2, n1)
        pi = p[:, n1:].reshape(ct, n2, n1)
        qr = pr * twc + pi * tws
        qi = pi * twc - pr * tws
        q = jnp.concatenate([_swap(qr).reshape(ct * n1, n2), _swap(qi).reshape(ct * n1, n2)], axis=1)
        xs = mm(q, fcplx_ref)
        return xs[:, :n2].reshape(ct, n1, n2), xs[:, n2:].reshape(ct, n1, n2)

    @pl.when(pl.program_id(1) == 0)
    def _():
        hr, hi = forward(hc_ref[...])
        hspec_ref[0] = hr
        hspec_ref[1] = hi

    u = ut_ref[...]
    xr, xi = forward(jnp.concatenate([u, jnp.zeros_like(u)], axis=1))
    hr = hspec_ref[0]
    hi = hspec_ref[1]
    yr = xr * hr - xi * hi
    yi = xr * hi + xi * hr
    r = mm(jnp.concatenate([yr.reshape(ct * n1, n2), yi.reshape(ct * n1, n2)], axis=1), icplx_ref)
    rr = r[:, :n2].reshape(ct, n1, n2)
    ri = r[:, n2:].reshape(ct, n1, n2)
    twct = twct_ref[...]
    twst = twst_ref[...]
    sr = rr * twct - ri * twst
    si = ri * twct + rr * twst
    s = jnp.concatenate([_swap(sr).reshape(rows, n1), _swap(si).reshape(rows, n1)], axis=1)
    z = mm(s, ireal_ref).reshape(ct, n2, n1)
    y = _swap(z)[:, :n1 // 2, :]
    yt_ref[...] = y + u * dbias_ref[...]


def _hy_conv_block(ut, hc, d_bias):
    b, d, l = ut.shape
    n2 = 128
    n1 = 2 * l // n2
    ct = HY_CONV_CH_TILE
    tables = _dft_tables(n1, n2)
    kern = functools.partial(_hy_conv_kernel, ct=ct, n1=n1, n2=n2)
    full = lambda a: pl.BlockSpec(a.shape, lambda c, bb: (0,) * a.ndim)
    yt = pl.pallas_call(
        kern,
        grid=(d // ct, b),
        in_specs=[pl.BlockSpec((None, ct, n1 // 2, n2), lambda c, bb: (bb, c, 0, 0)),
                  pl.BlockSpec((ct, n1, n2), lambda c, bb: (c, 0, 0)),
                  pl.BlockSpec((ct, 1, 1), lambda c, bb: (c, 0, 0))] + [full(t) for t in tables],
        out_specs=pl.BlockSpec((None, ct, n1 // 2, n2), lambda c, bb: (bb, c, 0, 0)),
        out_shape=jax.ShapeDtypeStruct((b, d, n1 // 2, n2), F32),
        scratch_shapes=[pltpu.VMEM((2, ct, n1, n2), F32)],
        compiler_params=_mixer_params(),
        name="hyena_longconv",
    )(ut.reshape(b, d, n1 // 2, n2), hc.reshape(d, n1, n2), d_bias.reshape(d, 1, 1), *tables)
    return yt.reshape(b, d, l)


def _hyena_block(x, gain, w_in, b_in, conv_w, conv_b, f_w1, f_b1, f_w2, f_b2, f_w3, f_freq, d_bias, w_out, b_out):
    l = x.shape[1]
    x0, ut = _hy_in_block(x, gain, w_in, b_in, conv_w, conv_b)
    hc = _hy_filter_block(l, f_w1, f_b1, f_w2, f_b2, f_w3, f_freq)
    yt = _hy_conv_block(ut, hc, d_bias)
    return _hy_out_block(x, x0, yt, w_out, b_out)


def _hy_out_kernel(x_ref, x0_ref, yt_ref, wout_ref, bout_ref, o_ref):
    y = yt_ref[...].T
    z = jnp.dot((y * x0_ref[...]).astype(BF16), wout_ref[...], preferred_element_type=F32)
    o_ref[...] = x_ref[...] + z + bout_ref[...]


def _hy_out_block(x, x0, yt, w_out, b_out):
    b, l, d = x.shape
    tm = MIX_ROW_TILE
    row = pl.BlockSpec((None, tm, d), lambda bb, i: (bb, i, 0))
    return pl.pallas_call(
        _hy_out_kernel,
        grid=(b, l // tm),
        in_specs=[row, row, pl.BlockSpec((None, d, tm), lambda bb, i: (bb, 0, i)),
                  pl.BlockSpec((d, d), lambda bb, i: (0, 0), **_RESIDENT),
                  pl.BlockSpec((1, d), lambda bb, i: (0, 0))],
        out_specs=row,
        out_shape=jax.ShapeDtypeStruct((b, l, d), F32),
        compiler_params=_mixer_params(),
        name="hyena_out",
    )(x, x0, yt, w_out.astype(BF16), b_out.reshape(1, d))


def _rms_norm(x, g):
    y = x * lax.rsqrt(jnp.mean(x * x, axis=-1, keepdims=True) + NORM_EPS)
    return y * g


def _dwconv(x, w):
    k = w.shape[0]
    p = k // 2
    l = x.shape[1]
    xp = jnp.pad(x, ((0, 0), (p, p), (0, 0)))
    return sum(xp[:, j:j + l] * w[j] for j in range(k))


def _short_conv_mixer(x, w_in, conv_w, w_out):
    b, c, h = jnp.split(x @ w_in, 3, axis=-1)
    return (b * _dwconv(c * h, conv_w)) @ w_out


def _hyena_filters(l, w1, b1, w2, b2, w3, freq):
    t = jnp.linspace(0.0, 1.0, l, dtype=F32)[:, None]
    w = 2.0 * math.pi * jnp.arange(l, dtype=F32)[:, None] / l
    f = jnp.linspace(1e-4, HY_BANDS - 1, HY_BANDS, dtype=F32)[None, :]
    z = jnp.concatenate([t, jnp.cos(f * w), -jnp.sin(f * w)], axis=-1)
    h = jnp.sin(freq * (z @ w1 + b1))
    h = jnp.sin(freq * (h @ w2 + b2))
    h = h @ w3
    d = h.shape[-1] // 2
    deltas = jnp.linspace(HY_MIN_DECAY, HY_MAX_DECAY, d, dtype=F32)
    h = h * jnp.exp(-t * jnp.abs(jnp.concatenate([deltas, deltas])))
    return h[:, :d], h[:, d:]


def _two_sided_fft_conv(u, h_f, h_b):
    l = u.shape[1]
    zero = jnp.zeros((1, h_f.shape[1]), h_f.dtype)
    h_circ = jnp.concatenate([h_f, zero, h_b[1:][::-1]], axis=0)
    uf = jnp.fft.rfft(u, n=2 * l, axis=1)
    hf = jnp.fft.rfft(h_circ, axis=0)
    return jnp.fft.irfft(uf * hf[None], n=2 * l, axis=1)[:, :l]


def _hyena_mixer(x, w_in, b_in, conv_w, conv_b, f_w1, f_b1, f_w2, f_b2, f_w3, f_freq, d_bias, w_out, b_out):
    l = x.shape[1]
    u = _dwconv(x @ w_in + b_in, conv_w) + conv_b
    x0, x1, v = jnp.split(u, 3, axis=-1)
    h_f, h_b = _hyena_filters(l, f_w1, f_b1, f_w2, f_b2, f_w3, f_freq)
    v = v * x1
    y = _two_sided_fft_conv(v, h_f, h_b) + v * d_bias
    return (y * x0) @ w_out + b_out


def _chunk_gated_delta_rule(q, k, v, g, beta):
    bt, l, h, dk = q.shape
    dv = v.shape[-1]
    c = GD_CHUNK
    n = l // c

    def blocks(t):
        t = t.reshape((bt, n, c, h) + t.shape[3:])
        return jnp.moveaxis(t, 3, 1)

    q, k, v, g, beta = blocks(q), blocks(k), blocks(v), blocks(g), blocks(beta)
    gc = jnp.cumsum(g, axis=-1)
    idx = jnp.arange(c)
    incl = idx[:, None] >= idx[None, :]
    strict = idx[:, None] > idx[None, :]
    decay = jnp.exp(jnp.where(incl, gc[..., :, None] - gc[..., None, :], -jnp.inf))
    kb = k * beta[..., None]
    vb = v * beta[..., None]
    lower = jnp.where(strict, jnp.einsum('bhncd,bhnsd->bhncs', kb, k) * decay, 0.0)
    eye = jnp.eye(c, dtype=q.dtype)
    rhs = jnp.concatenate([vb, kb * jnp.exp(gc)[..., None]], axis=-1)
    sol = lax.linalg.triangular_solve(lower + eye, rhs, left_side=True, lower=True, unit_diagonal=True)
    u, w = sol[..., :dv], sol[..., dv:]
    attn = jnp.einsum('bhncd,bhnsd->bhncs', q, k) * decay
    q_dec = q * jnp.exp(gc)[..., None]
    k_dec = k * jnp.exp(gc[..., -1:] - gc)[..., None]
    g_tot = jnp.exp(gc[..., -1])

    def step(s, inp):
        u_n, w_n, a_n, qd_n, kd_n, gt_n = inp
        v_new = u_n - jnp.einsum('bhcd,bhde->bhce', w_n, s)
        o_n = jnp.einsum('bhcd,bhde->bhce', qd_n, s) + jnp.einsum('bhcs,bhse->bhce', a_n, v_new)
        s = s * gt_n[..., None, None] + jnp.einsum('bhcd,bhce->bhde', kd_n, v_new)
        return s, o_n

    xs = tuple(jnp.moveaxis(t, 2, 0) for t in (u, w, attn, q_dec, k_dec, g_tot))
    s0 = jnp.zeros((bt, h, dk, dv), F32)
    _, o = lax.scan(step, s0, xs)
    o = jnp.moveaxis(o, 0, 2)
    return jnp.moveaxis(o, 1, 3).reshape(bt, l, h, dv)


def _l2norm(t):
    return t * lax.rsqrt(jnp.sum(t * t, axis=-1, keepdims=True) + 1e-6)


def _gated_deltanet_mixer(x, w_in, conv_w, a_log, dt_bias, norm_g, w_out):
    b_, l, d = x.shape
    h = GD_HEADS
    dk = d // h
    dv = d // h
    qkv_w = 2 * h * dk + h * dv
    proj = x @ w_in
    qkv, z, a, b = jnp.split(proj, [qkv_w, qkv_w + h * dv, qkv_w + h * dv + 2 * h], axis=-1)
    qkv = jax.nn.silu(_dwconv(qkv, conv_w))
    q, k, v = jnp.split(qkv, [h * dk, 2 * h * dk], axis=-1)
    q = _l2norm(q.reshape(b_, l, h, dk)) * (dk ** -0.5)
    k = _l2norm(k.reshape(b_, l, h, dk))
    v = v.reshape(b_, l, h, dv)
    a = a.reshape(b_, l, 2, h)
    b = b.reshape(b_, l, 2, h)
    g = -jnp.exp(a_log) * jax.nn.softplus(a + dt_bias)
    beta = jax.nn.sigmoid(b)
    flip = lambda t: jnp.flip(t, axis=1)
    q2 = jnp.concatenate([q, flip(q)], axis=0)
    k2 = jnp.concatenate([k, flip(k)], axis=0)
    v2 = jnp.concatenate([v, flip(v)], axis=0)
    g2 = jnp.concatenate([g[:, :, 0], flip(g[:, :, 1])], axis=0)
    b2 = jnp.concatenate([beta[:, :, 0], flip(beta[:, :, 1])], axis=0)
    o = _chunk_gated_delta_rule(q2, k2, v2, g2, b2)
    o = o[:b_] + flip(o[b_:])
    o = _rms_norm(o, norm_g) * jax.nn.silu(z.reshape(b_, l, h, dv))
    return o.reshape(b_, l, h * dv) @ w_out


def kernel(x, norms, final_norm, ffn_w_in, ffn_w_out, sc_w_in, sc_conv, sc_w_out, hy_w_in, hy_b_in, hy_conv, hy_conv_b, hy_f_w1, hy_f_b1, hy_f_w2, hy_f_b2, hy_f_w3, hy_f_freq, hy_d, hy_w_out, hy_b_out, gd_w_in, gd_conv, gd_a_log, gd_dt_bias, gd_norm, gd_w_out):
    b, l, d = x.shape
    depth = norms.shape[0]

    def ffn(xx, i, k):
        return _ffn_block(xx.reshape(b * l, d), norms[i, 2 * k], ffn_w_in[i, k], ffn_w_out[i, k]).reshape(b, l, d)

    for i in range(depth):
        m, j = i % N_MIXERS, i // N_MIXERS
        x = ffn(x, i, 0)
        if m == 0:
            x = _sc_mixer_block(x, norms[i, 1], sc_w_in[j], sc_conv[j], sc_w_out[j])
        elif m == 1:
            x = _hyena_block(x, norms[i, 1], hy_w_in[j], hy_b_in[j], hy_conv[j], hy_conv_b[j],
                             hy_f_w1[j], hy_f_b1[j], hy_f_w2[j], hy_f_b2[j], hy_f_w3[j],
                             hy_f_freq[j], hy_d[j], hy_w_out[j], hy_b_out[j])
        else:
            hn = _rms_norm(x, norms[i, 1])
            x = x + _gated_deltanet_mixer(hn, gd_w_in[j], gd_conv[j], gd_a_log[j], gd_dt_bias[j],
                                          gd_norm[j], gd_w_out[j])
        x = ffn(x, i, 1)
    return _rms_norm(x, final_norm)
```

```python
import functools
import math

import jax
import jax.numpy as jnp
from jax import lax
from jax.experimental import pallas as pl
from jax.experimental.pallas import tpu as pltpu

F32 = jnp.float32
BF16 = jnp.bfloat16

NORM_EPS = 1e-6
N_MIXERS = 3
HY_EMB = 33
HY_BANDS = (HY_EMB - 1) // 2
HY_MAX_DECAY = math.log(1e-2) / 0.3
HY_MIN_DECAY = math.log(1e-2) / 1.5
GD_HEADS = 8
GD_CHUNK = 64

V7X_VMEM_LIMIT_BYTES = 56 * 1024 * 1024
FFN_ROW_TILE = 512
FFN_HIDDEN_TILE = 256


def _ffn_kernel(x_ref, gain_ref, win_ref, wout_ref, *rest, n_hidden_tiles, hidden_tile, final_norm):
    final_ref, o_ref, hid_ref = rest if final_norm else (None,) + rest
    x = x_ref[...]
    xn = _rms_rows(x, gain_ref[...]).astype(BF16)
    for j in range(n_hidden_tiles):
        gu = jnp.dot(xn, win_ref[j], preferred_element_type=F32)
        g = gu[:, :hidden_tile]
        u = gu[:, hidden_tile:]
        hid_ref[:, j * hidden_tile:(j + 1) * hidden_tile] = (g * jax.nn.sigmoid(g) * u).astype(BF16)
    y = x + 0.5 * jnp.dot(hid_ref[...], wout_ref[...], preferred_element_type=F32)
    o_ref[...] = _rms_rows(y, final_ref[...]) if final_norm else y


def _prep_ffn_weights(w_in, w_out):
    d, f2 = w_in.shape
    f = f2 // 2
    tf = FFN_HIDDEN_TILE
    nf = f // tf
    wg = w_in[:, :f].reshape(d, nf, tf)
    wu = w_in[:, f:].reshape(d, nf, tf)
    w = jnp.concatenate([wg, wu], axis=-1)
    return jnp.transpose(w, (1, 0, 2)).astype(BF16), w_out.astype(BF16)


def _ffn_block(x2d, gain, w_in, w_out, final_gain=None):
    t, d = x2d.shape
    win_b, wout_b = _prep_ffn_weights(w_in, w_out)
    nf, _, tf2 = win_b.shape
    f = wout_b.shape[0]
    tm = FFN_ROW_TILE
    final_norm = final_gain is not None
    kern = functools.partial(_ffn_kernel, n_hidden_tiles=nf, hidden_tile=tf2 // 2, final_norm=final_norm)
    vec = pl.BlockSpec((1, d), lambda i: (0, 0))
    extra = ([vec], [final_gain.reshape(1, d)]) if final_norm else ([], [])
    return pl.pallas_call(
        kern,
        grid=(t // tm,),
        in_specs=[
            pl.BlockSpec((tm, d), lambda i: (i, 0)),
            vec,
            pl.BlockSpec((nf, d, tf2), lambda i: (0, 0, 0), **_RESIDENT),
            pl.BlockSpec((f, d), lambda i: (0, 0), **_RESIDENT),
        ] + extra[0],
        out_specs=pl.BlockSpec((tm, d), lambda i: (i, 0)),
        out_shape=jax.ShapeDtypeStruct((t, d), F32),
        scratch_shapes=[pltpu.VMEM((tm, f), BF16)],
        compiler_params=pltpu.CompilerParams(
            dimension_semantics=("arbitrary",), vmem_limit_bytes=V7X_VMEM_LIMIT_BYTES),
        name="ffn_final" if final_norm else "ffn_block",
    )(x2d, gain.reshape(1, d), win_b, wout_b, *extra[1])


MIX_ROW_TILE = 512
HALO = 8


def _rms_rows(x, gain):
    ms = jnp.mean(x * x, axis=-1, keepdims=True)
    return x * lax.rsqrt(ms + NORM_EPS) * gain


def _halo_specs(tm, d, seq_len):
    per = tm // HALO
    last_blk = seq_len // HALO - 1
    return [
        pl.BlockSpec((None, HALO, d), lambda b, i: (b, jnp.maximum(i * per - 1, 0), 0)),
        pl.BlockSpec((None, tm, d), lambda b, i: (b, i, 0)),
        pl.BlockSpec((None, HALO, d), lambda b, i: (b, jnp.minimum((i + 1) * per, last_blk), 0)),
    ]


def _store_with_halo(s_ref, main, halo, tm):
    i = pl.program_id(1)
    last = pl.num_programs(1) - 1
    s_ref[0:HALO, :] = jnp.where(i > 0, halo[:HALO], 0.0)
    s_ref[HALO:HALO + tm, :] = main
    s_ref[HALO + tm:2 * HALO + tm, :] = jnp.where(i < last, halo[HALO:], 0.0)


def _conv3_from(s_ref, cw, tm):
    return (cw[0:1] * s_ref[HALO - 1:HALO - 1 + tm, :] + cw[1:2] * s_ref[HALO:HALO + tm, :]
            + cw[2:3] * s_ref[HALO + 1:HALO + 1 + tm, :])


_RESIDENT = dict(pipeline_mode=pl.Buffered(1))


def _mixer_params():
    return pltpu.CompilerParams(dimension_semantics=("arbitrary", "arbitrary"),
                                vmem_limit_bytes=V7X_VMEM_LIMIT_BYTES)


def _sc_mixer_kernel(xp_ref, x_ref, xn_ref, gain_ref, win_ref, cw_ref, wout_ref, o_ref, ch_ref, *, tm, d):
    gain = gain_ref[...]
    x = x_ref[...]
    h = _rms_rows(x, gain).astype(BF16)
    c_main = jnp.dot(h, win_ref[:, d:2 * d], preferred_element_type=F32)
    h_main = jnp.dot(h, win_ref[:, 2 * d:], preferred_element_type=F32)
    hh = _rms_rows(jnp.concatenate([xp_ref[...], xn_ref[...]], axis=0), gain).astype(BF16)
    c_halo = jnp.dot(hh, win_ref[:, d:2 * d], preferred_element_type=F32)
    h_halo = jnp.dot(hh, win_ref[:, 2 * d:], preferred_element_type=F32)
    _store_with_halo(ch_ref, c_main * h_main, c_halo * h_halo, tm)
    conv = _conv3_from(ch_ref, cw_ref[...], tm)
    b_main = jnp.dot(h, win_ref[:, :d], preferred_element_type=F32)
    y = jnp.dot((b_main * conv).astype(BF16), wout_ref[...], preferred_element_type=F32)
    o_ref[...] = x + y


def _sc_mixer_block(x, gain, w_in, conv_w, w_out):
    b, l, d = x.shape
    tm = MIX_ROW_TILE
    kern = functools.partial(_sc_mixer_kernel, tm=tm, d=d)
    return pl.pallas_call(
        kern,
        grid=(b, l // tm),
        in_specs=_halo_specs(tm, d, l) + [
            pl.BlockSpec((1, d), lambda bb, i: (0, 0)),
            pl.BlockSpec((d, 3 * d), lambda bb, i: (0, 0), **_RESIDENT),
            pl.BlockSpec((3, d), lambda bb, i: (0, 0)),
            pl.BlockSpec((d, d), lambda bb, i: (0, 0), **_RESIDENT),
        ],
        out_specs=pl.BlockSpec((None, tm, d), lambda bb, i: (bb, i, 0)),
        out_shape=jax.ShapeDtypeStruct((b, l, d), F32),
        scratch_shapes=[pltpu.VMEM((tm + 2 * HALO, d), F32)],
        compiler_params=_mixer_params(),
        name="sc_mixer",
    )(x, x, x, gain.reshape(1, d), w_in.astype(BF16), conv_w, w_out.astype(BF16))


def _hy_in_kernel(xp_ref, x_ref, xn_ref, gain_ref, win_ref, bin_ref, cw_ref, cb_ref, x0_ref, vxt_ref, s_ref,
                  *, tm, d):
    gain = gain_ref[...]
    h = _rms_rows(x_ref[...], gain).astype(BF16)
    hh = _rms_rows(jnp.concatenate([xp_ref[...], xn_ref[...]], axis=0), gain).astype(BF16)
    parts = []
    for j in range(3):
        cols = slice(j * d, (j + 1) * d)
        bias = bin_ref[:, cols]
        main = jnp.dot(h, win_ref[:, cols], preferred_element_type=F32) + bias
        halo = jnp.dot(hh, win_ref[:, cols], preferred_element_type=F32) + bias
        _store_with_halo(s_ref, main, halo, tm)
        parts.append(_conv3_from(s_ref, cw_ref[:, cols], tm) + cb_ref[:, cols])
    x0_ref[...] = parts[0]
    vxt_ref[...] = (parts[2] * parts[1]).T


def _hy_in_block(x, gain, w_in, b_in, conv_w, conv_b):
    b, l, d = x.shape
    tm = MIX_ROW_TILE
    kern = functools.partial(_hy_in_kernel, tm=tm, d=d)
    return pl.pallas_call(
        kern,
        grid=(b, l // tm),
        in_specs=_halo_specs(tm, d, l) + [
            pl.BlockSpec((1, d), lambda bb, i: (0, 0)),
            pl.BlockSpec((d, 3 * d), lambda bb, i: (0, 0), **_RESIDENT),
            pl.BlockSpec((1, 3 * d), lambda bb, i: (0, 0)),
            pl.BlockSpec((3, 3 * d), lambda bb, i: (0, 0)),
            pl.BlockSpec((1, 3 * d), lambda bb, i: (0, 0)),
        ],
        out_specs=[pl.BlockSpec((None, tm, d), lambda bb, i: (bb, i, 0)),
                   pl.BlockSpec((None, d, tm), lambda bb, i: (bb, 0, i))],
        out_shape=[jax.ShapeDtypeStruct((b, l, d), F32), jax.ShapeDtypeStruct((b, d, l), F32)],
        scratch_shapes=[pltpu.VMEM((tm + 2 * HALO, d), F32)],
        compiler_params=_mixer_params(),
        name="hyena_in",
    )(x, x, x, gain.reshape(1, d), w_in.astype(BF16), b_in.reshape(1, 3 * d), conv_w, conv_b.reshape(1, 3 * d))


HY_FEAT_PAD = 64
HY_FILTER_TILE = 2048


def _hy_filter_kernel(zt_ref, w1t_ref, b1_ref, w2t_ref, b2_ref, freq_ref, w3t_ref, delta_ref, hc_ref):
    hi = lax.Precision.HIGHEST
    z = zt_ref[...]
    freq = freq_ref[...]
    a1 = jnp.dot(w1t_ref[...], z, precision=hi, preferred_element_type=F32) + b1_ref[...]
    h1 = jnp.sin(freq * a1)
    a2 = jnp.dot(w2t_ref[...], h1, precision=hi, preferred_element_type=F32) + b2_ref[...]
    h2 = jnp.sin(freq * a2)
    h = jnp.dot(w3t_ref[...], h2, precision=hi, preferred_element_type=F32)
    t_row = z[0:1, :]
    mask_row = z[HY_EMB:HY_EMB + 1, :]
    hc_ref[...] = h * jnp.exp(-t_row * delta_ref[...]) * mask_row


def _hy_filter_features(l):
    p = jnp.arange(2 * l)
    pos = jnp.where(p < l, p, 2 * l - p)
    valid = (p != l).astype(F32)
    pos = jnp.minimum(pos, l - 1)
    t = jnp.linspace(0.0, 1.0, l, dtype=F32)[pos][None, :]
    w = (2.0 * math.pi * jnp.arange(l, dtype=F32) / l)[pos][None, :]
    f = jnp.linspace(1e-4, HY_BANDS - 1, HY_BANDS, dtype=F32)[:, None]
    z = jnp.concatenate([t, jnp.cos(f * w), -jnp.sin(f * w), valid[None, :]], axis=0)
    return jnp.pad(z, ((0, HY_FEAT_PAD - z.shape[0]), (0, 0)))


def _hy_filter_block(l, w1, b1, w2, b2, w3, freq):
    order = w1.shape[1]
    d = w3.shape[1] // 2
    tl = min(HY_FILTER_TILE, l)
    half = l // tl
    zt = _hy_filter_features(l)
    w1t = jnp.pad(w1.T, ((0, 0), (0, HY_FEAT_PAD - w1.shape[0])))
    w3t = jnp.stack([w3[:, :d].T, w3[:, d:].T])
    delta = jnp.abs(jnp.linspace(HY_MIN_DECAY, HY_MAX_DECAY, d, dtype=F32)).reshape(d, 1)
    col = lambda v: v.reshape(order, 1)
    full = lambda shape: pl.BlockSpec(shape, lambda i: (0,) * len(shape))
    return pl.pallas_call(
        _hy_filter_kernel,
        grid=(2 * half,),
        in_specs=[pl.BlockSpec((HY_FEAT_PAD, tl), lambda i: (0, i)),
                  full((order, HY_FEAT_PAD)), full((order, 1)), full((order, order)), full((order, 1)),
                  full((order, 1)),
                  pl.BlockSpec((None, d, order), lambda i: (i // half, 0, 0)),
                  full((d, 1))],
        out_specs=pl.BlockSpec((d, tl), lambda i: (0, i)),
        out_shape=jax.ShapeDtypeStruct((d, 2 * l), F32),
        compiler_params=pltpu.CompilerParams(dimension_semantics=("arbitrary",),
                                             vmem_limit_bytes=V7X_VMEM_LIMIT_BYTES),
        name="hyena_filter",
    )(zt, w1t, col(b1), w2.T, col(b2), col(freq), w3t, delta)


HY_CONV_CH_TILE = 16


def _dft_tables(n1, n2):
    import numpy as np
    n = n1 * n2
    i1, i2 = np.arange(n1), np.arange(n2)
    a1 = 2.0 * np.pi * np.outer(i1, i1) / n1
    a2 = 2.0 * np.pi * np.outer(i2, i2) / n2
    c1, s1, c2, s2 = np.cos(a1), np.sin(a1), np.cos(a2), np.sin(a2)
    tw = 2.0 * np.pi * np.outer(i2, i1) / n
    f_real = np.concatenate([c1, -s1], axis=1)
    f_cplx = np.block([[c2, -s2], [s2, c2]])
    i_cplx = np.block([[c2, s2], [-s2, c2]])
    i_real = np.concatenate([c1, -s1], axis=0) / n
    as_bf = lambda a: jnp.asarray(a, dtype=F32).astype(BF16)
    as_f = lambda a: jnp.asarray(a, dtype=F32)
    return (as_bf(f_real), as_bf(f_cplx), as_bf(i_cplx), as_bf(i_real),
            as_f(np.cos(tw)), as_f(np.sin(tw)), as_f(np.cos(tw.T)), as_f(np.sin(tw.T)))


def _swap(a):
    return jnp.swapaxes(a, 1, 2)


def _hy_conv_kernel(ut_ref, hc_ref, dbias_ref, freal_ref, fcplx_ref, icplx_ref, ireal_ref, twc_ref, tws_ref,
                    twct_ref, twst_ref, yt_ref, hspec_ref, *, ct, n1, n2):
    rows = ct * n2
    twc = twc_ref[...]
    tws = tws_ref[...]

    def mm(a, m_ref):
        return jnp.dot(a.astype(BF16), m_ref[...], preferred_element_type=F32)

    def forward(x):
        p = mm(_swap(x).reshape(rows, n1), freal_ref)
        pr = p[:, :n1].reshape(ct, n2, n1)
        pi = p[:, n1:].reshape(ct, n2, n1)
        qr = pr * twc + pi * tws
        qi = pi * twc - pr * tws
        q = jnp.concatenate([_swap(qr).reshape(ct * n1, n2), _swap(qi).reshape(ct * n1, n2)], axis=1)
        xs = mm(q, fcplx_ref)
        return xs[:, :n2].reshape(ct, n1, n2), xs[:, n2:].reshape(ct, n1, n2)

    @pl.when(pl.program_id(1) == 0)
    def _():
        hr, hi = forward(hc_ref[...])
        hspec_ref[0] = hr
        hspec_ref[1] = hi

    u = ut_ref[...]
    xr, xi = forward(jnp.concatenate([u, jnp.zeros_like(u)], axis=1))
    hr = hspec_ref[0]
    hi = hspec_ref[1]
    yr = xr * hr - xi * hi
    yi = xr * hi + xi * hr
    r = mm(jnp.concatenate([yr.reshape(ct * n1, n2), yi.reshape(ct * n1, n2)], axis=1), icplx_ref)
    rr = r[:, :n2].reshape(ct, n1, n2)
    ri = r[:, n2:].reshape(ct, n1, n2)
    twct = twct_ref[...]
    twst = twst_ref[...]
    sr = rr * twct - ri * twst
    si = ri * twct + rr * twst
    s = jnp.concatenate([_swap(sr).reshape(rows, n1), _swap(si).reshape(rows, n1)], axis=1)
    z = mm(s, ireal_ref).reshape(ct, n2, n1)
    y = _swap(z)[:, :n1 // 2, :]
    yt_ref[...] = y + u * dbias_ref[...]


def _hy_conv_block(ut, hc, d_bias):
    b, d, l = ut.shape
    n2 = 128
    n1 = 2 * l // n2
    ct = HY_CONV_CH_TILE
    tables = _dft_tables(n1, n2)
    kern = functools.partial(_hy_conv_kernel, ct=ct, n1=n1, n2=n2)
    full = lambda a: pl.BlockSpec(a.shape, lambda c, bb: (0,) * a.ndim)
    yt = pl.pallas_call(
        kern,
        grid=(d // ct, b),
        in_specs=[pl.BlockSpec((None, ct, n1 // 2, n2), lambda c, bb: (bb, c, 0, 0)),
                  pl.BlockSpec((ct, n1, n2), lambda c, bb: (c, 0, 0)),
                  pl.BlockSpec((ct, 1, 1), lambda c, bb: (c, 0, 0))] + [full(t) for t in tables],
        out_specs=pl.BlockSpec((None, ct, n1 // 2, n2), lambda c, bb: (bb, c, 0, 0)),
        out_shape=jax.ShapeDtypeStruct((b, d, n1 // 2, n2), F32),
        scratch_shapes=[pltpu.VMEM((2, ct, n1, n2), F32)],
        compiler_params=_mixer_params(),
        name="hyena_longconv",
    )(ut.reshape(b, d, n1 // 2, n2), hc.reshape(d, n1, n2), d_bias.reshape(d, 1, 1), *tables)
    return yt.reshape(b, d, l)


def _hyena_block(x, gain, w_in, b_in, conv_w, conv_b, f_w1, f_b1, f_w2, f_b2, f_w3, f_freq, d_bias, w_out, b_out):
    l = x.shape[1]
    x0, ut = _hy_in_block(x, gain, w_in, b_in, conv_w, conv_b)
    hc = _hy_filter_block(l, f_w1, f_b1, f_w2, f_b2, f_w3, f_freq)
    yt = _hy_conv_block(ut, hc, d_bias)
    return _hy_out_block(x, x0, yt, w_out, b_out)


def _hy_out_kernel(x_ref, x0_ref, yt_ref, wout_ref, bout_ref, o_ref):
    y = yt_ref[...].T
    z = jnp.dot((y * x0_ref[...]).astype(BF16), wout_ref[...], preferred_element_type=F32)
    o_ref[...] = x_ref[...] + z + bout_ref[...]


def _hy_out_block(x, x0, yt, w_out, b_out):
    b, l, d = x.shape
    tm = MIX_ROW_TILE
    row = pl.BlockSpec((None, tm, d), lambda bb, i: (bb, i, 0))
    return pl.pallas_call(
        _hy_out_kernel,
        grid=(b, l // tm),
        in_specs=[row, row, pl.BlockSpec((None, d, tm), lambda bb, i: (bb, 0, i)),
                  pl.BlockSpec((d, d), lambda bb, i: (0, 0), **_RESIDENT),
                  pl.BlockSpec((1, d), lambda bb, i: (0, 0))],
        out_specs=row,
        out_shape=jax.ShapeDtypeStruct((b, l, d), F32),
        compiler_params=_mixer_params(),
        name="hyena_out",
    )(x, x0, yt, w_out.astype(BF16), b_out.reshape(1, d))


LANES = 128


def _softplus(x):
    return jnp.maximum(x, 0.0) + jnp.log1p(jnp.exp(-jnp.abs(x)))


def _gdn_in_kernel(xp_ref, x_ref, xn_ref, gain_ref, win_ref, cw_ref, alog_ref, dtb_ref, cum_ref, eye_ref,
                   q_ref, k_ref, v_ref, z_ref, beta_ref, gc_ref, gct_ref, s_ref, *, tm, d, heads):
    hi = lax.Precision.HIGHEST
    gain = gain_ref[...]
    h = _rms_rows(x_ref[...], gain).astype(BF16)
    hh = _rms_rows(jnp.concatenate([xp_ref[...], xn_ref[...]], axis=0), gain).astype(BF16)
    dk = d // heads
    for j, (out_ref, scale) in enumerate(((q_ref, dk ** -0.5), (k_ref, 1.0), (v_ref, None))):
        cols = slice(j * d, (j + 1) * d)
        main = jnp.dot(h, win_ref[:, cols], preferred_element_type=F32)
        halo = jnp.dot(hh, win_ref[:, cols], preferred_element_type=F32)
        _store_with_halo(s_ref, main, halo, tm)
        c = _conv3_from(s_ref, cw_ref[:, cols], tm)
        c = c * jax.nn.sigmoid(c)
        if scale is None:
            out_ref[...] = c
        else:
            for hd in range(heads):
                ch = c[:, hd * dk:(hd + 1) * dk]
                ss = jnp.sum(ch * ch, axis=-1, keepdims=True)
                out_ref[:, hd * dk:(hd + 1) * dk] = ch * (lax.rsqrt(ss + 1e-6) * scale)
    z_ref[...] = jnp.dot(h, win_ref[:, 3 * d:4 * d], preferred_element_type=F32)
    ab = jnp.dot(h, win_ref[:, 4 * d:], preferred_element_type=F32)
    nh2 = 2 * heads
    g = -jnp.exp(alog_ref[...]) * _softplus(ab[:, :nh2] + dtb_ref[...])
    beta_ref[...] = jax.nn.sigmoid(ab[:, nh2:2 * nh2])
    gc_f = jnp.dot(cum_ref[0], g, precision=hi, preferred_element_type=F32)
    gc_b = jnp.dot(cum_ref[1], g, precision=hi, preferred_element_type=F32)
    col = lax.broadcasted_iota(jnp.int32, (tm, nh2), 1)
    gc = jnp.where(col < heads, gc_f, gc_b)
    gc_ref[...] = gc
    gct_ref[...] = lax.dot_general(eye_ref[...], gc, (((1,), (1,)), ((), ())), precision=hi,
                                   preferred_element_type=F32)


def _chunk_cumsum_masks(tm, chunk):
    import numpy as np
    r = np.arange(tm)
    same = (r[:, None] // chunk) == (r[None, :] // chunk)
    lower = same & (r[None, :] <= r[:, None])
    upper = same & (r[None, :] >= r[:, None])
    return jnp.asarray(np.stack([lower, upper]).astype(np.float32))


def _gdn_in_block(x, gain, w_in, conv_w, a_log, dt_bias):
    b, l, d = x.shape
    heads = GD_HEADS
    nh2 = 2 * heads
    tm = MIX_ROW_TILE
    wpad = jnp.pad(w_in, ((0, 0), (0, 4 * d + LANES - w_in.shape[1]))).astype(BF16)
    kern = functools.partial(_gdn_in_kernel, tm=tm, d=d, heads=heads)
    row = pl.BlockSpec((None, tm, d), lambda bb, i: (bb, i, 0))
    small = pl.BlockSpec((None, tm, nh2), lambda bb, i: (bb, i, 0))
    wide = jax.ShapeDtypeStruct((b, l, d), F32)
    narrow = jax.ShapeDtypeStruct((b, l, nh2), F32)
    return pl.pallas_call(
        kern,
        grid=(b, l // tm),
        in_specs=_halo_specs(tm, d, l) + [
            pl.BlockSpec((1, d), lambda bb, i: (0, 0)),
            pl.BlockSpec((d, 4 * d + LANES), lambda bb, i: (0, 0), **_RESIDENT),
            pl.BlockSpec((3, 3 * d), lambda bb, i: (0, 0)),
            pl.BlockSpec((1, nh2), lambda bb, i: (0, 0)),
            pl.BlockSpec((1, nh2), lambda bb, i: (0, 0)),
            pl.BlockSpec((2, tm, tm), lambda bb, i: (0, 0, 0), **_RESIDENT),
            pl.BlockSpec((nh2, nh2), lambda bb, i: (0, 0)),
        ],
        out_specs=[row, row, row, row, small, small,
                   pl.BlockSpec((None, nh2, tm), lambda bb, i: (bb, 0, i))],
        out_shape=[wide, wide, wide, wide, narrow, narrow, jax.ShapeDtypeStruct((b, nh2, l), F32)],
        scratch_shapes=[pltpu.VMEM((tm + 2 * HALO, d), F32)],
        compiler_params=_mixer_params(),
        name="gdn_in",
    )(x, x, x, gain.reshape(1, d), wpad, conv_w, a_log.reshape(1, nh2), dt_bias.reshape(1, nh2),
      _chunk_cumsum_masks(tm, GD_CHUNK), jnp.eye(nh2, dtype=F32))


GD_GROUP = 8


def _decay(gc_col, gc_row, after_or_same):
    return jnp.exp(jnp.where(after_or_same, gc_col - gc_row, -jnp.inf))


def _order_masks(c, direction):
    row = lax.broadcasted_iota(jnp.int32, (c, c), 0)
    col = lax.broadcasted_iota(jnp.int32, (c, c), 1)
    delta = (row - col) * (1 - 2 * direction)
    return delta >= 0, delta > 0


def _gdn_prep_kernel(k_ref, beta_ref, gc_ref, gct_ref, a_ref, *, heads, dk, c, group):
    direction = pl.program_id(0)
    incl, strict = _order_masks(c, direction)
    for ci in range(group):
        rows = slice(ci * c, (ci + 1) * c)
        for hd in range(heads):
            kc = k_ref[rows, hd * dk:(hd + 1) * dk]
            kb = kc * beta_ref[rows, hd:hd + 1]
            gram = lax.dot_general(kb.astype(BF16), kc.astype(BF16), (((1,), (1,)), ((), ())),
                                   preferred_element_type=F32)
            dec = _decay(gc_ref[rows, hd:hd + 1], gct_ref[ci, hd:hd + 1, :], incl)
            a_ref[ci * heads + hd] = jnp.where(strict, gram * dec, 0.0)


def _gdn_prep_block(k, beta_d, gc_d, gct_d):
    b, l, d = k.shape
    heads, c, group = GD_HEADS, GD_CHUNK, GD_GROUP
    rows = c * group
    ng = l // rows
    kern = functools.partial(_gdn_prep_kernel, heads=heads, dk=d // heads, c=c, group=group)
    per = group * heads
    return pl.pallas_call(
        kern,
        grid=(2, b, ng),
        in_specs=[pl.BlockSpec((None, rows, d), lambda dr, bb, g: (bb, g, 0)),
                  pl.BlockSpec((None, None, rows, heads), lambda dr, bb, g: (dr, bb, g, 0)),
                  pl.BlockSpec((None, None, rows, heads), lambda dr, bb, g: (dr, bb, g, 0)),
                  pl.BlockSpec((None, None, group, heads, c), lambda dr, bb, g: (dr, bb, g, 0, 0))],
        out_specs=pl.BlockSpec((per, c, c), lambda dr, bb, g: ((dr * b + bb) * ng + g, 0, 0)),
        out_shape=jax.ShapeDtypeStruct((2 * b * ng * per, c, c), F32),
        compiler_params=pltpu.CompilerParams(dimension_semantics=("arbitrary",) * 3,
                                             vmem_limit_bytes=V7X_VMEM_LIMIT_BYTES),
        name="gdn_prep",
    )(k, beta_d, gc_d, gct_d)


SUBLANES = 8


def _gdn_tsolve_kernel(a_ref, t_ref, a3_ref, t3_ref, *, c, upper):
    p = a_ref.shape[0]
    a3_ref[...] = a_ref[...].T.reshape(c, c, p)
    t3_ref[...] = jnp.zeros_like(t3_ref)
    ntile = c // SUBLANES
    sub = lax.broadcasted_iota(jnp.int32, (SUBLANES, p), 0)
    order = range(c - 1, -1, -1) if upper else range(c)
    for i in order:
        solved = range(i + 1, c) if upper else range(i)
        lo, hi = (i // SUBLANES, ntile) if upper else (0, i // SUBLANES + 1)
        acc = [jnp.zeros((SUBLANES, p), F32) for _ in range(lo, hi)]
        for j in solved:
            jlo, jhi = (j // SUBLANES, ntile) if upper else (0, j // SUBLANES + 1)
            coef = a3_ref[i, j:j + 1, :]
            tj = t3_ref[j, jlo * SUBLANES:jhi * SUBLANES, :]
            for t in range(jlo, jhi):
                acc[t - lo] = acc[t - lo] - coef * tj[(t - jlo) * SUBLANES:(t - jlo + 1) * SUBLANES]
        ti = i // SUBLANES
        acc[ti - lo] = acc[ti - lo] + jnp.where(sub == i % SUBLANES, 1.0, 0.0)
        t3_ref[i, lo * SUBLANES:hi * SUBLANES, :] = jnp.concatenate(acc, axis=0)
    t_ref[...] = t3_ref[...].reshape(c * c, p).T


def _gdn_tsolve_block(a):
    pn, c, _ = a.shape
    blk = LANES
    half = pn // 2 // blk
    a2 = a.reshape(pn, c * c)
    outs = []
    for direction in range(2):
        kern = functools.partial(_gdn_tsolve_kernel, c=c, upper=bool(direction))
        outs.append(pl.pallas_call(
            kern,
            grid=(half,),
            in_specs=[pl.BlockSpec((blk, c * c), lambda i, o=direction * half: (i + o, 0))],
            out_specs=pl.BlockSpec((blk, c * c), lambda i: (i, 0)),
            out_shape=jax.ShapeDtypeStruct((pn // 2, c * c), F32),
            scratch_shapes=[pltpu.VMEM((c, c, blk), F32), pltpu.VMEM((c, c, blk), F32)],
            compiler_params=pltpu.CompilerParams(dimension_semantics=("arbitrary",),
                                                 vmem_limit_bytes=V7X_VMEM_LIMIT_BYTES),
            name="gdn_tsolve_bwd" if direction else "gdn_tsolve_fwd",
        )(a2))
    return jnp.concatenate(outs, axis=0).reshape(pn, c, c)


def _gdn_scan_kernel(q_ref, k_ref, v_ref, beta_ref, gc_ref, gct_ref, t_ref, o_ref, s_ref,
                     *, heads, dk, c, group):
    direction = pl.program_id(0)
    incl, _ = _order_masks(c, direction)

    @pl.when(pl.program_id(2) == 0)
    def _():
        s_ref[...] = jnp.zeros_like(s_ref)

    last_row = jnp.where(direction == 0, c - 1, 0)
    for step in range(group):
        ci = jnp.where(direction == 0, step, group - 1 - step)
        rows = pl.ds(pl.multiple_of(ci * c, c), c)
        for hd in range(heads):
            lanes = slice(hd * dk, (hd + 1) * dk)
            qc = q_ref[rows, lanes]
            kc = k_ref[rows, lanes]
            vc = v_ref[rows, lanes]
            bcol = beta_ref[rows, hd:hd + 1]
            gcol = gc_ref[rows, hd:hd + 1]
            grow = gct_ref[ci, hd:hd + 1, :]
            dec = _decay(gcol, grow, incl)
            egc = jnp.exp(gcol)
            attn = lax.dot_general(qc.astype(BF16), kc.astype(BF16), (((1,), (1,)), ((), ())),
                                   preferred_element_type=F32) * dec
            kb = kc * bcol
            rhs = jnp.concatenate([vc * bcol, kb * egc], axis=1).astype(BF16)
            sol = jnp.dot(t_ref[ci * heads + hd].astype(BF16), rhs, preferred_element_type=F32)
            u, w = sol[:, :dk], sol[:, dk:]
            s_f32 = s_ref[hd]
            s_bf = s_f32.astype(BF16)
            v_new = u - jnp.dot(w.astype(BF16), s_bf, preferred_element_type=F32)
            v_new_bf = v_new.astype(BF16)
            o = (jnp.dot((qc * egc).astype(BF16), s_bf, preferred_element_type=F32)
                 + jnp.dot(attn.astype(BF16), v_new_bf, preferred_element_type=F32))
            o_ref[rows, lanes] = o
            g_last = gc_ref[pl.ds(ci * c + last_row, 1), hd:hd + 1]
            k_dec = (kc * jnp.exp(g_last - gcol)).astype(BF16)
            s_ref[hd] = s_f32 * jnp.exp(g_last) + lax.dot_general(
                k_dec, v_new_bf, (((0,), (0,)), ((), ())), preferred_element_type=F32)


def _gdn_scan_block(q, k, v, beta_d, gc_d, gct_d, t):
    b, l, d = q.shape
    heads, c, group = GD_HEADS, GD_CHUNK, GD_GROUP
    dk = d // heads
    rows = c * group
    ng = l // rows
    per = group * heads
    kern = functools.partial(_gdn_scan_kernel, heads=heads, dk=dk, c=c, group=group)
    visit = lambda dr, g: jnp.where(dr == 0, g, ng - 1 - g)
    wide = pl.BlockSpec((None, rows, d), lambda dr, bb, g: (bb, visit(dr, g), 0))
    small = pl.BlockSpec((None, None, rows, heads), lambda dr, bb, g: (dr, bb, visit(dr, g), 0))
    return pl.pallas_call(
        kern,
        grid=(2, b, ng),
        in_specs=[wide, wide, wide, small, small,
                  pl.BlockSpec((None, None, group, heads, c), lambda dr, bb, g: (dr, bb, visit(dr, g), 0, 0)),
                  pl.BlockSpec((per, c, c), lambda dr, bb, g: ((dr * b + bb) * ng + visit(dr, g), 0, 0))],
        out_specs=pl.BlockSpec((None, None, rows, d), lambda dr, bb, g: (dr, bb, visit(dr, g), 0)),
        out_shape=jax.ShapeDtypeStruct((2, b, l, d), F32),
        scratch_shapes=[pltpu.VMEM((heads, dk, dk), F32)],
        compiler_params=pltpu.CompilerParams(dimension_semantics=("arbitrary",) * 3,
                                             vmem_limit_bytes=V7X_VMEM_LIMIT_BYTES),
        name="gdn_scan",
    )(q, k, v, beta_d, gc_d, gct_d, t)


def _gdn_out_kernel(x_ref, o_ref, z_ref, ng_ref, wout_ref, y_ref, g_ref, *, heads, dk):
    o = o_ref[0] + o_ref[1]
    z = z_ref[...]
    for hd in range(heads):
        lanes = slice(hd * dk, (hd + 1) * dk)
        oh = o[:, lanes]
        ms = jnp.mean(oh * oh, axis=-1, keepdims=True)
        zh = z[:, lanes]
        g_ref[:, lanes] = (oh * lax.rsqrt(ms + NORM_EPS) * ng_ref[...] * (zh * jax.nn.sigmoid(zh))).astype(BF16)
    y_ref[...] = x_ref[...] + jnp.dot(g_ref[...], wout_ref[...], preferred_element_type=F32)


def _gdn_out_block(x, o2, z, norm_g, w_out):
    b, l, d = x.shape
    heads = GD_HEADS
    dk = d // heads
    tm = MIX_ROW_TILE
    row = pl.BlockSpec((None, tm, d), lambda bb, i: (bb, i, 0))
    kern = functools.partial(_gdn_out_kernel, heads=heads, dk=dk)
    return pl.pallas_call(
        kern,
        grid=(b, l // tm),
        in_specs=[row, pl.BlockSpec((2, None, tm, d), lambda bb, i: (0, bb, i, 0)), row,
                  pl.BlockSpec((1, dk), lambda bb, i: (0, 0)),
                  pl.BlockSpec((d, d), lambda bb, i: (0, 0), **_RESIDENT)],
        out_specs=row,
        out_shape=jax.ShapeDtypeStruct((b, l, d), F32),
        scratch_shapes=[pltpu.VMEM((tm, d), BF16)],
        compiler_params=_mixer_params(),
        name="gdn_out",
    )(x, o2, z, norm_g.reshape(1, dk), w_out.astype(BF16))


def _gdn_block(x, gain, w_in, conv_w, a_log, dt_bias, norm_g, w_out):
    b, l, d = x.shape
    heads, c = GD_HEADS, GD_CHUNK
    q, k, v, z, beta, gc, gct = _gdn_in_block(x, gain, w_in, conv_w, a_log, dt_bias)
    per_dir = lambda t: jnp.transpose(t.reshape(b, l, 2, heads), (2, 0, 1, 3))
    beta_d, gc_d = per_dir(beta), per_dir(gc)
    gct_d = jnp.transpose(gct.reshape(b, 2, heads, l // c, c), (1, 0, 3, 2, 4))
    a = _gdn_prep_block(k, beta_d, gc_d, gct_d)
    t = _gdn_tsolve_block(a)
    o2 = _gdn_scan_block(q, k, v, beta_d, gc_d, gct_d, t)
    return _gdn_out_block(x, o2, z, norm_g, w_out)


def kernel(x, norms, final_norm, ffn_w_in, ffn_w_out, sc_w_in, sc_conv, sc_w_out, hy_w_in, hy_b_in, hy_conv, hy_conv_b, hy_f_w1, hy_f_b1, hy_f_w2, hy_f_b2, hy_f_w3, hy_f_freq, hy_d, hy_w_out, hy_b_out, gd_w_in, gd_conv, gd_a_log, gd_dt_bias, gd_norm, gd_w_out):
    b, l, d = x.shape
    depth = norms.shape[0]

    def ffn(xx, i, k):
        final = final_norm if (i == depth - 1 and k == 1) else None
        return _ffn_block(xx.reshape(b * l, d), norms[i, 2 * k], ffn_w_in[i, k], ffn_w_out[i, k],
                          final).reshape(b, l, d)

    for i in range(depth):
        m, j = i % N_MIXERS, i // N_MIXERS
        x = ffn(x, i, 0)
        if m == 0:
            x = _sc_mixer_block(x, norms[i, 1], sc_w_in[j], sc_conv[j], sc_w_out[j])
        elif m == 1:
            x = _hyena_block(x, norms[i, 1], hy_w_in[j], hy_b_in[j], hy_conv[j], hy_conv_b[j],
                             hy_f_w1[j], hy_f_b1[j], hy_f_w2[j], hy_f_b2[j], hy_f_w3[j],
                             hy_f_freq[j], hy_d[j], hy_w_out[j], hy_b_out[j])
        else:
            x = _gdn_block(x, norms[i, 1], gd_w_in[j], gd_conv[j], gd_a_log[j], gd_dt_bias[j],
                           gd_norm[j], gd_w_out[j])
        x = ffn(x, i, 1)
    return x
```

```python
import functools
import math

import jax
import jax.numpy as jnp
from jax import lax
from jax.experimental import pallas as pl
from jax.experimental.pallas import tpu as pltpu

F32 = jnp.float32
BF16 = jnp.bfloat16

NORM_EPS = 1e-6
N_MIXERS = 3
HY_EMB = 33
HY_BANDS = (HY_EMB - 1) // 2
HY_MAX_DECAY = math.log(1e-2) / 0.3
HY_MIN_DECAY = math.log(1e-2) / 1.5
GD_HEADS = 8
GD_CHUNK = 64

V7X_VMEM_LIMIT_BYTES = 56 * 1024 * 1024
FFN_ROW_TILE = 512
FFN_HIDDEN_TILE = 256


def _ffn_kernel(x_ref, gain_ref, win_ref, wout_ref, *rest, n_hidden_tiles, hidden_tile, final_norm):
    final_ref, o_ref, hid_ref = rest if final_norm else (None,) + rest
    x = x_ref[...]
    xn = _rms_rows(x, gain_ref[...]).astype(BF16)
    f = n_hidden_tiles * hidden_tile
    for j in range(n_hidden_tiles):
        cols = slice(j * hidden_tile, (j + 1) * hidden_tile)
        up_cols = slice(f + j * hidden_tile, f + (j + 1) * hidden_tile)
        g = jnp.dot(xn, win_ref[:, cols], preferred_element_type=F32)
        u = jnp.dot(xn, win_ref[:, up_cols], preferred_element_type=F32)
        hid_ref[:, cols] = (g * jax.nn.sigmoid(g) * u).astype(BF16)
    y = x + 0.5 * jnp.dot(hid_ref[...], wout_ref[...], preferred_element_type=F32)
    o_ref[...] = _rms_rows(y, final_ref[...]) if final_norm else y


def _ffn_block(x2d, gain, w_in, w_out, final_gain=None):
    t, d = x2d.shape
    win_b, wout_b = w_in.astype(BF16), w_out.astype(BF16)
    f = wout_b.shape[0]
    tm = FFN_ROW_TILE
    final_norm = final_gain is not None
    kern = functools.partial(_ffn_kernel, n_hidden_tiles=f // FFN_HIDDEN_TILE, hidden_tile=FFN_HIDDEN_TILE,
                             final_norm=final_norm)
    vec = pl.BlockSpec((1, d), lambda i: (0, 0))
    extra = ([vec], [final_gain.reshape(1, d)]) if final_norm else ([], [])
    return pl.pallas_call(
        kern,
        grid=(t // tm,),
        in_specs=[
            pl.BlockSpec((tm, d), lambda i: (i, 0)),
            vec,
            pl.BlockSpec((d, 2 * f), lambda i: (0, 0), **_RESIDENT),
            pl.BlockSpec((f, d), lambda i: (0, 0), **_RESIDENT),
        ] + extra[0],
        out_specs=pl.BlockSpec((tm, d), lambda i: (i, 0)),
        out_shape=jax.ShapeDtypeStruct((t, d), F32),
        scratch_shapes=[pltpu.VMEM((tm, f), BF16)],
        compiler_params=pltpu.CompilerParams(
            dimension_semantics=("arbitrary",), vmem_limit_bytes=V7X_VMEM_LIMIT_BYTES),
        name="ffn_final" if final_norm else "ffn_block",
    )(x2d, gain.reshape(1, d), win_b, wout_b, *extra[1])


MIX_ROW_TILE = 512
HALO = 8


def _rms_rows(x, gain):
    ms = jnp.mean(x * x, axis=-1, keepdims=True)
    return x * lax.rsqrt(ms + NORM_EPS) * gain


def _halo_specs(tm, d, seq_len):
    per = tm // HALO
    last_blk = seq_len // HALO - 1
    return [
        pl.BlockSpec((None, HALO, d), lambda b, i: (b, jnp.maximum(i * per - 1, 0), 0)),
        pl.BlockSpec((None, tm, d), lambda b, i: (b, i, 0)),
        pl.BlockSpec((None, HALO, d), lambda b, i: (b, jnp.minimum((i + 1) * per, last_blk), 0)),
    ]


def _store_with_halo(s_ref, main, halo, tm):
    i = pl.program_id(1)
    last = pl.num_programs(1) - 1
    s_ref[0:HALO, :] = jnp.where(i > 0, halo[:HALO], 0.0)
    s_ref[HALO:HALO + tm, :] = main
    s_ref[HALO + tm:2 * HALO + tm, :] = jnp.where(i < last, halo[HALO:], 0.0)


def _conv3_from(s_ref, cw, tm):
    return (cw[0:1] * s_ref[HALO - 1:HALO - 1 + tm, :] + cw[1:2] * s_ref[HALO:HALO + tm, :]
            + cw[2:3] * s_ref[HALO + 1:HALO + 1 + tm, :])


_RESIDENT = dict(pipeline_mode=pl.Buffered(1))


def _mixer_params():
    return pltpu.CompilerParams(dimension_semantics=("arbitrary", "arbitrary"),
                                vmem_limit_bytes=V7X_VMEM_LIMIT_BYTES)


def _sc_mixer_kernel(xp_ref, x_ref, xn_ref, gain_ref, win_ref, cw_ref, wout_ref, o_ref, ch_ref, *, tm, d):
    gain = gain_ref[...]
    x = x_ref[...]
    h = _rms_rows(x, gain).astype(BF16)
    c_main = jnp.dot(h, win_ref[:, d:2 * d], preferred_element_type=F32)
    h_main = jnp.dot(h, win_ref[:, 2 * d:], preferred_element_type=F32)
    hh = _rms_rows(jnp.concatenate([xp_ref[...], xn_ref[...]], axis=0), gain).astype(BF16)
    c_halo = jnp.dot(hh, win_ref[:, d:2 * d], preferred_element_type=F32)
    h_halo = jnp.dot(hh, win_ref[:, 2 * d:], preferred_element_type=F32)
    _store_with_halo(ch_ref, c_main * h_main, c_halo * h_halo, tm)
    conv = _conv3_from(ch_ref, cw_ref[...], tm)
    b_main = jnp.dot(h, win_ref[:, :d], preferred_element_type=F32)
    y = jnp.dot((b_main * conv).astype(BF16), wout_ref[...], preferred_element_type=F32)
    o_ref[...] = x + y


def _sc_mixer_block(x, gain, w_in, conv_w, w_out):
    b, l, d = x.shape
    tm = MIX_ROW_TILE
    kern = functools.partial(_sc_mixer_kernel, tm=tm, d=d)
    return pl.pallas_call(
        kern,
        grid=(b, l // tm),
        in_specs=_halo_specs(tm, d, l) + [
            pl.BlockSpec((1, d), lambda bb, i: (0, 0)),
            pl.BlockSpec((d, 3 * d), lambda bb, i: (0, 0), **_RESIDENT),
            pl.BlockSpec((3, d), lambda bb, i: (0, 0)),
            pl.BlockSpec((d, d), lambda bb, i: (0, 0), **_RESIDENT),
        ],
        out_specs=pl.BlockSpec((None, tm, d), lambda bb, i: (bb, i, 0)),
        out_shape=jax.ShapeDtypeStruct((b, l, d), F32),
        scratch_shapes=[pltpu.VMEM((tm + 2 * HALO, d), F32)],
        compiler_params=_mixer_params(),
        name="sc_mixer",
    )(x, x, x, gain.reshape(1, d), w_in.astype(BF16), conv_w, w_out.astype(BF16))


def _hy_in_kernel(xp_ref, x_ref, xn_ref, gain_ref, win_ref, bin_ref, cw_ref, cb_ref, x0_ref, vxt_ref, s_ref,
                  *, tm, d):
    gain = gain_ref[...]
    h = _rms_rows(x_ref[...], gain).astype(BF16)
    hh = _rms_rows(jnp.concatenate([xp_ref[...], xn_ref[...]], axis=0), gain).astype(BF16)
    parts = []
    for j in range(3):
        cols = slice(j * d, (j + 1) * d)
        bias = bin_ref[:, cols]
        main = jnp.dot(h, win_ref[:, cols], preferred_element_type=F32) + bias
        halo = jnp.dot(hh, win_ref[:, cols], preferred_element_type=F32) + bias
        _store_with_halo(s_ref, main, halo, tm)
        parts.append(_conv3_from(s_ref, cw_ref[:, cols], tm) + cb_ref[:, cols])
    x0_ref[...] = parts[0]
    vxt_ref[...] = (parts[2] * parts[1]).T


def _hy_in_block(x, gain, w_in, b_in, conv_w, conv_b):
    b, l, d = x.shape
    tm = MIX_ROW_TILE
    kern = functools.partial(_hy_in_kernel, tm=tm, d=d)
    return pl.pallas_call(
        kern,
        grid=(b, l // tm),
        in_specs=_halo_specs(tm, d, l) + [
            pl.BlockSpec((1, d), lambda bb, i: (0, 0)),
            pl.BlockSpec((d, 3 * d), lambda bb, i: (0, 0), **_RESIDENT),
            pl.BlockSpec((1, 3 * d), lambda bb, i: (0, 0)),
            pl.BlockSpec((3, 3 * d), lambda bb, i: (0, 0)),
            pl.BlockSpec((1, 3 * d), lambda bb, i: (0, 0)),
        ],
        out_specs=[pl.BlockSpec((None, tm, d), lambda bb, i: (bb, i, 0)),
                   pl.BlockSpec((None, d, tm), lambda bb, i: (bb, 0, i))],
        out_shape=[jax.ShapeDtypeStruct((b, l, d), F32), jax.ShapeDtypeStruct((b, d, l), F32)],
        scratch_shapes=[pltpu.VMEM((tm + 2 * HALO, d), F32)],
        compiler_params=_mixer_params(),
        name="hyena_in",
    )(x, x, x, gain.reshape(1, d), w_in.astype(BF16), b_in.reshape(1, 3 * d), conv_w, conv_b.reshape(1, 3 * d))


HY_FEAT_PAD = 64
HY_FILTER_TILE = 2048


def _hy_filter_kernel(zt_ref, w1t_ref, b1_ref, w2t_ref, b2_ref, freq_ref, w3t_ref, delta_ref, hc_ref):
    hi = lax.Precision.HIGHEST
    z = zt_ref[...]
    freq = freq_ref[...]
    a1 = jnp.dot(w1t_ref[...], z, precision=hi, preferred_element_type=F32) + b1_ref[...]
    h1 = jnp.sin(freq * a1)
    a2 = jnp.dot(w2t_ref[...], h1, precision=hi, preferred_element_type=F32) + b2_ref[...]
    h2 = jnp.sin(freq * a2)
    h = jnp.dot(w3t_ref[...], h2, precision=hi, preferred_element_type=F32)
    t_row = z[0:1, :]
    mask_row = z[HY_EMB:HY_EMB + 1, :]
    hc_ref[...] = h * jnp.exp(-t_row * delta_ref[...]) * mask_row


def _hy_filter_features(l):
    p = jnp.arange(2 * l)
    pos = jnp.where(p < l, p, 2 * l - p)
    valid = (p != l).astype(F32)
    pos = jnp.minimum(pos, l - 1).astype(F32)[None, :]
    t = pos / (l - 1)
    w = 2.0 * math.pi * pos / l
    f = jnp.linspace(1e-4, HY_BANDS - 1, HY_BANDS, dtype=F32)[:, None]
    z = jnp.concatenate([t, jnp.cos(f * w), -jnp.sin(f * w), valid[None, :]], axis=0)
    return jnp.pad(z, ((0, HY_FEAT_PAD - z.shape[0]), (0, 0)))


def _hy_filter_block(l, w1, b1, w2, b2, w3, freq):
    order = w1.shape[1]
    d = w3.shape[1] // 2
    tl = min(HY_FILTER_TILE, l)
    half = l // tl
    zt = _hy_filter_features(l)
    w1t = jnp.pad(w1.T, ((0, 0), (0, HY_FEAT_PAD - w1.shape[0])))
    w3t = jnp.stack([w3[:, :d].T, w3[:, d:].T])
    delta = jnp.abs(jnp.linspace(HY_MIN_DECAY, HY_MAX_DECAY, d, dtype=F32)).reshape(d, 1)
    col = lambda v: v.reshape(order, 1)
    full = lambda shape: pl.BlockSpec(shape, lambda i: (0,) * len(shape))
    return pl.pallas_call(
        _hy_filter_kernel,
        grid=(2 * half,),
        in_specs=[pl.BlockSpec((HY_FEAT_PAD, tl), lambda i: (0, i)),
                  full((order, HY_FEAT_PAD)), full((order, 1)), full((order, order)), full((order, 1)),
                  full((order, 1)),
                  pl.BlockSpec((None, d, order), lambda i: (i // half, 0, 0)),
                  full((d, 1))],
        out_specs=pl.BlockSpec((d, tl), lambda i: (0, i)),
        out_shape=jax.ShapeDtypeStruct((d, 2 * l), F32),
        compiler_params=pltpu.CompilerParams(dimension_semantics=("arbitrary",),
                                             vmem_limit_bytes=V7X_VMEM_LIMIT_BYTES),
        name="hyena_filter",
    )(zt, w1t, col(b1), w2.T, col(b2), col(freq), w3t, delta)


HY_CONV_CH_TILE = 16


def _dft_tables(n1, n2):
    import numpy as np
    n = n1 * n2
    i1, i2 = np.arange(n1), np.arange(n2)
    a1 = 2.0 * np.pi * np.outer(i1, i1) / n1
    a2 = 2.0 * np.pi * np.outer(i2, i2) / n2
    c1, s1, c2, s2 = np.cos(a1), np.sin(a1), np.cos(a2), np.sin(a2)
    tw = 2.0 * np.pi * np.outer(i2, i1) / n
    f_real = np.concatenate([c1, -s1], axis=1)
    f_cplx = np.block([[c2, -s2], [s2, c2]])
    i_cplx = np.block([[c2, s2], [-s2, c2]])
    i_real = np.concatenate([c1, -s1], axis=0) / n
    as_bf = lambda a: jnp.asarray(a, dtype=F32).astype(BF16)
    as_f = lambda a: jnp.asarray(a, dtype=F32)
    return (as_bf(f_real), as_bf(f_cplx), as_bf(i_cplx), as_bf(i_real),
            as_f(np.cos(tw)), as_f(np.sin(tw)), as_f(np.cos(tw.T)), as_f(np.sin(tw.T)))


def _swap(a):
    return jnp.swapaxes(a, 1, 2)


def _hy_conv_kernel(ut_ref, hc_ref, dbias_ref, freal_ref, fcplx_ref, icplx_ref, ireal_ref, twc_ref, tws_ref,
                    twct_ref, twst_ref, yt_ref, hspec_ref, *, ct, n1, n2):
    rows = ct * n2
    twc = twc_ref[...]
    tws = tws_ref[...]

    def mm(a, m_ref):
        return jnp.dot(a.astype(BF16), m_ref[...], preferred_element_type=F32)

    def forward(x):
        p = mm(_swap(x).reshape(rows, n1), freal_ref)
        pr = p[:, :n1].reshape(ct, n2, n1)
        pi = p[:, n1:].reshape(ct, n2, n1)
        qr = pr * twc + pi * tws
        qi = pi * twc - pr * tws
        q = jnp.concatenate([_swap(qr).reshape(ct * n1, n2), _swap(qi).reshape(ct * n1, n2)], axis=1)
        xs = mm(q, fcplx_ref)
        return xs[:, :n2].reshape(ct, n1, n2), xs[:, n2:].reshape(ct, n1, n2)

    @pl.when(pl.program_id(1) == 0)
    def _():
        hr, hi = forward(hc_ref[...])
        hspec_ref[0] = hr
        hspec_ref[1] = hi

    u = ut_ref[...]
    xr, xi = forward(jnp.concatenate([u, jnp.zeros_like(u)], axis=1))
    hr = hspec_ref[0]
    hi = hspec_ref[1]
    yr = xr * hr - xi * hi
    yi = xr * hi + xi * hr
    r = mm(jnp.concatenate([yr.reshape(ct * n1, n2), yi.reshape(ct * n1, n2)], axis=1), icplx_ref)
    rr = r[:, :n2].reshape(ct, n1, n2)
    ri = r[:, n2:].reshape(ct, n1, n2)
    twct = twct_ref[...]
    twst = twst_ref[...]
    sr = rr * twct - ri * twst
    si = ri * twct + rr * twst
    s = jnp.concatenate([_swap(sr).reshape(rows, n1), _swap(si).reshape(rows, n1)], axis=1)
    z = mm(s, ireal_ref).reshape(ct, n2, n1)
    y = _swap(z)[:, :n1 // 2, :]
    yt_ref[...] = y + u * dbias_ref[...]


def _hy_conv_block(ut, hc, d_bias):
    b, d, l = ut.shape
    n2 = 128
    n1 = 2 * l // n2
    ct = HY_CONV_CH_TILE
    tables = _dft_tables(n1, n2)
    kern = functools.partial(_hy_conv_kernel, ct=ct, n1=n1, n2=n2)
    full = lambda a: pl.BlockSpec(a.shape, lambda c, bb: (0,) * a.ndim)
    yt = pl.pallas_call(
        kern,
        grid=(d // ct, b),
        in_specs=[pl.BlockSpec((None, ct, n1 // 2, n2), lambda c, bb: (bb, c, 0, 0)),
                  pl.BlockSpec((ct, n1, n2), lambda c, bb: (c, 0, 0)),
                  pl.BlockSpec((ct, 1, 1), lambda c, bb: (c, 0, 0))] + [full(t) for t in tables],
        out_specs=pl.BlockSpec((None, ct, n1 // 2, n2), lambda c, bb: (bb, c, 0, 0)),
        out_shape=jax.ShapeDtypeStruct((b, d, n1 // 2, n2), F32),
        scratch_shapes=[pltpu.VMEM((2, ct, n1, n2), F32)],
        compiler_params=_mixer_params(),
        name="hyena_longconv",
    )(ut.reshape(b, d, n1 // 2, n2), hc.reshape(d, n1, n2), d_bias.reshape(d, 1, 1), *tables)
    return yt.reshape(b, d, l)


def _hyena_block(x, gain, w_in, b_in, conv_w, conv_b, f_w1, f_b1, f_w2, f_b2, f_w3, f_freq, d_bias, w_out, b_out):
    l = x.shape[1]
    x0, ut = _hy_in_block(x, gain, w_in, b_in, conv_w, conv_b)
    hc = _hy_filter_block(l, f_w1, f_b1, f_w2, f_b2, f_w3, f_freq)
    yt = _hy_conv_block(ut, hc, d_bias)
    return _hy_out_block(x, x0, yt, w_out, b_out)


def _hy_out_kernel(x_ref, x0_ref, yt_ref, wout_ref, bout_ref, o_ref):
    y = yt_ref[...].T
    z = jnp.dot((y * x0_ref[...]).astype(BF16), wout_ref[...], preferred_element_type=F32)
    o_ref[...] = x_ref[...] + z + bout_ref[...]


def _hy_out_block(x, x0, yt, w_out, b_out):
    b, l, d = x.shape
    tm = MIX_ROW_TILE
    row = pl.BlockSpec((None, tm, d), lambda bb, i: (bb, i, 0))
    return pl.pallas_call(
        _hy_out_kernel,
        grid=(b, l // tm),
        in_specs=[row, row, pl.BlockSpec((None, d, tm), lambda bb, i: (bb, 0, i)),
                  pl.BlockSpec((d, d), lambda bb, i: (0, 0), **_RESIDENT),
                  pl.BlockSpec((1, d), lambda bb, i: (0, 0))],
        out_specs=row,
        out_shape=jax.ShapeDtypeStruct((b, l, d), F32),
        compiler_params=_mixer_params(),
        name="hyena_out",
    )(x, x0, yt, w_out.astype(BF16), b_out.reshape(1, d))


LANES = 128


def _softplus(x):
    return jnp.maximum(x, 0.0) + jnp.log1p(jnp.exp(-jnp.abs(x)))


def _gdn_in_kernel(xp_ref, x_ref, xn_ref, gain_ref, win_ref, cw_ref, alog_ref, dtb_ref, cum_ref, eye_ref,
                   q_ref, k_ref, v_ref, z_ref, kt_ref, cols_ref, rows_ref, s_ref, *, tm, d, heads):
    hi = lax.Precision.HIGHEST
    gain = gain_ref[...]
    h = _rms_rows(x_ref[...], gain).astype(BF16)
    hh = _rms_rows(jnp.concatenate([xp_ref[...], xn_ref[...]], axis=0), gain).astype(BF16)
    dk = d // heads
    for j, (out_ref, scale) in enumerate(((q_ref, dk ** -0.5), (k_ref, 1.0), (v_ref, None))):
        cols = slice(j * d, (j + 1) * d)
        main = jnp.dot(h, win_ref[:, cols], preferred_element_type=F32)
        halo = jnp.dot(hh, win_ref[:, cols], preferred_element_type=F32)
        _store_with_halo(s_ref, main, halo, tm)
        c = _conv3_from(s_ref, cw_ref[:, cols], tm)
        c = c * jax.nn.sigmoid(c)
        if scale is None:
            out_ref[...] = c
        else:
            for hd in range(heads):
                ch = c[:, hd * dk:(hd + 1) * dk]
                ss = jnp.sum(ch * ch, axis=-1, keepdims=True)
                out_ref[:, hd * dk:(hd + 1) * dk] = ch * (lax.rsqrt(ss + 1e-6) * scale)
            if j == 1:
                kt = out_ref[...].T
                n_chunk, _, c = kt_ref.shape
                for ci in range(n_chunk):
                    kt_ref[ci] = kt[:, ci * c:(ci + 1) * c]
    z_ref[...] = jnp.dot(h, win_ref[:, 3 * d:4 * d], preferred_element_type=F32)
    ab = jnp.dot(h, win_ref[:, 4 * d:], preferred_element_type=F32)
    nh2 = 2 * heads
    g = -jnp.exp(alog_ref[...]) * _softplus(ab[:, :nh2] + dtb_ref[...])
    beta = jax.nn.sigmoid(ab[:, nh2:2 * nh2])
    gc_f = jnp.dot(cum_ref[0], g, precision=hi, preferred_element_type=F32)
    gc_b = jnp.dot(cum_ref[1], g, precision=hi, preferred_element_type=F32)
    chunk = tm // rows_ref.shape[1]
    for dr, gc in enumerate((gc_f, gc_b)):
        own = slice(dr * heads, (dr + 1) * heads)
        cols = jnp.concatenate([gc[:, own], beta[:, own]], axis=1)
        cols_ref[dr] = cols
        rows = lax.dot_general(eye_ref[...], cols, (((1,), (1,)), ((), ())), precision=hi,
                               preferred_element_type=F32)
        for ci in range(tm // chunk):
            rows_ref[dr, ci] = rows[:, ci * chunk:(ci + 1) * chunk]


def _chunk_cumsum_masks(tm, chunk):
    import numpy as np
    r = np.arange(tm)
    same = (r[:, None] // chunk) == (r[None, :] // chunk)
    lower = same & (r[None, :] <= r[:, None])
    upper = same & (r[None, :] >= r[:, None])
    return jnp.asarray(np.stack([lower, upper]).astype(np.float32))


def _gdn_in_block(x, gain, w_in, conv_w, a_log, dt_bias):
    b, l, d = x.shape
    heads = GD_HEADS
    nh2 = 2 * heads
    tm = MIX_ROW_TILE
    wpad = jnp.pad(w_in, ((0, 0), (0, 4 * d + LANES - w_in.shape[1]))).astype(BF16)
    kern = functools.partial(_gdn_in_kernel, tm=tm, d=d, heads=heads)
    row = pl.BlockSpec((None, tm, d), lambda bb, i: (bb, i, 0))
    wide = jax.ShapeDtypeStruct((b, l, d), F32)
    c = GD_CHUNK
    return pl.pallas_call(
        kern,
        grid=(b, l // tm),
        in_specs=_halo_specs(tm, d, l) + [
            pl.BlockSpec((1, d), lambda bb, i: (0, 0)),
            pl.BlockSpec((d, 4 * d + LANES), lambda bb, i: (0, 0), **_RESIDENT),
            pl.BlockSpec((3, 3 * d), lambda bb, i: (0, 0)),
            pl.BlockSpec((1, nh2), lambda bb, i: (0, 0)),
            pl.BlockSpec((1, nh2), lambda bb, i: (0, 0)),
            pl.BlockSpec((2, tm, tm), lambda bb, i: (0, 0, 0), **_RESIDENT),
            pl.BlockSpec((nh2, nh2), lambda bb, i: (0, 0)),
        ],
        out_specs=[row, row, row, row,
                   pl.BlockSpec((None, tm // c, d, c), lambda bb, i: (bb, i, 0, 0)),
                   pl.BlockSpec((2, None, tm, nh2), lambda bb, i: (0, bb, i, 0)),
                   pl.BlockSpec((2, None, tm // c, nh2, c), lambda bb, i: (0, bb, i, 0, 0))],
        out_shape=[wide, wide, wide, wide, jax.ShapeDtypeStruct((b, l // c, d, c), F32),
                   jax.ShapeDtypeStruct((2, b, l, nh2), F32),
                   jax.ShapeDtypeStruct((2, b, l // c, nh2, c), F32)],
        scratch_shapes=[pltpu.VMEM((tm + 2 * HALO, d), F32)],
        compiler_params=_mixer_params(),
        name="gdn_in",
    )(x, x, x, gain.reshape(1, d), wpad, conv_w, a_log.reshape(1, nh2), dt_bias.reshape(1, nh2),
      _chunk_cumsum_masks(tm, GD_CHUNK), jnp.eye(nh2, dtype=F32))


GD_GROUP = 8


def _decay(gc_col, gc_row, after_or_same):
    return jnp.exp(jnp.where(after_or_same, gc_col - gc_row, -jnp.inf))


def _order_masks(c, direction):
    row = lax.broadcasted_iota(jnp.int32, (c, c), 0)
    col = lax.broadcasted_iota(jnp.int32, (c, c), 1)
    delta = (row - col) * (1 - 2 * direction)
    return delta >= 0, delta > 0


def _gdn_prep_kernel(k_ref, cols_ref, rows_ref, a_ref, *, heads, dk, c, group):
    direction = pl.program_id(0)
    incl, strict = _order_masks(c, direction)
    for ci in range(group):
        rows = slice(ci * c, (ci + 1) * c)
        for hd in range(heads):
            kc = k_ref[rows, hd * dk:(hd + 1) * dk]
            kb = kc * cols_ref[rows, heads + hd:heads + hd + 1]
            gram = lax.dot_general(kb.astype(BF16), kc.astype(BF16), (((1,), (1,)), ((), ())),
                                   preferred_element_type=F32)
            dec = _decay(cols_ref[rows, hd:hd + 1], rows_ref[ci, hd:hd + 1, :], incl)
            a_ref[ci * heads + hd] = jnp.where(strict, gram * dec, 0.0)


def _gdn_prep_block(k, cols, rows_t):
    b, l, d = k.shape
    heads, c, group = GD_HEADS, GD_CHUNK, GD_GROUP
    rows = c * group
    ng = l // rows
    kern = functools.partial(_gdn_prep_kernel, heads=heads, dk=d // heads, c=c, group=group)
    per = group * heads
    return pl.pallas_call(
        kern,
        grid=(2, b, ng),
        in_specs=[pl.BlockSpec((None, rows, d), lambda dr, bb, g: (bb, g, 0)),
                  pl.BlockSpec((None, None, rows, 2 * heads), lambda dr, bb, g: (dr, bb, g, 0)),
                  pl.BlockSpec((None, None, group, 2 * heads, c), lambda dr, bb, g: (dr, bb, g, 0, 0))],
        out_specs=pl.BlockSpec((per, c, c), lambda dr, bb, g: ((dr * b + bb) * ng + g, 0, 0)),
        out_shape=jax.ShapeDtypeStruct((2 * b * ng * per, c, c), F32),
        compiler_params=pltpu.CompilerParams(dimension_semantics=("arbitrary",) * 3,
                                             vmem_limit_bytes=V7X_VMEM_LIMIT_BYTES),
        name="gdn_prep",
    )(k, cols, rows_t)


SUBLANES = 8


def _gdn_tsolve_kernel(a_ref, t_ref, a3_ref, t3_ref, *, c, upper):
    p = a_ref.shape[0]
    a3_ref[...] = a_ref[...].T.reshape(c, c, p)
    t3_ref[...] = jnp.zeros_like(t3_ref)
    ntile = c // SUBLANES
    sub = lax.broadcasted_iota(jnp.int32, (SUBLANES, p), 0)
    order = range(c - 1, -1, -1) if upper else range(c)
    for i in order:
        solved = range(i + 1, c) if upper else range(i)
        lo, hi = (i // SUBLANES, ntile) if upper else (0, i // SUBLANES + 1)
        acc = [jnp.zeros((SUBLANES, p), F32) for _ in range(lo, hi)]
        for j in solved:
            jlo, jhi = (j // SUBLANES, ntile) if upper else (0, j // SUBLANES + 1)
            coef = a3_ref[i, j:j + 1, :]
            tj = t3_ref[j, jlo * SUBLANES:jhi * SUBLANES, :]
            for t in range(jlo, jhi):
                acc[t - lo] = acc[t - lo] - coef * tj[(t - jlo) * SUBLANES:(t - jlo + 1) * SUBLANES]
        ti = i // SUBLANES
        acc[ti - lo] = acc[ti - lo] + jnp.where(sub == i % SUBLANES, 1.0, 0.0)
        t3_ref[i, lo * SUBLANES:hi * SUBLANES, :] = jnp.concatenate(acc, axis=0)
    t_ref[...] = t3_ref[...].reshape(c * c, p).T


def _gdn_tsolve_block(a):
    pn, c, _ = a.shape
    blk = LANES
    half = pn // 2 // blk
    a2 = a.reshape(pn, c * c)
    outs = []
    for direction in range(2):
        kern = functools.partial(_gdn_tsolve_kernel, c=c, upper=bool(direction))
        outs.append(pl.pallas_call(
            kern,
            grid=(half,),
            in_specs=[pl.BlockSpec((blk, c * c), lambda i, o=direction * half: (i + o, 0))],
            out_specs=pl.BlockSpec((blk, c * c), lambda i: (i, 0)),
            out_shape=jax.ShapeDtypeStruct((pn // 2, c * c), F32),
            scratch_shapes=[pltpu.VMEM((c, c, blk), F32), pltpu.VMEM((c, c, blk), F32)],
            compiler_params=pltpu.CompilerParams(dimension_semantics=("arbitrary",),
                                                 vmem_limit_bytes=V7X_VMEM_LIMIT_BYTES),
            name="gdn_tsolve_bwd" if direction else "gdn_tsolve_fwd",
        )(a2))
    return jnp.concatenate(outs, axis=0).reshape(pn, c, c)


def _gdn_scan_kernel(q_ref, k_ref, v_ref, kt_ref, cols_ref, rows_ref, t_ref, sel_ref, o_ref,
                     s_ref, g_ref, qe_ref, rhs_ref, tb_ref, attn_ref, u_ref, w_ref,
                     *, heads, dk, c, group, reverse):
    @pl.when(pl.program_id(1) == 0)
    def _():
        s_ref[...] = jnp.zeros_like(s_ref)

    cols = cols_ref[...]
    hi = cols.astype(BF16)
    rest = cols - hi.astype(F32)
    mid = rest.astype(BF16)
    lo = (rest - mid.astype(F32)).astype(BF16)
    sel = sel_ref[...]
    spread = lambda a: jnp.dot(a, sel, preferred_element_type=F32)
    g_all = (spread(hi) + spread(mid)) + spread(lo)
    g_ref[...] = g_all
    eg = jnp.exp(g_all)
    qe_ref[...] = (q_ref[...] * eg).astype(BF16)
    ke = k_ref[...] * eg
    for hd in range(heads):
        lanes = slice(hd * dk, (hd + 1) * dk)
        rhs_ref[:, 2 * hd * dk:(2 * hd + 1) * dk] = v_ref[:, lanes].astype(BF16)
        rhs_ref[:, (2 * hd + 1) * dk:(2 * hd + 2) * dk] = ke[:, lanes].astype(BF16)
    beta_rows = rows_ref[:, heads:, :].reshape(group * heads, 1, c)
    tb_ref[...] = (t_ref[...] * beta_rows).astype(BF16)

    row = lax.broadcasted_iota(jnp.int32, (c, c), 0)
    col = lax.broadcasted_iota(jnp.int32, (c, c), 1)
    incl = (row <= col) if reverse else (row >= col)
    head_lanes = [slice(hd * dk, (hd + 1) * dk) for hd in range(heads)]

    for ci in range(group):
        rows = slice(ci * c, (ci + 1) * c)
        for hd, lanes in enumerate(head_lanes):
            dec = _decay(g_ref[rows, hd * dk:hd * dk + c], rows_ref[ci, hd:hd + 1, :], incl)
            gram = lax.dot_general(q_ref[rows, lanes].astype(BF16), k_ref[rows, lanes].astype(BF16),
                                   (((1,), (1,)), ((), ())), preferred_element_type=F32)
            attn_ref[ci * heads + hd] = (gram * dec).astype(BF16)
            sol = jnp.dot(tb_ref[ci * heads + hd], rhs_ref[rows, 2 * hd * dk:(2 * hd + 2) * dk],
                          preferred_element_type=F32)
            u_ref[rows, lanes] = sol[:, :dk]
            w_ref[rows, lanes] = sol[:, dk:].astype(BF16)

    for step in range(group):
        ci = group - 1 - step if reverse else step
        rows = slice(ci * c, (ci + 1) * c)
        last = ci * c if reverse else (ci + 1) * c - 1
        states = [s_ref[hd] for hd in range(heads)]
        states_bf = [s.astype(BF16) for s in states]
        w_s = [jnp.dot(w_ref[rows, lanes], s, preferred_element_type=F32)
               for lanes, s in zip(head_lanes, states_bf)]
        q_s = [jnp.dot(qe_ref[rows, lanes], s, preferred_element_type=F32)
               for lanes, s in zip(head_lanes, states_bf)]
        for hd, lanes in enumerate(head_lanes):
            v_new_bf = (u_ref[rows, lanes] - w_s[hd]).astype(BF16)
            o_ref[rows, lanes] = q_s[hd] + jnp.dot(attn_ref[ci * heads + hd], v_new_bf,
                                                   preferred_element_type=F32)
            g_tot = g_ref[last:last + 1, lanes]
            k_dec_t = (kt_ref[ci, lanes, :] * jnp.exp(g_tot[:, :c] - rows_ref[ci, hd:hd + 1, :])).astype(BF16)
            s_ref[hd] = states[hd] * jnp.exp(g_tot) + jnp.dot(k_dec_t, v_new_bf, preferred_element_type=F32)


def _gdn_scan_block(q, k, v, kt, cols, rows_t, t, reverse):
    b, l, d = q.shape
    heads, c, group = GD_HEADS, GD_CHUNK, GD_GROUP
    dk = d // heads
    rows = c * group
    ng = l // rows
    per = group * heads
    dr = int(reverse)
    kern = functools.partial(_gdn_scan_kernel, heads=heads, dk=dk, c=c, group=group, reverse=reverse)
    visit = (lambda g: ng - 1 - g) if reverse else (lambda g: g)
    wide = pl.BlockSpec((None, rows, d), lambda bb, g: (bb, visit(g), 0))
    sel = jnp.repeat(jnp.eye(2 * heads, heads, dtype=BF16), dk, axis=1)
    return pl.pallas_call(
        kern,
        grid=(b, ng),
        in_specs=[wide, wide, wide,
                  pl.BlockSpec((None, group, d, c), lambda bb, g: (bb, visit(g), 0, 0)),
                  pl.BlockSpec((None, None, rows, 2 * heads), lambda bb, g: (dr, bb, visit(g), 0)),
                  pl.BlockSpec((None, None, group, 2 * heads, c), lambda bb, g: (dr, bb, visit(g), 0, 0)),
                  pl.BlockSpec((per, c, c), lambda bb, g: ((dr * b + bb) * ng + visit(g), 0, 0)),
                  pl.BlockSpec((2 * heads, d), lambda bb, g: (0, 0))],
        out_specs=wide,
        out_shape=jax.ShapeDtypeStruct((b, l, d), F32),
        scratch_shapes=[pltpu.VMEM((heads, dk, dk), F32), pltpu.VMEM((rows, d), F32),
                        pltpu.VMEM((rows, d), BF16), pltpu.VMEM((rows, 2 * d), BF16),
                        pltpu.VMEM((per, c, c), BF16), pltpu.VMEM((per, c, c), BF16),
                        pltpu.VMEM((rows, d), F32), pltpu.VMEM((rows, d), BF16)],
        compiler_params=_mixer_params(),
        name="gdn_scan_bwd" if reverse else "gdn_scan_fwd",
    )(q, k, v, kt, cols, rows_t, t, sel)


def _gdn_out_kernel(x_ref, of_ref, ob_ref, z_ref, ng_ref, wout_ref, y_ref, g_ref, *, heads, dk):
    o = of_ref[...] + ob_ref[...]
    z = z_ref[...]
    for hd in range(heads):
        lanes = slice(hd * dk, (hd + 1) * dk)
        oh = o[:, lanes]
        ms = jnp.mean(oh * oh, axis=-1, keepdims=True)
        zh = z[:, lanes]
        g_ref[:, lanes] = (oh * lax.rsqrt(ms + NORM_EPS) * ng_ref[...] * (zh * jax.nn.sigmoid(zh))).astype(BF16)
    y_ref[...] = x_ref[...] + jnp.dot(g_ref[...], wout_ref[...], preferred_element_type=F32)


def _gdn_out_block(x, o_fwd, o_bwd, z, norm_g, w_out):
    b, l, d = x.shape
    heads = GD_HEADS
    dk = d // heads
    tm = MIX_ROW_TILE
    row = pl.BlockSpec((None, tm, d), lambda bb, i: (bb, i, 0))
    kern = functools.partial(_gdn_out_kernel, heads=heads, dk=dk)
    return pl.pallas_call(
        kern,
        grid=(b, l // tm),
        in_specs=[row, row, row, row,
                  pl.BlockSpec((1, dk), lambda bb, i: (0, 0)),
                  pl.BlockSpec((d, d), lambda bb, i: (0, 0), **_RESIDENT)],
        out_specs=row,
        out_shape=jax.ShapeDtypeStruct((b, l, d), F32),
        scratch_shapes=[pltpu.VMEM((tm, d), BF16)],
        compiler_params=_mixer_params(),
        name="gdn_out",
    )(x, o_fwd, o_bwd, z, norm_g.reshape(1, dk), w_out.astype(BF16))


def _gdn_block(x, gain, w_in, conv_w, a_log, dt_bias, norm_g, w_out):
    q, k, v, z, kt, cols, rows_t = _gdn_in_block(x, gain, w_in, conv_w, a_log, dt_bias)
    t = _gdn_tsolve_block(_gdn_prep_block(k, cols, rows_t))
    o_fwd = _gdn_scan_block(q, k, v, kt, cols, rows_t, t, reverse=False)
    o_bwd = _gdn_scan_block(q, k, v, kt, cols, rows_t, t, reverse=True)
    return _gdn_out_block(x, o_fwd, o_bwd, z, norm_g, w_out)


def kernel(x, norms, final_norm, ffn_w_in, ffn_w_out, sc_w_in, sc_conv, sc_w_out, hy_w_in, hy_b_in, hy_conv, hy_conv_b, hy_f_w1, hy_f_b1, hy_f_w2, hy_f_b2, hy_f_w3, hy_f_freq, hy_d, hy_w_out, hy_b_out, gd_w_in, gd_conv, gd_a_log, gd_dt_bias, gd_norm, gd_w_out):
    b, l, d = x.shape
    depth = norms.shape[0]

    def ffn(xx, i, k):
        final = final_norm if (i == depth - 1 and k == 1) else None
        return _ffn_block(xx.reshape(b * l, d), norms[i, 2 * k], ffn_w_in[i, k], ffn_w_out[i, k],
                          final).reshape(b, l, d)

    for i in range(depth):
        m, j = i % N_MIXERS, i // N_MIXERS
        x = ffn(x, i, 0)
        if m == 0:
            x = _sc_mixer_block(x, norms[i, 1], sc_w_in[j], sc_conv[j], sc_w_out[j])
        elif m == 1:
            x = _hyena_block(x, norms[i, 1], hy_w_in[j], hy_b_in[j], hy_conv[j], hy_conv_b[j],
                             hy_f_w1[j], hy_f_b1[j], hy_f_w2[j], hy_f_b2[j], hy_f_w3[j],
                             hy_f_freq[j], hy_d[j], hy_w_out[j], hy_b_out[j])
        else:
            x = _gdn_block(x, norms[i, 1], gd_w_in[j], gd_conv[j], gd_a_log[j], gd_dt_bias[j],
                           gd_norm[j], gd_w_out[j])
        x = ffn(x, i, 1)
    return x
```

```python
import functools
import math

import jax
import jax.numpy as jnp
from jax import lax
from jax.experimental import pallas as pl
from jax.experimental.pallas import tpu as pltpu

F32 = jnp.float32
BF16 = jnp.bfloat16

NORM_EPS = 1e-6
N_MIXERS = 3
HY_EMB = 33
HY_BANDS = (HY_EMB - 1) // 2
HY_MAX_DECAY = math.log(1e-2) / 0.3
HY_MIN_DECAY = math.log(1e-2) / 1.5
GD_HEADS = 8
GD_CHUNK = 64

V7X_VMEM_LIMIT_BYTES = 56 * 1024 * 1024
FFN_ROW_TILE = 512
FFN_HIDDEN_TILE = 256


def _ffn_kernel(x_ref, gain_ref, win_ref, wout_ref, *rest, n_hidden_tiles, hidden_tile, final_norm):
    final_ref, o_ref, hid_ref = rest if final_norm else (None,) + rest
    x = x_ref[...]
    xn = _rms_rows(x, gain_ref[...]).astype(BF16)
    f = n_hidden_tiles * hidden_tile
    for j in range(n_hidden_tiles):
        cols = slice(j * hidden_tile, (j + 1) * hidden_tile)
        up_cols = slice(f + j * hidden_tile, f + (j + 1) * hidden_tile)
        g = jnp.dot(xn, win_ref[:, cols].astype(BF16), preferred_element_type=F32)
        u = jnp.dot(xn, win_ref[:, up_cols].astype(BF16), preferred_element_type=F32)
        hid_ref[:, cols] = (g * jax.nn.sigmoid(g) * u).astype(BF16)
    y = x + 0.5 * jnp.dot(hid_ref[...], wout_ref[...].astype(BF16), preferred_element_type=F32)
    o_ref[...] = _rms_rows(y, final_ref[...]) if final_norm else y


def _stacked_weight_spec(w, lead):
    tail = w.shape[len(lead):]
    index = tuple(lead) + (0,) * len(tail)
    return pl.BlockSpec((None,) * len(lead) + tail, lambda *_: index, **_RESIDENT)


def _ffn_block(x2d, gain, w_in, w_out, lead, final_gain=None):
    t, d = x2d.shape
    f = w_out.shape[-2]
    tm = FFN_ROW_TILE
    final_norm = final_gain is not None
    kern = functools.partial(_ffn_kernel, n_hidden_tiles=f // FFN_HIDDEN_TILE, hidden_tile=FFN_HIDDEN_TILE,
                             final_norm=final_norm)
    vec = pl.BlockSpec((1, d), lambda i: (0, 0))
    extra = ([vec], [final_gain.reshape(1, d)]) if final_norm else ([], [])
    return pl.pallas_call(
        kern,
        grid=(t // tm,),
        in_specs=[
            pl.BlockSpec((tm, d), lambda i: (i, 0)),
            vec,
            _stacked_weight_spec(w_in, lead),
            _stacked_weight_spec(w_out, lead),
        ] + extra[0],
        out_specs=pl.BlockSpec((tm, d), lambda i: (i, 0)),
        out_shape=jax.ShapeDtypeStruct((t, d), F32),
        scratch_shapes=[pltpu.VMEM((tm, f), BF16)],
        compiler_params=pltpu.CompilerParams(
            dimension_semantics=("arbitrary",), vmem_limit_bytes=V7X_VMEM_LIMIT_BYTES),
        name="ffn_final" if final_norm else "ffn_block",
    )(x2d, gain.reshape(1, d), w_in, w_out, *extra[1])


MIX_ROW_TILE = 512
HALO = 8


def _rms_rows(x, gain):
    ms = jnp.mean(x * x, axis=-1, keepdims=True)
    return x * lax.rsqrt(ms + NORM_EPS) * gain


def _halo_specs(tm, d, seq_len):
    per = tm // HALO
    last_blk = seq_len // HALO - 1
    return [
        pl.BlockSpec((None, HALO, d), lambda b, i: (b, jnp.maximum(i * per - 1, 0), 0)),
        pl.BlockSpec((None, tm, d), lambda b, i: (b, i, 0)),
        pl.BlockSpec((None, HALO, d), lambda b, i: (b, jnp.minimum((i + 1) * per, last_blk), 0)),
    ]


def _store_with_halo(s_ref, main, halo, tm):
    i = pl.program_id(1)
    last = pl.num_programs(1) - 1
    s_ref[0:HALO, :] = jnp.where(i > 0, halo[:HALO], 0.0)
    s_ref[HALO:HALO + tm, :] = main
    s_ref[HALO + tm:2 * HALO + tm, :] = jnp.where(i < last, halo[HALO:], 0.0)


def _conv3_from(s_ref, cw, tm):
    return (cw[0:1] * s_ref[HALO - 1:HALO - 1 + tm, :] + cw[1:2] * s_ref[HALO:HALO + tm, :]
            + cw[2:3] * s_ref[HALO + 1:HALO + 1 + tm, :])


_RESIDENT = dict(pipeline_mode=pl.Buffered(1))


def _mixer_params():
    return pltpu.CompilerParams(dimension_semantics=("arbitrary", "arbitrary"),
                                vmem_limit_bytes=V7X_VMEM_LIMIT_BYTES)


def _sc_mixer_kernel(xp_ref, x_ref, xn_ref, gain_ref, win_ref, cw_ref, wout_ref, o_ref, ch_ref, *, tm, d):
    gain = gain_ref[...]
    x = x_ref[...]
    h = _rms_rows(x, gain).astype(BF16)
    c_main = jnp.dot(h, win_ref[:, d:2 * d], preferred_element_type=F32)
    h_main = jnp.dot(h, win_ref[:, 2 * d:], preferred_element_type=F32)
    hh = _rms_rows(jnp.concatenate([xp_ref[...], xn_ref[...]], axis=0), gain).astype(BF16)
    c_halo = jnp.dot(hh, win_ref[:, d:2 * d], preferred_element_type=F32)
    h_halo = jnp.dot(hh, win_ref[:, 2 * d:], preferred_element_type=F32)
    _store_with_halo(ch_ref, c_main * h_main, c_halo * h_halo, tm)
    conv = _conv3_from(ch_ref, cw_ref[...], tm)
    b_main = jnp.dot(h, win_ref[:, :d], preferred_element_type=F32)
    y = jnp.dot((b_main * conv).astype(BF16), wout_ref[...], preferred_element_type=F32)
    o_ref[...] = x + y


def _sc_mixer_block(x, gain, w_in, conv_w, w_out):
    b, l, d = x.shape
    tm = MIX_ROW_TILE
    kern = functools.partial(_sc_mixer_kernel, tm=tm, d=d)
    return pl.pallas_call(
        kern,
        grid=(b, l // tm),
        in_specs=_halo_specs(tm, d, l) + [
            pl.BlockSpec((1, d), lambda bb, i: (0, 0)),
            pl.BlockSpec((d, 3 * d), lambda bb, i: (0, 0), **_RESIDENT),
            pl.BlockSpec((3, d), lambda bb, i: (0, 0)),
            pl.BlockSpec((d, d), lambda bb, i: (0, 0), **_RESIDENT),
        ],
        out_specs=pl.BlockSpec((None, tm, d), lambda bb, i: (bb, i, 0)),
        out_shape=jax.ShapeDtypeStruct((b, l, d), F32),
        scratch_shapes=[pltpu.VMEM((tm + 2 * HALO, d), F32)],
        compiler_params=_mixer_params(),
        name="sc_mixer",
    )(x, x, x, gain.reshape(1, d), w_in.astype(BF16), conv_w, w_out.astype(BF16))


def _hy_in_kernel(xp_ref, x_ref, xn_ref, gain_ref, win_ref, bin_ref, cw_ref, cb_ref, x0_ref, vxt_ref, s_ref,
                  *, tm, d):
    gain = gain_ref[...]
    h = _rms_rows(x_ref[...], gain).astype(BF16)
    hh = _rms_rows(jnp.concatenate([xp_ref[...], xn_ref[...]], axis=0), gain).astype(BF16)
    parts = []
    for j in range(3):
        cols = slice(j * d, (j + 1) * d)
        bias = bin_ref[:, cols]
        main = jnp.dot(h, win_ref[:, cols], preferred_element_type=F32) + bias
        halo = jnp.dot(hh, win_ref[:, cols], preferred_element_type=F32) + bias
        _store_with_halo(s_ref, main, halo, tm)
        parts.append(_conv3_from(s_ref, cw_ref[:, cols], tm) + cb_ref[:, cols])
    x0_ref[...] = parts[0]
    vxt_ref[...] = (parts[2] * parts[1]).T


def _hy_in_block(x, gain, w_in, b_in, conv_w, conv_b):
    b, l, d = x.shape
    tm = MIX_ROW_TILE
    kern = functools.partial(_hy_in_kernel, tm=tm, d=d)
    return pl.pallas_call(
        kern,
        grid=(b, l // tm),
        in_specs=_halo_specs(tm, d, l) + [
            pl.BlockSpec((1, d), lambda bb, i: (0, 0)),
            pl.BlockSpec((d, 3 * d), lambda bb, i: (0, 0), **_RESIDENT),
            pl.BlockSpec((1, 3 * d), lambda bb, i: (0, 0)),
            pl.BlockSpec((3, 3 * d), lambda bb, i: (0, 0)),
            pl.BlockSpec((1, 3 * d), lambda bb, i: (0, 0)),
        ],
        out_specs=[pl.BlockSpec((None, tm, d), lambda bb, i: (bb, i, 0)),
                   pl.BlockSpec((None, d, tm), lambda bb, i: (bb, 0, i))],
        out_shape=[jax.ShapeDtypeStruct((b, l, d), F32), jax.ShapeDtypeStruct((b, d, l), F32)],
        scratch_shapes=[pltpu.VMEM((tm + 2 * HALO, d), F32)],
        compiler_params=_mixer_params(),
        name="hyena_in",
    )(x, x, x, gain.reshape(1, d), w_in.astype(BF16), b_in.reshape(1, 3 * d), conv_w, conv_b.reshape(1, 3 * d))


HY_FEAT_PAD = 64
HY_FILTER_TILE = 2048


def _hy_filter_kernel(zt_ref, w1_ref, b1_ref, w2_ref, b2_ref, freq_ref, w3_ref, delta_ref, hc_ref):
    def t_dot(w, a):
        return lax.dot_general(w, a, (((0,), (0,)), ((), ())), precision=lax.Precision.HIGHEST,
                               preferred_element_type=F32)

    z = zt_ref[...]
    freq = freq_ref[...]
    h1 = jnp.sin(freq * (t_dot(w1_ref[...], z) + b1_ref[...]))
    h2 = jnp.sin(freq * (t_dot(w2_ref[...], h1) + b2_ref[...]))
    h = t_dot(w3_ref[...], h2)
    t_row = z[0:1, :]
    mask_row = z[HY_EMB:HY_EMB + 1, :]
    hc_ref[...] = h * jnp.exp(-t_row * delta_ref[...]) * mask_row


def _hy_filter_features(l):
    p = jnp.arange(2 * l)
    pos = jnp.where(p < l, p, 2 * l - p)
    valid = (p != l).astype(F32)
    pos = jnp.minimum(pos, l - 1).astype(F32)[None, :]
    t = pos / (l - 1)
    w = 2.0 * math.pi * pos / l
    f = jnp.linspace(1e-4, HY_BANDS - 1, HY_BANDS, dtype=F32)[:, None]
    z = jnp.concatenate([t, jnp.cos(f * w), -jnp.sin(f * w), valid[None, :]], axis=0)
    return jnp.pad(z, ((0, HY_FEAT_PAD - z.shape[0]), (0, 0)))


def _hy_filter_block(l, w1, b1, w2, b2, w3, freq):
    order = w1.shape[1]
    d = w3.shape[1] // 2
    tl = min(HY_FILTER_TILE, l)
    half = l // tl
    zt = _hy_filter_features(l)
    w1p = jnp.pad(w1, ((0, HY_FEAT_PAD - w1.shape[0]), (0, 0)))
    delta = jnp.abs(jnp.linspace(HY_MIN_DECAY, HY_MAX_DECAY, d, dtype=F32)).reshape(d, 1)
    col = lambda v: v.reshape(order, 1)
    full = lambda shape: pl.BlockSpec(shape, lambda i: (0,) * len(shape))
    return pl.pallas_call(
        _hy_filter_kernel,
        grid=(2 * half,),
        in_specs=[pl.BlockSpec((HY_FEAT_PAD, tl), lambda i: (0, i)),
                  full((HY_FEAT_PAD, order)), full((order, 1)), full((order, order)), full((order, 1)),
                  full((order, 1)),
                  pl.BlockSpec((order, d), lambda i: (0, i // half)),
                  full((d, 1))],
        out_specs=pl.BlockSpec((d, tl), lambda i: (0, i)),
        out_shape=jax.ShapeDtypeStruct((d, 2 * l), F32),
        compiler_params=pltpu.CompilerParams(dimension_semantics=("arbitrary",),
                                             vmem_limit_bytes=V7X_VMEM_LIMIT_BYTES),
        name="hyena_filter",
    )(zt, w1p, col(b1), w2, col(b2), col(freq), w3, delta)


HY_CONV_CH_TILE = 16


def _dft_tables(n1, n2):
    import numpy as np
    n = n1 * n2
    i1, i2 = np.arange(n1), np.arange(n2)
    a1 = 2.0 * np.pi * np.outer(i1, i1) / n1
    a2 = 2.0 * np.pi * np.outer(i2, i2) / n2
    c1, s1, c2, s2 = np.cos(a1), np.sin(a1), np.cos(a2), np.sin(a2)
    tw = 2.0 * np.pi * np.outer(i2, i1) / n
    f_real = np.concatenate([c1, -s1], axis=1)
    f_cplx = np.block([[c2, -s2], [s2, c2]])
    i_cplx = np.block([[c2, s2], [-s2, c2]])
    i_real = np.concatenate([c1, -s1], axis=0) / n
    as_bf = lambda a: jnp.asarray(a, dtype=F32).astype(BF16)
    as_f = lambda a: jnp.asarray(a, dtype=F32)
    return (as_bf(f_real), as_bf(f_cplx), as_bf(i_cplx), as_bf(i_real),
            as_f(np.cos(tw)), as_f(np.sin(tw)), as_f(np.cos(tw.T)), as_f(np.sin(tw.T)))


def _swap(a):
    return jnp.swapaxes(a, 1, 2)


def _hy_conv_kernel(ut_ref, hc_ref, dbias_ref, freal_ref, fcplx_ref, icplx_ref, ireal_ref, twc_ref, tws_ref,
                    twct_ref, twst_ref, yt_ref, hspec_ref, *, ct, n1, n2):
    rows = ct * n2
    twc = twc_ref[...]
    tws = tws_ref[...]

    def mm(a, m_ref):
        return jnp.dot(a.astype(BF16), m_ref[...], preferred_element_type=F32)

    def forward(x):
        p = mm(_swap(x).reshape(rows, n1), freal_ref)
        pr = p[:, :n1].reshape(ct, n2, n1)
        pi = p[:, n1:].reshape(ct, n2, n1)
        qr = pr * twc + pi * tws
        qi = pi * twc - pr * tws
        q = jnp.concatenate([_swap(qr).reshape(ct * n1, n2), _swap(qi).reshape(ct * n1, n2)], axis=1)
        xs = mm(q, fcplx_ref)
        return xs[:, :n2].reshape(ct, n1, n2), xs[:, n2:].reshape(ct, n1, n2)

    @pl.when(pl.program_id(1) == 0)
    def _():
        hr, hi = forward(hc_ref[...])
        hspec_ref[0] = hr
        hspec_ref[1] = hi

    u = ut_ref[...]
    xr, xi = forward(jnp.concatenate([u, jnp.zeros_like(u)], axis=1))
    hr = hspec_ref[0]
    hi = hspec_ref[1]
    yr = xr * hr - xi * hi
    yi = xr * hi + xi * hr
    r = mm(jnp.concatenate([yr.reshape(ct * n1, n2), yi.reshape(ct * n1, n2)], axis=1), icplx_ref)
    rr = r[:, :n2].reshape(ct, n1, n2)
    ri = r[:, n2:].reshape(ct, n1, n2)
    twct = twct_ref[...]
    twst = twst_ref[...]
    sr = rr * twct - ri * twst
    si = ri * twct + rr * twst
    s = jnp.concatenate([_swap(sr).reshape(rows, n1), _swap(si).reshape(rows, n1)], axis=1)
    z = mm(s, ireal_ref).reshape(ct, n2, n1)
    y = _swap(z)[:, :n1 // 2, :]
    yt_ref[...] = y + u * dbias_ref[...]


def _hy_conv_block(ut, hc, d_bias):
    b, d, l = ut.shape
    n2 = 128
    n1 = 2 * l // n2
    ct = HY_CONV_CH_TILE
    tables = _dft_tables(n1, n2)
    kern = functools.partial(_hy_conv_kernel, ct=ct, n1=n1, n2=n2)
    full = lambda a: pl.BlockSpec(a.shape, lambda c, bb: (0,) * a.ndim)
    yt = pl.pallas_call(
        kern,
        grid=(d // ct, b),
        in_specs=[pl.BlockSpec((None, ct, n1 // 2, n2), lambda c, bb: (bb, c, 0, 0)),
                  pl.BlockSpec((ct, n1, n2), lambda c, bb: (c, 0, 0)),
                  pl.BlockSpec((ct, 1, 1), lambda c, bb: (c, 0, 0))] + [full(t) for t in tables],
        out_specs=pl.BlockSpec((None, ct, n1 // 2, n2), lambda c, bb: (bb, c, 0, 0)),
        out_shape=jax.ShapeDtypeStruct((b, d, n1 // 2, n2), F32),
        scratch_shapes=[pltpu.VMEM((2, ct, n1, n2), F32)],
        compiler_params=_mixer_params(),
        name="hyena_longconv",
    )(ut.reshape(b, d, n1 // 2, n2), hc.reshape(d, n1, n2), d_bias.reshape(d, 1, 1), *tables)
    return yt.reshape(b, d, l)


def _hyena_block(x, gain, w_in, b_in, conv_w, conv_b, f_w1, f_b1, f_w2, f_b2, f_w3, f_freq, d_bias, w_out, b_out):
    l = x.shape[1]
    x0, ut = _hy_in_block(x, gain, w_in, b_in, conv_w, conv_b)
    hc = _hy_filter_block(l, f_w1, f_b1, f_w2, f_b2, f_w3, f_freq)
    yt = _hy_conv_block(ut, hc, d_bias)
    return _hy_out_block(x, x0, yt, w_out, b_out)


def _hy_out_kernel(x_ref, x0_ref, yt_ref, wout_ref, bout_ref, o_ref):
    y = yt_ref[...].T
    z = jnp.dot((y * x0_ref[...]).astype(BF16), wout_ref[...], preferred_element_type=F32)
    o_ref[...] = x_ref[...] + z + bout_ref[...]


def _hy_out_block(x, x0, yt, w_out, b_out):
    b, l, d = x.shape
    tm = MIX_ROW_TILE
    row = pl.BlockSpec((None, tm, d), lambda bb, i: (bb, i, 0))
    return pl.pallas_call(
        _hy_out_kernel,
        grid=(b, l // tm),
        in_specs=[row, row, pl.BlockSpec((None, d, tm), lambda bb, i: (bb, 0, i)),
                  pl.BlockSpec((d, d), lambda bb, i: (0, 0), **_RESIDENT),
                  pl.BlockSpec((1, d), lambda bb, i: (0, 0))],
        out_specs=row,
        out_shape=jax.ShapeDtypeStruct((b, l, d), F32),
        compiler_params=_mixer_params(),
        name="hyena_out",
    )(x, x0, yt, w_out.astype(BF16), b_out.reshape(1, d))


LANES = 128


def _softplus(x):
    return jnp.maximum(x, 0.0) + jnp.log1p(jnp.exp(-jnp.abs(x)))


def _gdn_in_kernel(xp_ref, x_ref, xn_ref, gain_ref, win_ref, cw_ref, alog_ref, dtb_ref, cum_ref, eye_ref,
                   q_ref, k_ref, v_ref, z_ref, kt_ref, cols_ref, rows_ref, pairs_ref, s_ref, *, tm, d, heads):
    hi = lax.Precision.HIGHEST
    gain = gain_ref[...]
    h = _rms_rows(x_ref[...], gain).astype(BF16)
    hh = _rms_rows(jnp.concatenate([xp_ref[...], xn_ref[...]], axis=0), gain).astype(BF16)
    dk = d // heads
    for j, (out_ref, scale) in enumerate(((q_ref, dk ** -0.5), (k_ref, 1.0), (v_ref, None))):
        cols = slice(j * d, (j + 1) * d)
        w_cols = win_ref[:, cols].astype(BF16)
        main = jnp.dot(h, w_cols, preferred_element_type=F32)
        halo = jnp.dot(hh, w_cols, preferred_element_type=F32)
        _store_with_halo(s_ref, main, halo, tm)
        c = _conv3_from(s_ref, cw_ref[:, cols], tm)
        c = c * jax.nn.sigmoid(c)
        if scale is None:
            out_ref[...] = c
        else:
            for hd in range(heads):
                ch = c[:, hd * dk:(hd + 1) * dk]
                ss = jnp.sum(ch * ch, axis=-1, keepdims=True)
                out_ref[:, hd * dk:(hd + 1) * dk] = ch * (lax.rsqrt(ss + 1e-6) * scale)
            if j == 1:
                kt = out_ref[...].T
                n_chunk, _, c = kt_ref.shape
                for ci in range(n_chunk):
                    kt_ref[ci] = kt[:, ci * c:(ci + 1) * c]
    z_ref[...] = jnp.dot(h, win_ref[:, 3 * d:4 * d].astype(BF16), preferred_element_type=F32)
    ab = jnp.dot(h, win_ref[:, 4 * d:].astype(BF16), preferred_element_type=F32)
    nh2 = 2 * heads
    g = -jnp.exp(alog_ref[...]) * _softplus(ab[:, :nh2] + dtb_ref[...])
    beta = jax.nn.sigmoid(ab[:, nh2:2 * nh2])
    gc_f = jnp.dot(cum_ref[0], g, precision=hi, preferred_element_type=F32)
    gc_b = jnp.dot(cum_ref[1], g, precision=hi, preferred_element_type=F32)
    chunk = tm // rows_ref.shape[1]
    for dr, gc in enumerate((gc_f, gc_b)):
        own = slice(dr * heads, (dr + 1) * heads)
        cols = jnp.concatenate([gc[:, own], beta[:, own]], axis=1)
        cols_ref[dr] = cols
        rows = lax.dot_general(eye_ref[...], cols, (((1,), (1,)), ((), ())), precision=hi,
                               preferred_element_type=F32)
        for ci in range(tm // chunk):
            chunk_rows = rows[:, ci * chunk:(ci + 1) * chunk]
            rows_ref[dr, ci] = chunk_rows
            pairs_ref[dr, ci] = jnp.concatenate(
                [jnp.concatenate([chunk_rows[r:r + 1], chunk_rows[r + 1:r + 2]], axis=1)
                 for r in range(0, nh2, 2)], axis=0)


def _chunk_cumsum_masks(tm, chunk):
    import numpy as np
    r = np.arange(tm)
    same = (r[:, None] // chunk) == (r[None, :] // chunk)
    lower = same & (r[None, :] <= r[:, None])
    upper = same & (r[None, :] >= r[:, None])
    return jnp.asarray(np.stack([lower, upper]).astype(np.float32))


def _gdn_in_block(x, gain, w_in, lead, conv_w, a_log, dt_bias):
    b, l, d = x.shape
    heads = GD_HEADS
    nh2 = 2 * heads
    tm = MIX_ROW_TILE
    kern = functools.partial(_gdn_in_kernel, tm=tm, d=d, heads=heads)
    row = pl.BlockSpec((None, tm, d), lambda bb, i: (bb, i, 0))
    wide = jax.ShapeDtypeStruct((b, l, d), F32)
    c = GD_CHUNK
    return pl.pallas_call(
        kern,
        grid=(b, l // tm),
        in_specs=_halo_specs(tm, d, l) + [
            pl.BlockSpec((1, d), lambda bb, i: (0, 0)),
            _stacked_weight_spec(w_in, lead),
            pl.BlockSpec((3, 3 * d), lambda bb, i: (0, 0)),
            pl.BlockSpec((1, nh2), lambda bb, i: (0, 0)),
            pl.BlockSpec((1, nh2), lambda bb, i: (0, 0)),
            pl.BlockSpec((2, tm, tm), lambda bb, i: (0, 0, 0), **_RESIDENT),
            pl.BlockSpec((nh2, nh2), lambda bb, i: (0, 0)),
        ],
        out_specs=[row, row, row, row,
                   pl.BlockSpec((None, tm // c, d, c), lambda bb, i: (bb, i, 0, 0)),
                   pl.BlockSpec((2, None, tm, nh2), lambda bb, i: (0, bb, i, 0)),
                   pl.BlockSpec((2, None, tm // c, nh2, c), lambda bb, i: (0, bb, i, 0, 0)),
                   pl.BlockSpec((2, None, tm // c, heads, 2 * c), lambda bb, i: (0, bb, i, 0, 0))],
        out_shape=[wide, wide, wide, wide, jax.ShapeDtypeStruct((b, l // c, d, c), F32),
                   jax.ShapeDtypeStruct((2, b, l, nh2), F32),
                   jax.ShapeDtypeStruct((2, b, l // c, nh2, c), F32),
                   jax.ShapeDtypeStruct((2, b, l // c, heads, 2 * c), F32)],
        scratch_shapes=[pltpu.VMEM((tm + 2 * HALO, d), F32)],
        compiler_params=_mixer_params(),
        name="gdn_in",
    )(x, x, x, gain.reshape(1, d), w_in, conv_w, a_log.reshape(1, nh2), dt_bias.reshape(1, nh2),
      _chunk_cumsum_masks(tm, GD_CHUNK), jnp.eye(nh2, dtype=F32))


GD_GROUP = 8


def _decay(gc_col, gc_row, after_or_same):
    return jnp.exp(jnp.where(after_or_same, gc_col - gc_row, -jnp.inf))


def _spread_exact(cols, sel):
    hi = cols.astype(BF16)
    rest = cols - hi.astype(F32)
    mid = rest.astype(BF16)
    lo = (rest - mid.astype(F32)).astype(BF16)
    part = lambda a: jnp.dot(a, sel, preferred_element_type=F32)
    return (part(hi) + part(mid)) + part(lo)


def _head_pair_block_diag(left, right):
    zero = jnp.zeros_like(left)
    return jnp.concatenate([jnp.concatenate([left, zero], axis=1),
                            jnp.concatenate([zero, right], axis=1)], axis=0)


def _gdn_prep_kernel(k_ref, cols_ref, pairs_ref, selg_ref, selb_ref, a_ref, *, heads, dk, c, group):
    direction = pl.program_id(0)
    row = lax.broadcasted_iota(jnp.int32, (c, 2 * c), 0)
    col = lax.broadcasted_iota(jnp.int32, (c, 2 * c), 1) % c
    delta = (row - col) * (1 - 2 * direction)
    cols = cols_ref[...]
    g_pairs = _spread_exact(cols, selg_ref[...])
    b_pairs = _spread_exact(cols, selb_ref[...])
    n_pair = heads // 2
    for ci in range(group):
        rows = slice(ci * c, (ci + 1) * c)
        for m in range(n_pair):
            lanes = slice(2 * c * m, 2 * c * (m + 1))
            kp = k_ref[rows, 2 * dk * m:2 * dk * (m + 1)].astype(BF16)
            gram = lax.dot_general(kp, _head_pair_block_diag(kp[:, :dk], kp[:, dk:]),
                                   (((1,), (1,)), ((), ())), preferred_element_type=F32)
            dec = jnp.exp(jnp.where(delta > 0, g_pairs[rows, lanes] - pairs_ref[ci, m:m + 1, :], -jnp.inf))
            a_ref[:, ci * n_pair + m, :] = b_pairs[rows, lanes] * gram * dec


def _pair_selectors(heads, c):
    eye = jnp.eye(2 * heads, heads, dtype=BF16)
    return jnp.repeat(eye, c, axis=1), jnp.repeat(jnp.roll(eye, heads, axis=0), c, axis=1)


def _gdn_prep_block(k, cols, pairs):
    b, l, d = k.shape
    heads, c, group = GD_HEADS, GD_CHUNK, GD_GROUP
    rows = c * group
    ng = l // rows
    kern = functools.partial(_gdn_prep_kernel, heads=heads, dk=d // heads, c=c, group=group)
    per = group * heads // 2
    full = lambda a: pl.BlockSpec(a.shape, lambda dr, bb, g: (0,) * a.ndim)
    selg, selb = _pair_selectors(heads, c)
    return pl.pallas_call(
        kern,
        grid=(2, b, ng),
        in_specs=[pl.BlockSpec((None, rows, d), lambda dr, bb, g: (bb, g, 0)),
                  pl.BlockSpec((None, None, rows, 2 * heads), lambda dr, bb, g: (dr, bb, g, 0)),
                  pl.BlockSpec((None, None, group, heads, 2 * c), lambda dr, bb, g: (dr, bb, g, 0, 0)),
                  full(selg), full(selb)],
        out_specs=pl.BlockSpec((None, c, per, 2 * c), lambda dr, bb, g: (dr, 0, bb * ng + g, 0)),
        out_shape=jax.ShapeDtypeStruct((2, c, b * ng * per, 2 * c), F32),
        compiler_params=pltpu.CompilerParams(dimension_semantics=("arbitrary",) * 3,
                                             vmem_limit_bytes=V7X_VMEM_LIMIT_BYTES),
        name="gdn_prep",
    )(k, cols, pairs, selg, selb)


SUBLANES = 8


def _gdn_tsolve_kernel(a_ref, t_ref, a3_ref, t3_ref, *, c, upper):
    p = a_ref.shape[1]
    for i in range(c):
        slab = a_ref[i].T
        a3_ref[0, i] = slab[:c]
        a3_ref[1, i] = slab[c:]
    t3_ref[...] = jnp.zeros_like(t3_ref)
    ntile = c // SUBLANES
    sub = lax.broadcasted_iota(jnp.int32, (SUBLANES, p), 0)
    order = range(c - 1, -1, -1) if upper else range(c)
    for i in order:
        solved = range(i + 1, c) if upper else range(i)
        lo, hi = (i // SUBLANES, ntile) if upper else (0, i // SUBLANES + 1)
        ti = i // SUBLANES
        for half in range(2):
            acc = [jnp.zeros((SUBLANES, p), F32) for _ in range(lo, hi)]
            for j in solved:
                jlo, jhi = (j // SUBLANES, ntile) if upper else (0, j // SUBLANES + 1)
                coef = a3_ref[half, i, j:j + 1, :]
                tj = t3_ref[half, j, jlo * SUBLANES:jhi * SUBLANES, :]
                for t in range(jlo, jhi):
                    acc[t - lo] = acc[t - lo] - coef * tj[(t - jlo) * SUBLANES:(t - jlo + 1) * SUBLANES]
            acc[ti - lo] = acc[ti - lo] + jnp.where(sub == i % SUBLANES, 1.0, 0.0)
            t3_ref[half, i, lo * SUBLANES:hi * SUBLANES, :] = jnp.concatenate(acc, axis=0)
    for i in range(c):
        t_ref[i] = jnp.concatenate([t3_ref[0, i], t3_ref[1, i]], axis=0).T


def _gdn_tsolve_block(a, direction):
    _, c, pairs, _ = a.shape
    blk = LANES
    upper = bool(direction)
    kern = functools.partial(_gdn_tsolve_kernel, c=c, upper=upper)
    return pl.pallas_call(
        kern,
        grid=(pairs // blk,),
        in_specs=[pl.BlockSpec((None, c, blk, 2 * c), lambda i: (direction, 0, i, 0))],
        out_specs=pl.BlockSpec((c, blk, 2 * c), lambda i: (0, i, 0)),
        out_shape=jax.ShapeDtypeStruct(a.shape[1:], F32),
        scratch_shapes=[pltpu.VMEM((2, c, c, blk), F32), pltpu.VMEM((2, c, c, blk), F32)],
        compiler_params=pltpu.CompilerParams(dimension_semantics=("arbitrary",),
                                             vmem_limit_bytes=V7X_VMEM_LIMIT_BYTES),
        name="gdn_tsolve_bwd" if upper else "gdn_tsolve_fwd",
    )(a)


def _gdn_scan_kernel(q_ref, k_ref, v_ref, kt_ref, cols_ref, rows_ref, pairs_ref, t_ref, sel_ref, o_ref,
                     s_ref, g_ref, qe_ref, rhs_ref, attn_ref, u_ref, w_ref,
                     *, heads, dk, c, group, reverse):
    @pl.when(pl.program_id(1) == 0)
    def _():
        s_ref[...] = jnp.zeros_like(s_ref)

    g_all = _spread_exact(cols_ref[...], sel_ref[...])
    g_ref[...] = g_all
    eg = jnp.exp(g_all)
    qe_ref[...] = (q_ref[...] * eg).astype(BF16)
    ke = k_ref[...] * eg
    for hd in range(heads):
        lanes = slice(hd * dk, (hd + 1) * dk)
        rhs_ref[:, 2 * hd * dk:(2 * hd + 1) * dk] = v_ref[:, lanes].astype(BF16)
        rhs_ref[:, (2 * hd + 1) * dk:(2 * hd + 2) * dk] = ke[:, lanes].astype(BF16)

    row = lax.broadcasted_iota(jnp.int32, (c, c), 0)
    col = lax.broadcasted_iota(jnp.int32, (c, c), 1)
    incl = (row <= col) if reverse else (row >= col)
    head_lanes = [slice(hd * dk, (hd + 1) * dk) for hd in range(heads)]
    n_pair = heads // 2

    for ci in range(group):
        rows = slice(ci * c, (ci + 1) * c)
        for hd, lanes in enumerate(head_lanes):
            dec = _decay(g_ref[rows, hd * dk:hd * dk + c], rows_ref[ci, hd:hd + 1, :], incl)
            gram = lax.dot_general(q_ref[rows, lanes].astype(BF16), k_ref[rows, lanes].astype(BF16),
                                   (((1,), (1,)), ((), ())), preferred_element_type=F32)
            attn_ref[ci * heads + hd] = (gram * dec).astype(BF16)
        for m in range(n_pair):
            t_pair = t_ref[:, ci * n_pair + m, :] * pairs_ref[ci, n_pair + m:n_pair + m + 1, :]
            width = 2 * dk
            rhs_pair = _head_pair_block_diag(rhs_ref[rows, 2 * m * width:(2 * m + 1) * width],
                                             rhs_ref[rows, (2 * m + 1) * width:(2 * m + 2) * width])
            sol = jnp.dot(t_pair.astype(BF16), rhs_pair, preferred_element_type=F32)
            for par in range(2):
                lanes = head_lanes[2 * m + par]
                u_ref[rows, lanes] = sol[:, par * width:par * width + dk]
                w_ref[rows, lanes] = sol[:, par * width + dk:(par + 1) * width].astype(BF16)

    for step in range(group):
        ci = group - 1 - step if reverse else step
        rows = slice(ci * c, (ci + 1) * c)
        last = ci * c if reverse else (ci + 1) * c - 1
        states = [s_ref[hd] for hd in range(heads)]
        states_bf = [s.astype(BF16) for s in states]
        w_s = [jnp.dot(w_ref[rows, lanes], s, preferred_element_type=F32)
               for lanes, s in zip(head_lanes, states_bf)]
        q_s = [jnp.dot(qe_ref[rows, lanes], s, preferred_element_type=F32)
               for lanes, s in zip(head_lanes, states_bf)]
        for hd, lanes in enumerate(head_lanes):
            v_new_bf = (u_ref[rows, lanes] - w_s[hd]).astype(BF16)
            o_ref[rows, lanes] = q_s[hd] + jnp.dot(attn_ref[ci * heads + hd], v_new_bf,
                                                   preferred_element_type=F32)
            g_tot = g_ref[last:last + 1, lanes]
            k_dec_t = (kt_ref[ci, lanes, :] * jnp.exp(g_tot[:, :c] - rows_ref[ci, hd:hd + 1, :])).astype(BF16)
            s_ref[hd] = states[hd] * jnp.exp(g_tot) + jnp.dot(k_dec_t, v_new_bf, preferred_element_type=F32)


def _gdn_scan_block(q, k, v, kt, cols, rows_t, pairs, t, reverse):
    b, l, d = q.shape
    heads, c, group = GD_HEADS, GD_CHUNK, GD_GROUP
    dk = d // heads
    rows = c * group
    ng = l // rows
    per = group * heads
    dr = int(reverse)
    kern = functools.partial(_gdn_scan_kernel, heads=heads, dk=dk, c=c, group=group, reverse=reverse)
    visit = (lambda g: ng - 1 - g) if reverse else (lambda g: g)
    wide = pl.BlockSpec((None, rows, d), lambda bb, g: (bb, visit(g), 0))
    sel = jnp.repeat(jnp.eye(2 * heads, heads, dtype=BF16), dk, axis=1)
    return pl.pallas_call(
        kern,
        grid=(b, ng),
        in_specs=[wide, wide, wide,
                  pl.BlockSpec((None, group, d, c), lambda bb, g: (bb, visit(g), 0, 0)),
                  pl.BlockSpec((None, None, rows, 2 * heads), lambda bb, g: (dr, bb, visit(g), 0)),
                  pl.BlockSpec((None, None, group, 2 * heads, c), lambda bb, g: (dr, bb, visit(g), 0, 0)),
                  pl.BlockSpec((None, None, group, heads, 2 * c), lambda bb, g: (dr, bb, visit(g), 0, 0)),
                  pl.BlockSpec((c, per // 2, 2 * c), lambda bb, g: (0, bb * ng + visit(g), 0)),
                  pl.BlockSpec((2 * heads, d), lambda bb, g: (0, 0))],
        out_specs=wide,
        out_shape=jax.ShapeDtypeStruct((b, l, d), F32),
        scratch_shapes=[pltpu.VMEM((heads, dk, dk), F32), pltpu.VMEM((rows, d), F32),
                        pltpu.VMEM((rows, d), BF16), pltpu.VMEM((rows, 2 * d), BF16),
                        pltpu.VMEM((per, c, c), BF16),
                        pltpu.VMEM((rows, d), F32), pltpu.VMEM((rows, d), BF16)],
        compiler_params=_mixer_params(),
        name="gdn_scan_bwd" if reverse else "gdn_scan_fwd",
    )(q, k, v, kt, cols, rows_t, pairs, t, sel)


def _gdn_out_kernel(x_ref, of_ref, ob_ref, z_ref, ng_ref, wout_ref, y_ref, g_ref, *, heads, dk):
    o = of_ref[...] + ob_ref[...]
    z = z_ref[...]
    for hd in range(heads):
        lanes = slice(hd * dk, (hd + 1) * dk)
        oh = o[:, lanes]
        ms = jnp.mean(oh * oh, axis=-1, keepdims=True)
        zh = z[:, lanes]
        g_ref[:, lanes] = (oh * lax.rsqrt(ms + NORM_EPS) * ng_ref[...] * (zh * jax.nn.sigmoid(zh))).astype(BF16)
    y_ref[...] = x_ref[...] + jnp.dot(g_ref[...], wout_ref[...], preferred_element_type=F32)


def _gdn_out_block(x, o_fwd, o_bwd, z, norm_g, w_out):
    b, l, d = x.shape
    heads = GD_HEADS
    dk = d // heads
    tm = MIX_ROW_TILE
    row = pl.BlockSpec((None, tm, d), lambda bb, i: (bb, i, 0))
    kern = functools.partial(_gdn_out_kernel, heads=heads, dk=dk)
    return pl.pallas_call(
        kern,
        grid=(b, l // tm),
        in_specs=[row, row, row, row,
                  pl.BlockSpec((1, dk), lambda bb, i: (0, 0)),
                  pl.BlockSpec((d, d), lambda bb, i: (0, 0), **_RESIDENT)],
        out_specs=row,
        out_shape=jax.ShapeDtypeStruct((b, l, d), F32),
        scratch_shapes=[pltpu.VMEM((tm, d), BF16)],
        compiler_params=_mixer_params(),
        name="gdn_out",
    )(x, o_fwd, o_bwd, z, norm_g.reshape(1, dk), w_out.astype(BF16))


def _gdn_block(x, gain, w_in, lead, conv_w, a_log, dt_bias, norm_g, w_out):
    q, k, v, z, kt, cols, rows_t, pairs = _gdn_in_block(x, gain, w_in, lead, conv_w, a_log, dt_bias)
    a = _gdn_prep_block(k, cols, pairs)
    outs = [_gdn_scan_block(q, k, v, kt, cols, rows_t, pairs, _gdn_tsolve_block(a, dr), reverse=bool(dr))
            for dr in range(2)]
    return _gdn_out_block(x, outs[0], outs[1], z, norm_g, w_out)


def kernel(x, norms, final_norm, ffn_w_in, ffn_w_out, sc_w_in, sc_conv, sc_w_out, hy_w_in, hy_b_in, hy_conv, hy_conv_b, hy_f_w1, hy_f_b1, hy_f_w2, hy_f_b2, hy_f_w3, hy_f_freq, hy_d, hy_w_out, hy_b_out, gd_w_in, gd_conv, gd_a_log, gd_dt_bias, gd_norm, gd_w_out):
    b, l, d = x.shape
    depth = norms.shape[0]

    def ffn(xx, i, k):
        final = final_norm if (i == depth - 1 and k == 1) else None
        return _ffn_block(xx.reshape(b * l, d), norms[i, 2 * k], ffn_w_in, ffn_w_out, (i, k),
                          final).reshape(b, l, d)

    for i in range(depth):
        m, j = i % N_MIXERS, i // N_MIXERS
        x = ffn(x, i, 0)
        if m == 0:
            x = _sc_mixer_block(x, norms[i, 1], sc_w_in[j], sc_conv[j], sc_w_out[j])
        elif m == 1:
            x = _hyena_block(x, norms[i, 1], hy_w_in[j], hy_b_in[j], hy_conv[j], hy_conv_b[j],
                             hy_f_w1[j], hy_f_b1[j], hy_f_w2[j], hy_f_b2[j], hy_f_w3[j],
                             hy_f_freq[j], hy_d[j], hy_w_out[j], hy_b_out[j])
        else:
            x = _gdn_block(x, norms[i, 1], gd_w_in, (j,), gd_conv[j], gd_a_log[j], gd_dt_bias[j],
                           gd_norm[j], gd_w_out[j])
        x = ffn(x, i, 1)
    return x
```

```python
import functools
import math

import jax
import jax.numpy as jnp
from jax import lax
from jax.experimental import pallas as pl
from jax.experimental.pallas import tpu as pltpu

F32 = jnp.float32
BF16 = jnp.bfloat16

NORM_EPS = 1e-6
N_MIXERS = 3
HY_EMB = 33
HY_BANDS = (HY_EMB - 1) // 2
HY_MAX_DECAY = math.log(1e-2) / 0.3
HY_MIN_DECAY = math.log(1e-2) / 1.5
GD_HEADS = 8
GD_CHUNK = 64

V7X_VMEM_LIMIT_BYTES = 56 * 1024 * 1024
FFN_ROW_TILE = 512
FFN_HIDDEN_TILE = 256


def _ffn_kernel(x_ref, gain_ref, win_ref, wout_ref, *rest, n_hidden_tiles, hidden_tile, final_norm):
    final_ref, o_ref, hid_ref = rest if final_norm else (None,) + rest
    x = x_ref[...]
    xn = _rms_rows(x, gain_ref[...]).astype(BF16)
    f = n_hidden_tiles * hidden_tile
    for j in range(n_hidden_tiles):
        cols = slice(j * hidden_tile, (j + 1) * hidden_tile)
        up_cols = slice(f + j * hidden_tile, f + (j + 1) * hidden_tile)
        g = jnp.dot(xn, win_ref[:, cols].astype(BF16), preferred_element_type=F32)
        u = jnp.dot(xn, win_ref[:, up_cols].astype(BF16), preferred_element_type=F32)
        hid_ref[:, cols] = (g * jax.nn.sigmoid(g) * u).astype(BF16)
    y = x + 0.5 * jnp.dot(hid_ref[...], wout_ref[...].astype(BF16), preferred_element_type=F32)
    o_ref[...] = _rms_rows(y, final_ref[...]) if final_norm else y


def _stacked_weight_spec(w, lead):
    tail = w.shape[len(lead):]
    index = tuple(lead) + (0,) * len(tail)
    return pl.BlockSpec((None,) * len(lead) + tail, lambda *_: index, **_RESIDENT)


def _ffn_block(x2d, gain, w_in, w_out, lead, final_gain=None):
    t, d = x2d.shape
    f = w_out.shape[-2]
    tm = FFN_ROW_TILE
    final_norm = final_gain is not None
    kern = functools.partial(_ffn_kernel, n_hidden_tiles=f // FFN_HIDDEN_TILE, hidden_tile=FFN_HIDDEN_TILE,
                             final_norm=final_norm)
    vec = pl.BlockSpec((1, d), lambda i: (0, 0))
    extra = ([vec], [final_gain.reshape(1, d)]) if final_norm else ([], [])
    return pl.pallas_call(
        kern,
        grid=(t // tm,),
        in_specs=[
            pl.BlockSpec((tm, d), lambda i: (i, 0)),
            vec,
            _stacked_weight_spec(w_in, lead),
            _stacked_weight_spec(w_out, lead),
        ] + extra[0],
        out_specs=pl.BlockSpec((tm, d), lambda i: (i, 0)),
        out_shape=jax.ShapeDtypeStruct((t, d), F32),
        scratch_shapes=[pltpu.VMEM((tm, f), BF16)],
        compiler_params=pltpu.CompilerParams(
            dimension_semantics=("arbitrary",), vmem_limit_bytes=V7X_VMEM_LIMIT_BYTES),
        name="ffn_final" if final_norm else "ffn_block",
    )(x2d, gain.reshape(1, d), w_in, w_out, *extra[1])


MIX_ROW_TILE = 1024
GD_ROW_TILE = 512
HALO = 8


def _rms_rows(x, gain):
    ms = jnp.mean(x * x, axis=-1, keepdims=True)
    return x * lax.rsqrt(ms + NORM_EPS) * gain


def _halo_specs(tm, d, seq_len):
    per = tm // HALO
    last_blk = seq_len // HALO - 1
    return [
        pl.BlockSpec((None, HALO, d), lambda b, i: (b, jnp.maximum(i * per - 1, 0), 0)),
        pl.BlockSpec((None, tm, d), lambda b, i: (b, i, 0)),
        pl.BlockSpec((None, HALO, d), lambda b, i: (b, jnp.minimum((i + 1) * per, last_blk), 0)),
    ]


def _store_with_halo(s_ref, main, halo, tm):
    i = pl.program_id(1)
    last = pl.num_programs(1) - 1
    s_ref[0:HALO, :] = jnp.where(i > 0, halo[:HALO], 0.0)
    s_ref[HALO:HALO + tm, :] = main
    s_ref[HALO + tm:2 * HALO + tm, :] = jnp.where(i < last, halo[HALO:], 0.0)


def _conv3_from(s_ref, cw, tm):
    return (cw[0:1] * s_ref[HALO - 1:HALO - 1 + tm, :] + cw[1:2] * s_ref[HALO:HALO + tm, :]
            + cw[2:3] * s_ref[HALO + 1:HALO + 1 + tm, :])


_RESIDENT = dict(pipeline_mode=pl.Buffered(1))


def _mixer_params():
    return pltpu.CompilerParams(dimension_semantics=("arbitrary", "arbitrary"),
                                vmem_limit_bytes=V7X_VMEM_LIMIT_BYTES)


def _sc_mixer_kernel(xp_ref, x_ref, xn_ref, gain_ref, win_ref, cw_ref, wout_ref, o_ref, ch_ref, *, tm, d):
    gain = gain_ref[...]
    x = x_ref[...]
    h = _rms_rows(x, gain).astype(BF16)
    c_main = jnp.dot(h, win_ref[:, d:2 * d], preferred_element_type=F32)
    h_main = jnp.dot(h, win_ref[:, 2 * d:], preferred_element_type=F32)
    hh = _rms_rows(jnp.concatenate([xp_ref[...], xn_ref[...]], axis=0), gain).astype(BF16)
    c_halo = jnp.dot(hh, win_ref[:, d:2 * d], preferred_element_type=F32)
    h_halo = jnp.dot(hh, win_ref[:, 2 * d:], preferred_element_type=F32)
    _store_with_halo(ch_ref, c_main * h_main, c_halo * h_halo, tm)
    conv = _conv3_from(ch_ref, cw_ref[...], tm)
    b_main = jnp.dot(h, win_ref[:, :d], preferred_element_type=F32)
    y = jnp.dot((b_main * conv).astype(BF16), wout_ref[...], preferred_element_type=F32)
    o_ref[...] = x + y


def _sc_mixer_block(x, gain, w_in, conv_w, w_out):
    b, l, d = x.shape
    tm = MIX_ROW_TILE
    kern = functools.partial(_sc_mixer_kernel, tm=tm, d=d)
    return pl.pallas_call(
        kern,
        grid=(b, l // tm),
        in_specs=_halo_specs(tm, d, l) + [
            pl.BlockSpec((1, d), lambda bb, i: (0, 0)),
            pl.BlockSpec((d, 3 * d), lambda bb, i: (0, 0), **_RESIDENT),
            pl.BlockSpec((3, d), lambda bb, i: (0, 0)),
            pl.BlockSpec((d, d), lambda bb, i: (0, 0), **_RESIDENT),
        ],
        out_specs=pl.BlockSpec((None, tm, d), lambda bb, i: (bb, i, 0)),
        out_shape=jax.ShapeDtypeStruct((b, l, d), F32),
        scratch_shapes=[pltpu.VMEM((tm + 2 * HALO, d), F32)],
        compiler_params=_mixer_params(),
        name="sc_mixer",
    )(x, x, x, gain.reshape(1, d), w_in.astype(BF16), conv_w, w_out.astype(BF16))


def _hy_in_kernel(xp_ref, x_ref, xn_ref, gain_ref, win_ref, bin_ref, cw_ref, cb_ref, x0_ref, vxt_ref, s_ref,
                  *, tm, d):
    gain = gain_ref[...]
    h = _rms_rows(x_ref[...], gain).astype(BF16)
    hh = _rms_rows(jnp.concatenate([xp_ref[...], xn_ref[...]], axis=0), gain).astype(BF16)
    parts = []
    for j in range(3):
        cols = slice(j * d, (j + 1) * d)
        bias = bin_ref[:, cols]
        main = jnp.dot(h, win_ref[:, cols], preferred_element_type=F32) + bias
        halo = jnp.dot(hh, win_ref[:, cols], preferred_element_type=F32) + bias
        _store_with_halo(s_ref, main, halo, tm)
        parts.append(_conv3_from(s_ref, cw_ref[:, cols], tm) + cb_ref[:, cols])
    x0_ref[...] = parts[0]
    vxt_ref[...] = (parts[2] * parts[1]).T


def _hy_in_block(x, gain, w_in, b_in, conv_w, conv_b):
    b, l, d = x.shape
    tm = MIX_ROW_TILE
    kern = functools.partial(_hy_in_kernel, tm=tm, d=d)
    return pl.pallas_call(
        kern,
        grid=(b, l // tm),
        in_specs=_halo_specs(tm, d, l) + [
            pl.BlockSpec((1, d), lambda bb, i: (0, 0)),
            pl.BlockSpec((d, 3 * d), lambda bb, i: (0, 0), **_RESIDENT),
            pl.BlockSpec((1, 3 * d), lambda bb, i: (0, 0)),
            pl.BlockSpec((3, 3 * d), lambda bb, i: (0, 0)),
            pl.BlockSpec((1, 3 * d), lambda bb, i: (0, 0)),
        ],
        out_specs=[pl.BlockSpec((None, tm, d), lambda bb, i: (bb, i, 0)),
                   pl.BlockSpec((None, d, tm), lambda bb, i: (bb, 0, i))],
        out_shape=[jax.ShapeDtypeStruct((b, l, d), F32), jax.ShapeDtypeStruct((b, d, l), F32)],
        scratch_shapes=[pltpu.VMEM((tm + 2 * HALO, d), F32)],
        compiler_params=_mixer_params(),
        name="hyena_in",
    )(x, x, x, gain.reshape(1, d), w_in.astype(BF16), b_in.reshape(1, 3 * d), conv_w, conv_b.reshape(1, 3 * d))


HY_FEAT_PAD = 64
HY_FILTER_TILE = 2048


def _hy_filter_kernel(zt_ref, w1_ref, b1_ref, w2_ref, b2_ref, freq_ref, w3_ref, delta_ref, hc_ref):
    def t_dot(w, a):
        return lax.dot_general(w, a, (((0,), (0,)), ((), ())), precision=lax.Precision.HIGHEST,
                               preferred_element_type=F32)

    z = zt_ref[...]
    freq = freq_ref[...]
    h1 = jnp.sin(freq * (t_dot(w1_ref[...], z) + b1_ref[...]))
    h2 = jnp.sin(freq * (t_dot(w2_ref[...], h1) + b2_ref[...]))
    h = t_dot(w3_ref[...], h2)
    t_row = z[0:1, :]
    mask_row = z[HY_EMB:HY_EMB + 1, :]
    hc_ref[...] = h * jnp.exp(-t_row * delta_ref[...]) * mask_row


def _hy_filter_features(l):
    p = jnp.arange(2 * l)
    pos = jnp.where(p < l, p, 2 * l - p)
    valid = (p != l).astype(F32)
    pos = jnp.minimum(pos, l - 1).astype(F32)[None, :]
    t = pos / (l - 1)
    w = 2.0 * math.pi * pos / l
    f = jnp.linspace(1e-4, HY_BANDS - 1, HY_BANDS, dtype=F32)[:, None]
    z = jnp.concatenate([t, jnp.cos(f * w), -jnp.sin(f * w), valid[None, :]], axis=0)
    return jnp.pad(z, ((0, HY_FEAT_PAD - z.shape[0]), (0, 0)))


def _hy_filter_block(l, w1, b1, w2, b2, w3, freq):
    order = w1.shape[1]
    d = w3.shape[1] // 2
    tl = min(HY_FILTER_TILE, l)
    half = l // tl
    zt = _hy_filter_features(l)
    w1p = jnp.pad(w1, ((0, HY_FEAT_PAD - w1.shape[0]), (0, 0)))
    delta = jnp.abs(jnp.linspace(HY_MIN_DECAY, HY_MAX_DECAY, d, dtype=F32)).reshape(d, 1)
    col = lambda v: v.reshape(order, 1)
    full = lambda shape: pl.BlockSpec(shape, lambda i: (0,) * len(shape))
    return pl.pallas_call(
        _hy_filter_kernel,
        grid=(2 * half,),
        in_specs=[pl.BlockSpec((HY_FEAT_PAD, tl), lambda i: (0, i)),
                  full((HY_FEAT_PAD, order)), full((order, 1)), full((order, order)), full((order, 1)),
                  full((order, 1)),
                  pl.BlockSpec((order, d), lambda i: (0, i // half)),
                  full((d, 1))],
        out_specs=pl.BlockSpec((d, tl), lambda i: (0, i)),
        out_shape=jax.ShapeDtypeStruct((d, 2 * l), F32),
        compiler_params=pltpu.CompilerParams(dimension_semantics=("arbitrary",),
                                             vmem_limit_bytes=V7X_VMEM_LIMIT_BYTES),
        name="hyena_filter",
    )(zt, w1p, col(b1), w2, col(b2), col(freq), w3, delta)


HY_CONV_CH_TILE = 16


def _dft_tables(n1, n2):
    import numpy as np
    n = n1 * n2
    i1, i2 = np.arange(n1), np.arange(n2)
    a1 = 2.0 * np.pi * np.outer(i1, i1) / n1
    a2 = 2.0 * np.pi * np.outer(i2, i2) / n2
    c1, s1, c2, s2 = np.cos(a1), np.sin(a1), np.cos(a2), np.sin(a2)
    tw = 2.0 * np.pi * np.outer(i2, i1) / n
    f_real = np.concatenate([c1, -s1], axis=1)
    f_cplx = np.block([[c2, -s2], [s2, c2]])
    i_cplx = np.block([[c2, s2], [-s2, c2]])
    i_real = np.concatenate([c1, -s1], axis=0) / n
    as_bf = lambda a: jnp.asarray(a, dtype=F32).astype(BF16)
    as_f = lambda a: jnp.asarray(a, dtype=F32)
    return (as_bf(f_real), as_bf(f_cplx), as_bf(i_cplx), as_bf(i_real),
            as_f(np.cos(tw)), as_f(np.sin(tw)), as_f(np.cos(tw.T)), as_f(np.sin(tw.T)))


def _swap(a):
    return jnp.swapaxes(a, 1, 2)


def _hy_conv_kernel(ut_ref, hc_ref, dbias_ref, freal_ref, fcplx_ref, icplx_ref, ireal_ref, twc_ref, tws_ref,
                    twct_ref, twst_ref, yt_ref, hspec_ref, *, ct, n1, n2):
    rows = ct * n2
    twc = twc_ref[...]
    tws = tws_ref[...]

    def mm(a, m_ref):
        return jnp.dot(a.astype(BF16), m_ref[...], preferred_element_type=F32)

    def forward(x):
        p = mm(_swap(x).reshape(rows, n1), freal_ref)
        pr = p[:, :n1].reshape(ct, n2, n1)
        pi = p[:, n1:].reshape(ct, n2, n1)
        qr = pr * twc + pi * tws
        qi = pi * twc - pr * tws
        q = jnp.concatenate([_swap(qr).reshape(ct * n1, n2), _swap(qi).reshape(ct * n1, n2)], axis=1)
        xs = mm(q, fcplx_ref)
        return xs[:, :n2].reshape(ct, n1, n2), xs[:, n2:].reshape(ct, n1, n2)

    @pl.when(pl.program_id(1) == 0)
    def _():
        hr, hi = forward(hc_ref[...])
        hspec_ref[0] = hr
        hspec_ref[1] = hi

    u = ut_ref[...]
    xr, xi = forward(jnp.concatenate([u, jnp.zeros_like(u)], axis=1))
    hr = hspec_ref[0]
    hi = hspec_ref[1]
    yr = xr * hr - xi * hi
    yi = xr * hi + xi * hr
    r = mm(jnp.concatenate([yr.reshape(ct * n1, n2), yi.reshape(ct * n1, n2)], axis=1), icplx_ref)
    rr = r[:, :n2].reshape(ct, n1, n2)
    ri = r[:, n2:].reshape(ct, n1, n2)
    twct = twct_ref[...]
    twst = twst_ref[...]
    sr = rr * twct - ri * twst
    si = ri * twct + rr * twst
    s = jnp.concatenate([_swap(sr).reshape(rows, n1), _swap(si).reshape(rows, n1)], axis=1)
    z = mm(s, ireal_ref).reshape(ct, n2, n1)
    y = _swap(z)[:, :n1 // 2, :]
    yt_ref[...] = y + u * dbias_ref[...]


def _hy_conv_block(ut, hc, d_bias):
    b, d, l = ut.shape
    n2 = 128
    n1 = 2 * l // n2
    ct = HY_CONV_CH_TILE
    tables = _dft_tables(n1, n2)
    kern = functools.partial(_hy_conv_kernel, ct=ct, n1=n1, n2=n2)
    full = lambda a: pl.BlockSpec(a.shape, lambda c, bb: (0,) * a.ndim)
    yt = pl.pallas_call(
        kern,
        grid=(d // ct, b),
        in_specs=[pl.BlockSpec((None, ct, n1 // 2, n2), lambda c, bb: (bb, c, 0, 0)),
                  pl.BlockSpec((ct, n1, n2), lambda c, bb: (c, 0, 0)),
                  pl.BlockSpec((ct, 1, 1), lambda c, bb: (c, 0, 0))] + [full(t) for t in tables],
        out_specs=pl.BlockSpec((None, ct, n1 // 2, n2), lambda c, bb: (bb, c, 0, 0)),
        out_shape=jax.ShapeDtypeStruct((b, d, n1 // 2, n2), F32),
        scratch_shapes=[pltpu.VMEM((2, ct, n1, n2), F32)],
        compiler_params=_mixer_params(),
        name="hyena_longconv",
    )(ut.reshape(b, d, n1 // 2, n2), hc.reshape(d, n1, n2), d_bias.reshape(d, 1, 1), *tables)
    return yt.reshape(b, d, l)


def _hyena_block(x, gain, w_in, b_in, conv_w, conv_b, f_w1, f_b1, f_w2, f_b2, f_w3, f_freq, d_bias, w_out, b_out):
    l = x.shape[1]
    x0, ut = _hy_in_block(x, gain, w_in, b_in, conv_w, conv_b)
    hc = _hy_filter_block(l, f_w1, f_b1, f_w2, f_b2, f_w3, f_freq)
    yt = _hy_conv_block(ut, hc, d_bias)
    return _hy_out_block(x, x0, yt, w_out, b_out)


def _hy_out_kernel(x_ref, x0_ref, yt_ref, wout_ref, bout_ref, o_ref):
    y = yt_ref[...].T
    z = jnp.dot((y * x0_ref[...]).astype(BF16), wout_ref[...], preferred_element_type=F32)
    o_ref[...] = x_ref[...] + z + bout_ref[...]


def _hy_out_block(x, x0, yt, w_out, b_out):
    b, l, d = x.shape
    tm = MIX_ROW_TILE
    row = pl.BlockSpec((None, tm, d), lambda bb, i: (bb, i, 0))
    return pl.pallas_call(
        _hy_out_kernel,
        grid=(b, l // tm),
        in_specs=[row, row, pl.BlockSpec((None, d, tm), lambda bb, i: (bb, 0, i)),
                  pl.BlockSpec((d, d), lambda bb, i: (0, 0), **_RESIDENT),
                  pl.BlockSpec((1, d), lambda bb, i: (0, 0))],
        out_specs=row,
        out_shape=jax.ShapeDtypeStruct((b, l, d), F32),
        compiler_params=_mixer_params(),
        name="hyena_out",
    )(x, x0, yt, w_out.astype(BF16), b_out.reshape(1, d))


LANES = 128


def _softplus(x):
    return jnp.maximum(x, 0.0) + jnp.log1p(jnp.exp(-jnp.abs(x)))


def _gdn_in_kernel(xp_ref, x_ref, xn_ref, gain_ref, win_ref, cw_ref, alog_ref, dtb_ref, cum_ref, eye_ref,
                   q_ref, k_ref, v_ref, z_ref, kt_ref, cols_ref, rows_ref, pairs_ref, s_ref, *, tm, d, heads):
    hi = lax.Precision.HIGHEST
    gain = gain_ref[...]
    h = _rms_rows(x_ref[...], gain).astype(BF16)
    hh = _rms_rows(jnp.concatenate([xp_ref[...], xn_ref[...]], axis=0), gain).astype(BF16)
    dk = d // heads
    for j, (out_ref, scale) in enumerate(((q_ref, dk ** -0.5), (k_ref, 1.0), (v_ref, None))):
        cols = slice(j * d, (j + 1) * d)
        w_cols = win_ref[:, cols].astype(BF16)
        main = jnp.dot(h, w_cols, preferred_element_type=F32)
        halo = jnp.dot(hh, w_cols, preferred_element_type=F32)
        _store_with_halo(s_ref, main, halo, tm)
        c = _conv3_from(s_ref, cw_ref[:, cols], tm)
        c = c * jax.nn.sigmoid(c)
        if scale is None:
            out_ref[...] = c
        else:
            for hd in range(heads):
                ch = c[:, hd * dk:(hd + 1) * dk]
                ss = jnp.sum(ch * ch, axis=-1, keepdims=True)
                out_ref[:, hd * dk:(hd + 1) * dk] = ch * (lax.rsqrt(ss + 1e-6) * scale)
            if j == 1:
                kt = out_ref[...].T
                n_chunk, _, c = kt_ref.shape
                for ci in range(n_chunk):
                    kt_ref[ci] = kt[:, ci * c:(ci + 1) * c]
    z_ref[...] = jnp.dot(h, win_ref[:, 3 * d:4 * d].astype(BF16), preferred_element_type=F32)
    ab = jnp.dot(h, win_ref[:, 4 * d:].astype(BF16), preferred_element_type=F32)
    nh2 = 2 * heads
    g = -jnp.exp(alog_ref[...]) * _softplus(ab[:, :nh2] + dtb_ref[...])
    beta = jax.nn.sigmoid(ab[:, nh2:2 * nh2])
    gc_f = jnp.dot(cum_ref[0], g, precision=hi, preferred_element_type=F32)
    gc_b = jnp.dot(cum_ref[1], g, precision=hi, preferred_element_type=F32)
    chunk = tm // rows_ref.shape[1]
    for dr, gc in enumerate((gc_f, gc_b)):
        own = slice(dr * heads, (dr + 1) * heads)
        cols = jnp.concatenate([gc[:, own], beta[:, own]], axis=1)
        cols_ref[dr] = cols
        rows = lax.dot_general(eye_ref[...], cols, (((1,), (1,)), ((), ())), precision=hi,
                               preferred_element_type=F32)
        for ci in range(tm // chunk):
            chunk_rows = rows[:, ci * chunk:(ci + 1) * chunk]
            rows_ref[dr, ci] = chunk_rows
            pairs_ref[dr, ci] = jnp.concatenate(
                [jnp.concatenate([chunk_rows[r:r + 1], chunk_rows[r + 1:r + 2]], axis=1)
                 for r in range(0, nh2, 2)], axis=0)


def _chunk_cumsum_masks(tm, chunk):
    import numpy as np
    r = np.arange(tm)
    same = (r[:, None] // chunk) == (r[None, :] // chunk)
    lower = same & (r[None, :] <= r[:, None])
    upper = same & (r[None, :] >= r[:, None])
    return jnp.asarray(np.stack([lower, upper]).astype(np.float32))


def _gdn_in_block(x, gain, w_in, lead, conv_w, a_log, dt_bias):
    b, l, d = x.shape
    heads = GD_HEADS
    nh2 = 2 * heads
    tm = GD_ROW_TILE
    kern = functools.partial(_gdn_in_kernel, tm=tm, d=d, heads=heads)
    row = pl.BlockSpec((None, tm, d), lambda bb, i: (bb, i, 0))
    wide = jax.ShapeDtypeStruct((b, l, d), F32)
    c = GD_CHUNK
    return pl.pallas_call(
        kern,
        grid=(b, l // tm),
        in_specs=_halo_specs(tm, d, l) + [
            pl.BlockSpec((1, d), lambda bb, i: (0, 0)),
            _stacked_weight_spec(w_in, lead),
            pl.BlockSpec((3, 3 * d), lambda bb, i: (0, 0)),
            pl.BlockSpec((1, nh2), lambda bb, i: (0, 0)),
            pl.BlockSpec((1, nh2), lambda bb, i: (0, 0)),
            pl.BlockSpec((2, tm, tm), lambda bb, i: (0, 0, 0), **_RESIDENT),
            pl.BlockSpec((nh2, nh2), lambda bb, i: (0, 0)),
        ],
        out_specs=[row, row, row, row,
                   pl.BlockSpec((None, tm // c, d, c), lambda bb, i: (bb, i, 0, 0)),
                   pl.BlockSpec((2, None, tm, nh2), lambda bb, i: (0, bb, i, 0)),
                   pl.BlockSpec((2, None, tm // c, nh2, c), lambda bb, i: (0, bb, i, 0, 0)),
                   pl.BlockSpec((2, None, tm // c, heads, 2 * c), lambda bb, i: (0, bb, i, 0, 0))],
        out_shape=[wide, wide, wide, wide, jax.ShapeDtypeStruct((b, l // c, d, c), F32),
                   jax.ShapeDtypeStruct((2, b, l, nh2), F32),
                   jax.ShapeDtypeStruct((2, b, l // c, nh2, c), F32),
                   jax.ShapeDtypeStruct((2, b, l // c, heads, 2 * c), F32)],
        scratch_shapes=[pltpu.VMEM((tm + 2 * HALO, d), F32)],
        compiler_params=_mixer_params(),
        name="gdn_in",
    )(x, x, x, gain.reshape(1, d), w_in, conv_w, a_log.reshape(1, nh2), dt_bias.reshape(1, nh2),
      _chunk_cumsum_masks(tm, GD_CHUNK), jnp.eye(nh2, dtype=F32))


GD_GROUP = 8


def _decay(gc_col, gc_row, after_or_same):
    return jnp.exp(jnp.where(after_or_same, gc_col - gc_row, -jnp.inf))


def _spread_exact(cols, sel):
    hi = cols.astype(BF16)
    rest = cols - hi.astype(F32)
    mid = rest.astype(BF16)
    lo = (rest - mid.astype(F32)).astype(BF16)
    part = lambda a: jnp.dot(a, sel, preferred_element_type=F32)
    return (part(hi) + part(mid)) + part(lo)


def _head_pair_block_diag(left, right):
    zero = jnp.zeros_like(left)
    return jnp.concatenate([jnp.concatenate([left, zero], axis=1),
                            jnp.concatenate([zero, right], axis=1)], axis=0)


def _gdn_prep_kernel(k_ref, cols_ref, pairs_ref, selg_ref, selb_ref, a_ref, *, heads, dk, c, group):
    direction = pl.program_id(0)
    row = lax.broadcasted_iota(jnp.int32, (c, 2 * c), 0)
    col = lax.broadcasted_iota(jnp.int32, (c, 2 * c), 1) % c
    delta = (row - col) * (1 - 2 * direction)
    cols = cols_ref[...]
    g_pairs = _spread_exact(cols, selg_ref[...])
    b_pairs = _spread_exact(cols, selb_ref[...])
    n_pair = heads // 2
    for ci in range(group):
        rows = slice(ci * c, (ci + 1) * c)
        for m in range(n_pair):
            lanes = slice(2 * c * m, 2 * c * (m + 1))
            kp = k_ref[rows, 2 * dk * m:2 * dk * (m + 1)].astype(BF16)
            gram = lax.dot_general(kp, _head_pair_block_diag(kp[:, :dk], kp[:, dk:]),
                                   (((1,), (1,)), ((), ())), preferred_element_type=F32)
            dec = jnp.exp(jnp.where(delta > 0, g_pairs[rows, lanes] - pairs_ref[ci, m:m + 1, :], -jnp.inf))
            a_ref[:, ci * n_pair + m, :] = b_pairs[rows, lanes] * gram * dec


def _pair_selectors(heads, c):
    eye = jnp.eye(2 * heads, heads, dtype=BF16)
    return jnp.repeat(eye, c, axis=1), jnp.repeat(jnp.roll(eye, heads, axis=0), c, axis=1)


def _gdn_prep_block(k, cols, pairs):
    b, l, d = k.shape
    heads, c, group = GD_HEADS, GD_CHUNK, GD_GROUP
    rows = c * group
    ng = l // rows
    kern = functools.partial(_gdn_prep_kernel, heads=heads, dk=d // heads, c=c, group=group)
    per = group * heads // 2
    full = lambda a: pl.BlockSpec(a.shape, lambda dr, bb, g: (0,) * a.ndim)
    selg, selb = _pair_selectors(heads, c)
    return pl.pallas_call(
        kern,
        grid=(2, b, ng),
        in_specs=[pl.BlockSpec((None, rows, d), lambda dr, bb, g: (bb, g, 0)),
                  pl.BlockSpec((None, None, rows, 2 * heads), lambda dr, bb, g: (dr, bb, g, 0)),
                  pl.BlockSpec((None, None, group, heads, 2 * c), lambda dr, bb, g: (dr, bb, g, 0, 0)),
                  full(selg), full(selb)],
        out_specs=pl.BlockSpec((None, c, per, 2 * c), lambda dr, bb, g: (dr, 0, bb * ng + g, 0)),
        out_shape=jax.ShapeDtypeStruct((2, c, b * ng * per, 2 * c), F32),
        compiler_params=pltpu.CompilerParams(dimension_semantics=("arbitrary",) * 3,
                                             vmem_limit_bytes=V7X_VMEM_LIMIT_BYTES),
        name="gdn_prep",
    )(k, cols, pairs, selg, selb)


SUBLANES = 8


def _gdn_tsolve_kernel(a_ref, t_ref, a3_ref, t3_ref, *, c, upper):
    p = a_ref.shape[1]
    for i in range(c):
        slab = a_ref[i].T
        a3_ref[0, i] = slab[:c]
        a3_ref[1, i] = slab[c:]
    t3_ref[...] = jnp.zeros_like(t3_ref)
    ntile = c // SUBLANES
    sub = lax.broadcasted_iota(jnp.int32, (SUBLANES, p), 0)
    order = range(c - 1, -1, -1) if upper else range(c)
    for i in order:
        solved = range(i + 1, c) if upper else range(i)
        lo, hi = (i // SUBLANES, ntile) if upper else (0, i // SUBLANES + 1)
        ti = i // SUBLANES
        for half in range(2):
            acc = [jnp.zeros((SUBLANES, p), F32) for _ in range(lo, hi)]
            for j in solved:
                jlo, jhi = (j // SUBLANES, ntile) if upper else (0, j // SUBLANES + 1)
                coef = a3_ref[half, i, j:j + 1, :]
                tj = t3_ref[half, j, jlo * SUBLANES:jhi * SUBLANES, :]
                for t in range(jlo, jhi):
                    acc[t - lo] = acc[t - lo] - coef * tj[(t - jlo) * SUBLANES:(t - jlo + 1) * SUBLANES]
            acc[ti - lo] = acc[ti - lo] + jnp.where(sub == i % SUBLANES, 1.0, 0.0)
            t3_ref[half, i, lo * SUBLANES:hi * SUBLANES, :] = jnp.concatenate(acc, axis=0)
    for i in range(c):
        t_ref[i] = jnp.concatenate([t3_ref[0, i], t3_ref[1, i]], axis=0).T


def _gdn_tsolve_block(a, direction):
    _, c, pairs, _ = a.shape
    blk = LANES
    upper = bool(direction)
    kern = functools.partial(_gdn_tsolve_kernel, c=c, upper=upper)
    return pl.pallas_call(
        kern,
        grid=(pairs // blk,),
        in_specs=[pl.BlockSpec((None, c, blk, 2 * c), lambda i: (direction, 0, i, 0))],
        out_specs=pl.BlockSpec((c, blk, 2 * c), lambda i: (0, i, 0)),
        out_shape=jax.ShapeDtypeStruct(a.shape[1:], F32),
        scratch_shapes=[pltpu.VMEM((2, c, c, blk), F32), pltpu.VMEM((2, c, c, blk), F32)],
        compiler_params=pltpu.CompilerParams(dimension_semantics=("arbitrary",),
                                             vmem_limit_bytes=V7X_VMEM_LIMIT_BYTES),
        name="gdn_tsolve_bwd" if upper else "gdn_tsolve_fwd",
    )(a)


def _gdn_scan_kernel(q_ref, k_ref, v_ref, kt_ref, cols_ref, rows_ref, pairs_ref, t_ref, sel_ref, *rest,
                     heads, dk, c, group, reverse, add_other):
    other_ref = rest[0] if add_other else None
    o_ref, s_ref, g_ref, qe_ref, rhs_ref, attn_ref, u_ref, w_ref = rest[int(add_other):]

    @pl.when(pl.program_id(1) == 0)
    def _():
        s_ref[...] = jnp.zeros_like(s_ref)

    g_all = _spread_exact(cols_ref[...], sel_ref[...])
    g_ref[...] = g_all
    eg = jnp.exp(g_all)
    qe_ref[...] = (q_ref[...] * eg).astype(BF16)
    ke = k_ref[...] * eg
    for hd in range(heads):
        lanes = slice(hd * dk, (hd + 1) * dk)
        rhs_ref[:, 2 * hd * dk:(2 * hd + 1) * dk] = v_ref[:, lanes].astype(BF16)
        rhs_ref[:, (2 * hd + 1) * dk:(2 * hd + 2) * dk] = ke[:, lanes].astype(BF16)

    row = lax.broadcasted_iota(jnp.int32, (c, c), 0)
    col = lax.broadcasted_iota(jnp.int32, (c, c), 1)
    incl = (row <= col) if reverse else (row >= col)
    head_lanes = [slice(hd * dk, (hd + 1) * dk) for hd in range(heads)]
    n_pair = heads // 2

    for ci in range(group):
        rows = slice(ci * c, (ci + 1) * c)
        for hd, lanes in enumerate(head_lanes):
            dec = _decay(g_ref[rows, hd * dk:hd * dk + c], rows_ref[ci, hd:hd + 1, :], incl)
            gram = lax.dot_general(q_ref[rows, lanes].astype(BF16), k_ref[rows, lanes].astype(BF16),
                                   (((1,), (1,)), ((), ())), preferred_element_type=F32)
            attn_ref[ci * heads + hd] = (gram * dec).astype(BF16)
        for m in range(n_pair):
            width = 2 * dk
            stage = (rows, slice(2 * m * dk, (2 * m + 1) * dk))
            u_ref[stage] = t_ref[:, ci * n_pair + m, :]
            t_pair = u_ref[stage] * pairs_ref[ci, n_pair + m:n_pair + m + 1, :]
            rhs_pair = _head_pair_block_diag(rhs_ref[rows, 2 * m * width:(2 * m + 1) * width],
                                             rhs_ref[rows, (2 * m + 1) * width:(2 * m + 2) * width])
            sol = jnp.dot(t_pair.astype(BF16), rhs_pair, preferred_element_type=F32)
            for par in range(2):
                lanes = head_lanes[2 * m + par]
                u_ref[rows, lanes] = sol[:, par * width:par * width + dk]
                w_ref[rows, lanes] = sol[:, par * width + dk:(par + 1) * width].astype(BF16)

    for step in range(group):
        ci = group - 1 - step if reverse else step
        rows = slice(ci * c, (ci + 1) * c)
        last = ci * c if reverse else (ci + 1) * c - 1
        states = [s_ref[hd] for hd in range(heads)]
        states_bf = [s.astype(BF16) for s in states]
        w_s = [jnp.dot(w_ref[rows, lanes], s, preferred_element_type=F32)
               for lanes, s in zip(head_lanes, states_bf)]
        q_s = [jnp.dot(qe_ref[rows, lanes], s, preferred_element_type=F32)
               for lanes, s in zip(head_lanes, states_bf)]
        for hd, lanes in enumerate(head_lanes):
            v_new_bf = (u_ref[rows, lanes] - w_s[hd]).astype(BF16)
            o = q_s[hd] + jnp.dot(attn_ref[ci * heads + hd], v_new_bf, preferred_element_type=F32)
            o_ref[rows, lanes] = o + other_ref[rows, lanes] if add_other else o
            g_tot = g_ref[last:last + 1, lanes]
            k_dec_t = (kt_ref[ci, lanes, :] * jnp.exp(g_tot[:, :c] - rows_ref[ci, hd:hd + 1, :])).astype(BF16)
            s_ref[hd] = states[hd] * jnp.exp(g_tot) + jnp.dot(k_dec_t, v_new_bf, preferred_element_type=F32)


def _gdn_scan_block(q, k, v, kt, cols, rows_t, pairs, t, reverse, other=None):
    b, l, d = q.shape
    heads, c, group = GD_HEADS, GD_CHUNK, GD_GROUP
    dk = d // heads
    rows = c * group
    ng = l // rows
    per = group * heads
    dr = int(reverse)
    add_other = other is not None
    kern = functools.partial(_gdn_scan_kernel, heads=heads, dk=dk, c=c, group=group, reverse=reverse,
                             add_other=add_other)
    visit = (lambda g: ng - 1 - g) if reverse else (lambda g: g)
    wide = pl.BlockSpec((None, rows, d), lambda bb, g: (bb, visit(g), 0))
    sel = jnp.repeat(jnp.eye(2 * heads, heads, dtype=BF16), dk, axis=1)
    return pl.pallas_call(
        kern,
        grid=(b, ng),
        in_specs=[wide, wide, wide,
                  pl.BlockSpec((None, group, d, c), lambda bb, g: (bb, visit(g), 0, 0)),
                  pl.BlockSpec((None, None, rows, 2 * heads), lambda bb, g: (dr, bb, visit(g), 0)),
                  pl.BlockSpec((None, None, group, 2 * heads, c), lambda bb, g: (dr, bb, visit(g), 0, 0)),
                  pl.BlockSpec((None, None, group, heads, 2 * c), lambda bb, g: (dr, bb, visit(g), 0, 0)),
                  pl.BlockSpec((c, per // 2, 2 * c), lambda bb, g: (0, bb * ng + visit(g), 0)),
                  pl.BlockSpec((2 * heads, d), lambda bb, g: (0, 0))] + [wide] * add_other,
        out_specs=wide,
        out_shape=jax.ShapeDtypeStruct((b, l, d), F32),
        scratch_shapes=[pltpu.VMEM((heads, dk, dk), F32), pltpu.VMEM((rows, d), F32),
                        pltpu.VMEM((rows, d), BF16), pltpu.VMEM((rows, 2 * d), BF16),
                        pltpu.VMEM((per, c, c), BF16),
                        pltpu.VMEM((rows, d), F32), pltpu.VMEM((rows, d), BF16)],
        compiler_params=_mixer_params(),
        name="gdn_scan_bwd" if reverse else "gdn_scan_fwd",
    )(q, k, v, kt, cols, rows_t, pairs, t, sel, *([other] if add_other else []))


def _gdn_out_kernel(x_ref, o_ref, z_ref, ng_ref, wout_ref, y_ref, g_ref, *, heads, dk):
    o = o_ref[...]
    z = z_ref[...]
    for hd in range(heads):
        lanes = slice(hd * dk, (hd + 1) * dk)
        oh = o[:, lanes]
        ms = jnp.mean(oh * oh, axis=-1, keepdims=True)
        zh = z[:, lanes]
        g_ref[:, lanes] = (oh * lax.rsqrt(ms + NORM_EPS) * ng_ref[...] * (zh * jax.nn.sigmoid(zh))).astype(BF16)
    y_ref[...] = x_ref[...] + jnp.dot(g_ref[...], wout_ref[...], preferred_element_type=F32)


def _gdn_out_block(x, o, z, norm_g, w_out):
    b, l, d = x.shape
    heads = GD_HEADS
    dk = d // heads
    tm = GD_ROW_TILE
    row = pl.BlockSpec((None, tm, d), lambda bb, i: (bb, i, 0))
    kern = functools.partial(_gdn_out_kernel, heads=heads, dk=dk)
    return pl.pallas_call(
        kern,
        grid=(b, l // tm),
        in_specs=[row, row, row,
                  pl.BlockSpec((1, dk), lambda bb, i: (0, 0)),
                  pl.BlockSpec((d, d), lambda bb, i: (0, 0), **_RESIDENT)],
        out_specs=row,
        out_shape=jax.ShapeDtypeStruct((b, l, d), F32),
        scratch_shapes=[pltpu.VMEM((tm, d), BF16)],
        compiler_params=_mixer_params(),
        name="gdn_out",
    )(x, o, z, norm_g.reshape(1, dk), w_out.astype(BF16))


def _gdn_block(x, gain, w_in, lead, conv_w, a_log, dt_bias, norm_g, w_out):
    q, k, v, z, kt, cols, rows_t, pairs = _gdn_in_block(x, gain, w_in, lead, conv_w, a_log, dt_bias)
    a = _gdn_prep_block(k, cols, pairs)
    o = None
    for dr in range(2):
        o = _gdn_scan_block(q, k, v, kt, cols, rows_t, pairs, _gdn_tsolve_block(a, dr), bool(dr), other=o)
    return _gdn_out_block(x, o, z, norm_g, w_out)


def kernel(x, norms, final_norm, ffn_w_in, ffn_w_out, sc_w_in, sc_conv, sc_w_out, hy_w_in, hy_b_in, hy_conv, hy_conv_b, hy_f_w1, hy_f_b1, hy_f_w2, hy_f_b2, hy_f_w3, hy_f_freq, hy_d, hy_w_out, hy_b_out, gd_w_in, gd_conv, gd_a_log, gd_dt_bias, gd_norm, gd_w_out):
    b, l, d = x.shape
    depth = norms.shape[0]

    def ffn(xx, i, k):
        final = final_norm if (i == depth - 1 and k == 1) else None
        return _ffn_block(xx.reshape(b * l, d), norms[i, 2 * k], ffn_w_in, ffn_w_out, (i, k),
                          final).reshape(b, l, d)

    for i in range(depth):
        m, j = i % N_MIXERS, i // N_MIXERS
        x = ffn(x, i, 0)
        if m == 0:
            x = _sc_mixer_block(x, norms[i, 1], sc_w_in[j], sc_conv[j], sc_w_out[j])
        elif m == 1:
            x = _hyena_block(x, norms[i, 1], hy_w_in[j], hy_b_in[j], hy_conv[j], hy_conv_b[j],
                             hy_f_w1[j], hy_f_b1[j], hy_f_w2[j], hy_f_b2[j], hy_f_w3[j],
                             hy_f_freq[j], hy_d[j], hy_w_out[j], hy_b_out[j])
        else:
            x = _gdn_block(x, norms[i, 1], gd_w_in, (j,), gd_conv[j], gd_a_log[j], gd_dt_bias[j],
                           gd_norm[j], gd_w_out[j])
        x = ffn(x, i, 1)
    return x
```

```python
import functools
import math

import jax
import jax.numpy as jnp
from jax import lax
from jax.experimental import pallas as pl
from jax.experimental.pallas import tpu as pltpu

F32 = jnp.float32
BF16 = jnp.bfloat16

NORM_EPS = 1e-6
N_MIXERS = 3
HY_EMB = 33
HY_BANDS = (HY_EMB - 1) // 2
HY_MAX_DECAY = math.log(1e-2) / 0.3
HY_MIN_DECAY = math.log(1e-2) / 1.5
GD_HEADS = 8
GD_CHUNK = 64

V7X_VMEM_LIMIT_BYTES = 56 * 1024 * 1024
FFN_ROW_TILE = 512
FFN_HIDDEN_TILE = 256


def _ffn_kernel(x_ref, gain_ref, win_ref, wout_ref, *rest, n_hidden_tiles, hidden_tile, final_norm):
    final_ref, o_ref, hid_ref = rest if final_norm else (None,) + rest
    x = x_ref[...]
    xn = _rms_rows(x, gain_ref[...]).astype(BF16)
    f = n_hidden_tiles * hidden_tile
    for j in range(n_hidden_tiles):
        cols = slice(j * hidden_tile, (j + 1) * hidden_tile)
        up_cols = slice(f + j * hidden_tile, f + (j + 1) * hidden_tile)
        g = jnp.dot(xn, win_ref[:, cols].astype(BF16), preferred_element_type=F32)
        u = jnp.dot(xn, win_ref[:, up_cols].astype(BF16), preferred_element_type=F32)
        hid_ref[:, cols] = (g * jax.nn.sigmoid(g) * u).astype(BF16)
    y = x + 0.5 * jnp.dot(hid_ref[...], wout_ref[...].astype(BF16), preferred_element_type=F32)
    o_ref[...] = _rms_rows(y, final_ref[...]) if final_norm else y


def _stacked_weight_spec(w, lead):
    tail = w.shape[len(lead):]
    index = tuple(lead) + (0,) * len(tail)
    return pl.BlockSpec((None,) * len(lead) + tail, lambda *_: index, **_RESIDENT)


def _ffn_block(x2d, gain, w_in, w_out, lead, final_gain=None):
    t, d = x2d.shape
    f = w_out.shape[-2]
    tm = FFN_ROW_TILE
    final_norm = final_gain is not None
    kern = functools.partial(_ffn_kernel, n_hidden_tiles=f // FFN_HIDDEN_TILE, hidden_tile=FFN_HIDDEN_TILE,
                             final_norm=final_norm)
    vec = pl.BlockSpec((1, d), lambda i: (0, 0))
    extra = ([vec], [final_gain.reshape(1, d)]) if final_norm else ([], [])
    return pl.pallas_call(
        kern,
        grid=(t // tm,),
        in_specs=[
            pl.BlockSpec((tm, d), lambda i: (i, 0)),
            vec,
            _stacked_weight_spec(w_in, lead),
            _stacked_weight_spec(w_out, lead),
        ] + extra[0],
        out_specs=pl.BlockSpec((tm, d), lambda i: (i, 0)),
        out_shape=jax.ShapeDtypeStruct((t, d), F32),
        scratch_shapes=[pltpu.VMEM((tm, f), BF16)],
        compiler_params=pltpu.CompilerParams(
            dimension_semantics=("arbitrary",), vmem_limit_bytes=V7X_VMEM_LIMIT_BYTES),
        name="ffn_final" if final_norm else "ffn_block",
    )(x2d, gain.reshape(1, d), w_in, w_out, *extra[1])


MIX_ROW_TILE = 1024
GD_ROW_TILE = 512
HALO = 8


def _rms_rows(x, gain):
    ms = jnp.mean(x * x, axis=-1, keepdims=True)
    return x * lax.rsqrt(ms + NORM_EPS) * gain


def _bf16_terms(x):
    hi = x.astype(BF16)
    rest = x - hi.astype(F32)
    mid = rest.astype(BF16)
    lo = (rest - mid.astype(F32)).astype(BF16)
    return hi, mid, lo


def _halo_specs(tm, d, seq_len):
    per = tm // HALO
    last_blk = seq_len // HALO - 1
    return [
        pl.BlockSpec((None, HALO, d), lambda b, i: (b, jnp.maximum(i * per - 1, 0), 0)),
        pl.BlockSpec((None, tm, d), lambda b, i: (b, i, 0)),
        pl.BlockSpec((None, HALO, d), lambda b, i: (b, jnp.minimum((i + 1) * per, last_blk), 0)),
    ]


def _store_with_halo(s_ref, main, halo, tm):
    i = pl.program_id(1)
    last = pl.num_programs(1) - 1
    s_ref[0:HALO, :] = jnp.where(i > 0, halo[:HALO], 0.0)
    s_ref[HALO:HALO + tm, :] = main
    s_ref[HALO + tm:2 * HALO + tm, :] = jnp.where(i < last, halo[HALO:], 0.0)


def _conv3_from(s_ref, cw, tm):
    return (cw[0:1] * s_ref[HALO - 1:HALO - 1 + tm, :] + cw[1:2] * s_ref[HALO:HALO + tm, :]
            + cw[2:3] * s_ref[HALO + 1:HALO + 1 + tm, :])


_RESIDENT = dict(pipeline_mode=pl.Buffered(1))


def _mixer_params():
    return pltpu.CompilerParams(dimension_semantics=("arbitrary", "arbitrary"),
                                vmem_limit_bytes=V7X_VMEM_LIMIT_BYTES)


def _sc_mixer_kernel(xp_ref, x_ref, xn_ref, gain_ref, win_ref, cw_ref, wout_ref, o_ref, ch_ref, *, tm, d):
    gain = gain_ref[...]
    x = x_ref[...]
    h = _rms_rows(x, gain).astype(BF16)
    c_main = jnp.dot(h, win_ref[:, d:2 * d], preferred_element_type=F32)
    h_main = jnp.dot(h, win_ref[:, 2 * d:], preferred_element_type=F32)
    hh = _rms_rows(jnp.concatenate([xp_ref[...], xn_ref[...]], axis=0), gain).astype(BF16)
    c_halo = jnp.dot(hh, win_ref[:, d:2 * d], preferred_element_type=F32)
    h_halo = jnp.dot(hh, win_ref[:, 2 * d:], preferred_element_type=F32)
    _store_with_halo(ch_ref, c_main * h_main, c_halo * h_halo, tm)
    conv = _conv3_from(ch_ref, cw_ref[...], tm)
    b_main = jnp.dot(h, win_ref[:, :d], preferred_element_type=F32)
    y = jnp.dot((b_main * conv).astype(BF16), wout_ref[...], preferred_element_type=F32)
    o_ref[...] = x + y


def _sc_mixer_block(x, gain, w_in, conv_w, w_out):
    b, l, d = x.shape
    tm = MIX_ROW_TILE
    kern = functools.partial(_sc_mixer_kernel, tm=tm, d=d)
    return pl.pallas_call(
        kern,
        grid=(b, l // tm),
        in_specs=_halo_specs(tm, d, l) + [
            pl.BlockSpec((1, d), lambda bb, i: (0, 0)),
            pl.BlockSpec((d, 3 * d), lambda bb, i: (0, 0), **_RESIDENT),
            pl.BlockSpec((3, d), lambda bb, i: (0, 0)),
            pl.BlockSpec((d, d), lambda bb, i: (0, 0), **_RESIDENT),
        ],
        out_specs=pl.BlockSpec((None, tm, d), lambda bb, i: (bb, i, 0)),
        out_shape=jax.ShapeDtypeStruct((b, l, d), F32),
        scratch_shapes=[pltpu.VMEM((tm + 2 * HALO, d), F32)],
        compiler_params=_mixer_params(),
        name="sc_mixer",
    )(x, x, x, gain.reshape(1, d), w_in.astype(BF16), conv_w, w_out.astype(BF16))


def _hy_in_kernel(xp_ref, x_ref, xn_ref, gain_ref, win_ref, bin_ref, cw_ref, cb_ref, x0_ref, vxt_ref, s_ref,
                  *, tm, d):
    gain = gain_ref[...]
    h = _rms_rows(x_ref[...], gain).astype(BF16)
    hh = _rms_rows(jnp.concatenate([xp_ref[...], xn_ref[...]], axis=0), gain).astype(BF16)
    parts = []
    for j in range(3):
        cols = slice(j * d, (j + 1) * d)
        bias = bin_ref[:, cols]
        main = jnp.dot(h, win_ref[:, cols], preferred_element_type=F32) + bias
        halo = jnp.dot(hh, win_ref[:, cols], preferred_element_type=F32) + bias
        _store_with_halo(s_ref, main, halo, tm)
        parts.append(_conv3_from(s_ref, cw_ref[:, cols], tm) + cb_ref[:, cols])
    x0_ref[...] = parts[0]
    vxt_ref[...] = (parts[2] * parts[1]).T


def _hy_in_block(x, gain, w_in, b_in, conv_w, conv_b):
    b, l, d = x.shape
    tm = MIX_ROW_TILE
    kern = functools.partial(_hy_in_kernel, tm=tm, d=d)
    return pl.pallas_call(
        kern,
        grid=(b, l // tm),
        in_specs=_halo_specs(tm, d, l) + [
            pl.BlockSpec((1, d), lambda bb, i: (0, 0)),
            pl.BlockSpec((d, 3 * d), lambda bb, i: (0, 0), **_RESIDENT),
            pl.BlockSpec((1, 3 * d), lambda bb, i: (0, 0)),
            pl.BlockSpec((3, 3 * d), lambda bb, i: (0, 0)),
            pl.BlockSpec((1, 3 * d), lambda bb, i: (0, 0)),
        ],
        out_specs=[pl.BlockSpec((None, tm, d), lambda bb, i: (bb, i, 0)),
                   pl.BlockSpec((None, d, tm), lambda bb, i: (bb, 0, i))],
        out_shape=[jax.ShapeDtypeStruct((b, l, d), F32), jax.ShapeDtypeStruct((b, d, l), F32)],
        scratch_shapes=[pltpu.VMEM((tm + 2 * HALO, d), F32)],
        compiler_params=_mixer_params(),
        name="hyena_in",
    )(x, x, x, gain.reshape(1, d), w_in.astype(BF16), b_in.reshape(1, 3 * d), conv_w, conv_b.reshape(1, 3 * d))


HY_FEAT_PAD = 64
HY_FILTER_TILE = 2048


def _hy_filter_kernel(zt_ref, w1_ref, b1_ref, w2_ref, b2_ref, freq_ref, w3_ref, delta_ref, hc_ref):
    def t_dot(w, a):
        return lax.dot_general(w, a, (((0,), (0,)), ((), ())), precision=lax.Precision.HIGHEST,
                               preferred_element_type=F32)

    z = zt_ref[...]
    freq = freq_ref[...]
    h1 = jnp.sin(freq * (t_dot(w1_ref[...], z) + b1_ref[...]))
    h2 = jnp.sin(freq * (t_dot(w2_ref[...], h1) + b2_ref[...]))
    h = t_dot(w3_ref[...], h2)
    t_row = z[0:1, :]
    mask_row = z[HY_EMB:HY_EMB + 1, :]
    hc_ref[...] = h * jnp.exp(-t_row * delta_ref[...]) * mask_row


def _hy_filter_features(l):
    p = jnp.arange(2 * l)
    pos = jnp.where(p < l, p, 2 * l - p)
    valid = (p != l).astype(F32)
    pos = jnp.minimum(pos, l - 1).astype(F32)[None, :]
    t = pos / (l - 1)
    w = 2.0 * math.pi * pos / l
    f = jnp.linspace(1e-4, HY_BANDS - 1, HY_BANDS, dtype=F32)[:, None]
    z = jnp.concatenate([t, jnp.cos(f * w), -jnp.sin(f * w), valid[None, :]], axis=0)
    return jnp.pad(z, ((0, HY_FEAT_PAD - z.shape[0]), (0, 0)))


def _hy_filter_block(l, w1, b1, w2, b2, w3, freq):
    order = w1.shape[1]
    d = w3.shape[1] // 2
    tl = min(HY_FILTER_TILE, l)
    half = l // tl
    zt = _hy_filter_features(l)
    w1p = jnp.pad(w1, ((0, HY_FEAT_PAD - w1.shape[0]), (0, 0)))
    delta = jnp.abs(jnp.linspace(HY_MIN_DECAY, HY_MAX_DECAY, d, dtype=F32)).reshape(d, 1)
    col = lambda v: v.reshape(order, 1)
    full = lambda shape: pl.BlockSpec(shape, lambda i: (0,) * len(shape))
    return pl.pallas_call(
        _hy_filter_kernel,
        grid=(2 * half,),
        in_specs=[pl.BlockSpec((HY_FEAT_PAD, tl), lambda i: (0, i)),
                  full((HY_FEAT_PAD, order)), full((order, 1)), full((order, order)), full((order, 1)),
                  full((order, 1)),
                  pl.BlockSpec((order, d), lambda i: (0, i // half)),
                  full((d, 1))],
        out_specs=pl.BlockSpec((d, tl), lambda i: (0, i)),
        out_shape=jax.ShapeDtypeStruct((d, 2 * l), F32),
        compiler_params=pltpu.CompilerParams(dimension_semantics=("arbitrary",),
                                             vmem_limit_bytes=V7X_VMEM_LIMIT_BYTES),
        name="hyena_filter",
    )(zt, w1p, col(b1), w2, col(b2), col(freq), w3, delta)


HY_CONV_CH_TILE = 16


def _dft_tables(n1, n2):
    import numpy as np
    n = n1 * n2
    i1, i2 = np.arange(n1), np.arange(n2)
    a1 = 2.0 * np.pi * np.outer(i1, i1) / n1
    a2 = 2.0 * np.pi * np.outer(i2, i2) / n2
    c1, s1, c2, s2 = np.cos(a1), np.sin(a1), np.cos(a2), np.sin(a2)
    tw = 2.0 * np.pi * np.outer(i2, i1) / n
    f_real = np.concatenate([c1, -s1], axis=1)
    f_cplx = np.block([[c2, -s2], [s2, c2]])
    i_cplx = np.block([[c2, s2], [-s2, c2]])
    i_real = np.concatenate([c1, -s1], axis=0) / n
    as_bf = lambda a: jnp.asarray(a, dtype=F32).astype(BF16)
    as_f = lambda a: jnp.asarray(a, dtype=F32)
    return (as_bf(f_real), as_bf(f_cplx), as_bf(i_cplx), as_bf(i_real),
            as_f(np.cos(tw)), as_f(np.sin(tw)), as_f(np.cos(tw.T)), as_f(np.sin(tw.T)))


def _swap(a):
    return jnp.swapaxes(a, 1, 2)


def _hy_conv_kernel(ut_ref, hc_ref, dbias_ref, freal_ref, fcplx_ref, icplx_ref, ireal_ref, twc_ref, tws_ref,
                    twct_ref, twst_ref, yt_ref, hspec_ref, *, ct, n1, n2):
    rows = ct * n2
    twc = twc_ref[...]
    tws = tws_ref[...]

    def mm(a, m_ref):
        return jnp.dot(a.astype(BF16), m_ref[...], preferred_element_type=F32)

    def forward(x):
        p = mm(_swap(x).reshape(rows, n1), freal_ref)
        pr = p[:, :n1].reshape(ct, n2, n1)
        pi = p[:, n1:].reshape(ct, n2, n1)
        qr = pr * twc + pi * tws
        qi = pi * twc - pr * tws
        q = jnp.concatenate([_swap(qr).reshape(ct * n1, n2), _swap(qi).reshape(ct * n1, n2)], axis=1)
        xs = mm(q, fcplx_ref)
        return xs[:, :n2].reshape(ct, n1, n2), xs[:, n2:].reshape(ct, n1, n2)

    @pl.when(pl.program_id(1) == 0)
    def _():
        hr, hi = forward(hc_ref[...])
        hspec_ref[0] = hr
        hspec_ref[1] = hi

    u = ut_ref[...]
    xr, xi = forward(jnp.concatenate([u, jnp.zeros_like(u)], axis=1))
    hr = hspec_ref[0]
    hi = hspec_ref[1]
    yr = xr * hr - xi * hi
    yi = xr * hi + xi * hr
    r = mm(jnp.concatenate([yr.reshape(ct * n1, n2), yi.reshape(ct * n1, n2)], axis=1), icplx_ref)
    rr = r[:, :n2].reshape(ct, n1, n2)
    ri = r[:, n2:].reshape(ct, n1, n2)
    twct = twct_ref[...]
    twst = twst_ref[...]
    sr = rr * twct - ri * twst
    si = ri * twct + rr * twst
    s = jnp.concatenate([_swap(sr).reshape(rows, n1), _swap(si).reshape(rows, n1)], axis=1)
    z = mm(s, ireal_ref).reshape(ct, n2, n1)
    y = _swap(z)[:, :n1 // 2, :]
    yt_ref[...] = y + u * dbias_ref[...]


def _hy_conv_block(ut, hc, d_bias):
    b, d, l = ut.shape
    n2 = 128
    n1 = 2 * l // n2
    ct = HY_CONV_CH_TILE
    tables = _dft_tables(n1, n2)
    kern = functools.partial(_hy_conv_kernel, ct=ct, n1=n1, n2=n2)
    full = lambda a: pl.BlockSpec(a.shape, lambda c, bb: (0,) * a.ndim)
    yt = pl.pallas_call(
        kern,
        grid=(d // ct, b),
        in_specs=[pl.BlockSpec((None, ct, n1 // 2, n2), lambda c, bb: (bb, c, 0, 0)),
                  pl.BlockSpec((ct, n1, n2), lambda c, bb: (c, 0, 0)),
                  pl.BlockSpec((ct, 1, 1), lambda c, bb: (c, 0, 0))] + [full(t) for t in tables],
        out_specs=pl.BlockSpec((None, ct, n1 // 2, n2), lambda c, bb: (bb, c, 0, 0)),
        out_shape=jax.ShapeDtypeStruct((b, d, n1 // 2, n2), F32),
        scratch_shapes=[pltpu.VMEM((2, ct, n1, n2), F32)],
        compiler_params=_mixer_params(),
        name="hyena_longconv",
    )(ut.reshape(b, d, n1 // 2, n2), hc.reshape(d, n1, n2), d_bias.reshape(d, 1, 1), *tables)
    return yt.reshape(b, d, l)


def _hyena_block(x, gain, w_in, b_in, conv_w, conv_b, f_w1, f_b1, f_w2, f_b2, f_w3, f_freq, d_bias, w_out, b_out):
    l = x.shape[1]
    x0, ut = _hy_in_block(x, gain, w_in, b_in, conv_w, conv_b)
    hc = _hy_filter_block(l, f_w1, f_b1, f_w2, f_b2, f_w3, f_freq)
    yt = _hy_conv_block(ut, hc, d_bias)
    return _hy_out_block(x, x0, yt, w_out, b_out)


def _hy_out_kernel(x_ref, x0_ref, yt_ref, wout_ref, bout_ref, o_ref):
    y = yt_ref[...].T
    z = jnp.dot((y * x0_ref[...]).astype(BF16), wout_ref[...], preferred_element_type=F32)
    o_ref[...] = x_ref[...] + z + bout_ref[...]


def _hy_out_block(x, x0, yt, w_out, b_out):
    b, l, d = x.shape
    tm = MIX_ROW_TILE
    row = pl.BlockSpec((None, tm, d), lambda bb, i: (bb, i, 0))
    return pl.pallas_call(
        _hy_out_kernel,
        grid=(b, l // tm),
        in_specs=[row, row, pl.BlockSpec((None, d, tm), lambda bb, i: (bb, 0, i)),
                  pl.BlockSpec((d, d), lambda bb, i: (0, 0), **_RESIDENT),
                  pl.BlockSpec((1, d), lambda bb, i: (0, 0))],
        out_specs=row,
        out_shape=jax.ShapeDtypeStruct((b, l, d), F32),
        compiler_params=_mixer_params(),
        name="hyena_out",
    )(x, x0, yt, w_out.astype(BF16), b_out.reshape(1, d))


LANES = 128


def _softplus(x):
    return jnp.maximum(x, 0.0) + jnp.log1p(jnp.exp(-jnp.abs(x)))


def _gdn_in_kernel(xp_ref, x_ref, xn_ref, gain_ref, win_ref, cw_ref, alog_ref, dtb_ref, cum_ref, eye_ref,
                   q_ref, k_ref, v_ref, z_ref, kt_ref, cols_ref, rows_ref, pairs_ref, s_ref, *, tm, d, heads):
    hi = lax.Precision.HIGHEST
    gain = gain_ref[...]
    h = _rms_rows(x_ref[...], gain).astype(BF16)
    hh = _rms_rows(jnp.concatenate([xp_ref[...], xn_ref[...]], axis=0), gain).astype(BF16)
    dk = d // heads
    for j, (out_ref, scale) in enumerate(((q_ref, dk ** -0.5), (k_ref, 1.0), (v_ref, None))):
        cols = slice(j * d, (j + 1) * d)
        w_cols = win_ref[:, cols].astype(BF16)
        main = jnp.dot(h, w_cols, preferred_element_type=F32)
        halo = jnp.dot(hh, w_cols, preferred_element_type=F32)
        _store_with_halo(s_ref, main, halo, tm)
        c = _conv3_from(s_ref, cw_ref[:, cols], tm)
        c = c * jax.nn.sigmoid(c)
        if scale is None:
            out_ref[...] = c
        else:
            for hd in range(heads):
                ch = c[:, hd * dk:(hd + 1) * dk]
                ss = jnp.sum(ch * ch, axis=-1, keepdims=True)
                out_ref[:, hd * dk:(hd + 1) * dk] = ch * (lax.rsqrt(ss + 1e-6) * scale)
            if j == 1:
                kt = out_ref[...].T
                n_chunk, _, c = kt_ref.shape
                for ci in range(n_chunk):
                    kt_ref[ci] = kt[:, ci * c:(ci + 1) * c]
    z_ref[...] = jnp.dot(h, win_ref[:, 3 * d:4 * d].astype(BF16), preferred_element_type=F32)
    ab = jnp.dot(h, win_ref[:, 4 * d:].astype(BF16), preferred_element_type=F32)
    nh2 = 2 * heads
    g = -jnp.exp(alog_ref[...]) * _softplus(ab[:, :nh2] + dtb_ref[...])
    beta = jax.nn.sigmoid(ab[:, nh2:2 * nh2])
    g_terms = _bf16_terms(g)
    cumsum = lambda mask: sum(jnp.dot(mask, term, preferred_element_type=F32) for term in g_terms)
    gc_f = cumsum(cum_ref[0])
    gc_b = cumsum(cum_ref[1])
    chunk = tm // rows_ref.shape[1]
    for dr, gc in enumerate((gc_f, gc_b)):
        own = slice(dr * heads, (dr + 1) * heads)
        cols = jnp.concatenate([gc[:, own], beta[:, own]], axis=1)
        cols_ref[dr] = cols
        rows = lax.dot_general(eye_ref[...], cols, (((1,), (1,)), ((), ())), precision=hi,
                               preferred_element_type=F32)
        for ci in range(tm // chunk):
            chunk_rows = rows[:, ci * chunk:(ci + 1) * chunk]
            rows_ref[dr, ci] = chunk_rows
            pairs_ref[dr, ci] = jnp.concatenate(
                [jnp.concatenate([chunk_rows[r:r + 1], chunk_rows[r + 1:r + 2]], axis=1)
                 for r in range(0, nh2, 2)], axis=0)


def _chunk_cumsum_masks(tm, chunk):
    import numpy as np
    r = np.arange(tm)
    same = (r[:, None] // chunk) == (r[None, :] // chunk)
    lower = same & (r[None, :] <= r[:, None])
    upper = same & (r[None, :] >= r[:, None])
    return jnp.asarray(np.stack([lower, upper]).astype(np.float32)).astype(BF16)


def _gdn_in_block(x, gain, w_in, lead, conv_w, a_log, dt_bias):
    b, l, d = x.shape
    heads = GD_HEADS
    nh2 = 2 * heads
    tm = GD_ROW_TILE
    kern = functools.partial(_gdn_in_kernel, tm=tm, d=d, heads=heads)
    row = pl.BlockSpec((None, tm, d), lambda bb, i: (bb, i, 0))
    wide = jax.ShapeDtypeStruct((b, l, d), F32)
    c = GD_CHUNK
    return pl.pallas_call(
        kern,
        grid=(b, l // tm),
        in_specs=_halo_specs(tm, d, l) + [
            pl.BlockSpec((1, d), lambda bb, i: (0, 0)),
            _stacked_weight_spec(w_in, lead),
            pl.BlockSpec((3, 3 * d), lambda bb, i: (0, 0)),
            pl.BlockSpec((1, nh2), lambda bb, i: (0, 0)),
            pl.BlockSpec((1, nh2), lambda bb, i: (0, 0)),
            pl.BlockSpec((2, tm, tm), lambda bb, i: (0, 0, 0), **_RESIDENT),
            pl.BlockSpec((nh2, nh2), lambda bb, i: (0, 0)),
        ],
        out_specs=[row, row, row, row,
                   pl.BlockSpec((None, tm // c, d, c), lambda bb, i: (bb, i, 0, 0)),
                   pl.BlockSpec((2, None, tm, nh2), lambda bb, i: (0, bb, i, 0)),
                   pl.BlockSpec((2, None, tm // c, nh2, c), lambda bb, i: (0, bb, i, 0, 0)),
                   pl.BlockSpec((2, None, tm // c, heads, 2 * c), lambda bb, i: (0, bb, i, 0, 0))],
        out_shape=[wide, wide, wide, wide, jax.ShapeDtypeStruct((b, l // c, d, c), F32),
                   jax.ShapeDtypeStruct((2, b, l, nh2), F32),
                   jax.ShapeDtypeStruct((2, b, l // c, nh2, c), F32),
                   jax.ShapeDtypeStruct((2, b, l // c, heads, 2 * c), F32)],
        scratch_shapes=[pltpu.VMEM((tm + 2 * HALO, d), F32)],
        compiler_params=_mixer_params(),
        name="gdn_in",
    )(x, x, x, gain.reshape(1, d), w_in, conv_w, a_log.reshape(1, nh2), dt_bias.reshape(1, nh2),
      _chunk_cumsum_masks(tm, GD_CHUNK), jnp.eye(nh2, dtype=F32))


GD_GROUP = 8


def _decay(gc_col, gc_row, after_or_same):
    return jnp.exp(jnp.where(after_or_same, gc_col - gc_row, -jnp.inf))


def _spread_exact(cols, sel):
    hi, mid, lo = (jnp.dot(term, sel, preferred_element_type=F32) for term in _bf16_terms(cols))
    return (hi + mid) + lo


def _head_pair_block_diag(left, right):
    zero = jnp.zeros_like(left)
    return jnp.concatenate([jnp.concatenate([left, zero], axis=1),
                            jnp.concatenate([zero, right], axis=1)], axis=0)


def _gdn_prep_kernel(k_ref, cols_ref, pairs_ref, selg_ref, selb_ref, a_ref, *, heads, dk, c, group):
    direction = pl.program_id(0)
    row = lax.broadcasted_iota(jnp.int32, (c, 2 * c), 0)
    col = lax.broadcasted_iota(jnp.int32, (c, 2 * c), 1) % c
    delta = (row - col) * (1 - 2 * direction)
    cols = cols_ref[...]
    g_pairs = _spread_exact(cols, selg_ref[...])
    b_pairs = _spread_exact(cols, selb_ref[...])
    n_pair = heads // 2
    for ci in range(group):
        rows = slice(ci * c, (ci + 1) * c)
        for m in range(n_pair):
            lanes = slice(2 * c * m, 2 * c * (m + 1))
            kp = k_ref[rows, 2 * dk * m:2 * dk * (m + 1)].astype(BF16)
            gram = lax.dot_general(kp, _head_pair_block_diag(kp[:, :dk], kp[:, dk:]),
                                   (((1,), (1,)), ((), ())), preferred_element_type=F32)
            dec = jnp.exp(jnp.where(delta > 0, g_pairs[rows, lanes] - pairs_ref[ci, m:m + 1, :], -jnp.inf))
            a_ref[:, ci * n_pair + m, :] = b_pairs[rows, lanes] * gram * dec


def _pair_selectors(heads, c):
    eye = jnp.eye(2 * heads, heads, dtype=BF16)
    return jnp.repeat(eye, c, axis=1), jnp.repeat(jnp.roll(eye, heads, axis=0), c, axis=1)


def _gdn_prep_block(k, cols, pairs):
    b, l, d = k.shape
    heads, c, group = GD_HEADS, GD_CHUNK, GD_GROUP
    rows = c * group
    ng = l // rows
    kern = functools.partial(_gdn_prep_kernel, heads=heads, dk=d // heads, c=c, group=group)
    per = group * heads // 2
    full = lambda a: pl.BlockSpec(a.shape, lambda dr, bb, g: (0,) * a.ndim)
    selg, selb = _pair_selectors(heads, c)
    return pl.pallas_call(
        kern,
        grid=(2, b, ng),
        in_specs=[pl.BlockSpec((None, rows, d), lambda dr, bb, g: (bb, g, 0)),
                  pl.BlockSpec((None, None, rows, 2 * heads), lambda dr, bb, g: (dr, bb, g, 0)),
                  pl.BlockSpec((None, None, group, heads, 2 * c), lambda dr, bb, g: (dr, bb, g, 0, 0)),
                  full(selg), full(selb)],
        out_specs=pl.BlockSpec((None, c, per, 2 * c), lambda dr, bb, g: (dr, 0, bb * ng + g, 0)),
        out_shape=jax.ShapeDtypeStruct((2, c, b * ng * per, 2 * c), F32),
        compiler_params=pltpu.CompilerParams(dimension_semantics=("arbitrary",) * 3,
                                             vmem_limit_bytes=V7X_VMEM_LIMIT_BYTES),
        name="gdn_prep",
    )(k, cols, pairs, selg, selb)


SUBLANES = 8


def _gdn_tsolve_kernel(a_ref, t_ref, a3_ref, t3_ref, *, c, upper):
    p = a_ref.shape[1]
    for i in range(c):
        slab = a_ref[i].T
        a3_ref[0, i] = slab[:c]
        a3_ref[1, i] = slab[c:]
    t3_ref[...] = jnp.zeros_like(t3_ref)
    ntile = c // SUBLANES
    sub = lax.broadcasted_iota(jnp.int32, (SUBLANES, p), 0)
    order = range(c - 1, -1, -1) if upper else range(c)
    for i in order:
        solved = range(i + 1, c) if upper else range(i)
        lo, hi = (i // SUBLANES, ntile) if upper else (0, i // SUBLANES + 1)
        ti = i // SUBLANES
        for half in range(2):
            acc = [jnp.zeros((SUBLANES, p), F32) for _ in range(lo, hi)]
            for j in solved:
                jlo, jhi = (j // SUBLANES, ntile) if upper else (0, j // SUBLANES + 1)
                coef = a3_ref[half, i, j:j + 1, :]
                tj = t3_ref[half, j, jlo * SUBLANES:jhi * SUBLANES, :]
                for t in range(jlo, jhi):
                    acc[t - lo] = acc[t - lo] - coef * tj[(t - jlo) * SUBLANES:(t - jlo + 1) * SUBLANES]
            acc[ti - lo] = acc[ti - lo] + jnp.where(sub == i % SUBLANES, 1.0, 0.0)
            t3_ref[half, i, lo * SUBLANES:hi * SUBLANES, :] = jnp.concatenate(acc, axis=0)
    for ib in range(0, c, SUBLANES):
        slabs = jnp.stack([jnp.concatenate([t3_ref[0, i], t3_ref[1, i]], axis=0).T
                           for i in range(ib, ib + SUBLANES)], axis=0)
        t_ref[:, ib:ib + SUBLANES, :] = jnp.swapaxes(slabs, 0, 1)


def _gdn_tsolve_block(a, direction):
    _, c, pairs, _ = a.shape
    blk = LANES
    upper = bool(direction)
    kern = functools.partial(_gdn_tsolve_kernel, c=c, upper=upper)
    return pl.pallas_call(
        kern,
        grid=(pairs // blk,),
        in_specs=[pl.BlockSpec((None, c, blk, 2 * c), lambda i: (direction, 0, i, 0))],
        out_specs=pl.BlockSpec((blk, c, 2 * c), lambda i: (i, 0, 0)),
        out_shape=jax.ShapeDtypeStruct((pairs, c, 2 * c), F32),
        scratch_shapes=[pltpu.VMEM((2, c, c, blk), F32), pltpu.VMEM((2, c, c, blk), F32)],
        compiler_params=pltpu.CompilerParams(dimension_semantics=("arbitrary",),
                                             vmem_limit_bytes=V7X_VMEM_LIMIT_BYTES),
        name="gdn_tsolve_bwd" if upper else "gdn_tsolve_fwd",
    )(a)


def _gdn_scan_kernel(q_ref, k_ref, v_ref, kt_ref, cols_ref, rows_ref, pairs_ref, t_ref, sel_ref, *rest,
                     heads, dk, c, group, reverse, add_other):
    other_ref = rest[0] if add_other else None
    o_ref, s_ref, g_ref, qe_ref, rhs_ref, attn_ref, u_ref, w_ref = rest[int(add_other):]

    @pl.when(pl.program_id(1) == 0)
    def _():
        s_ref[...] = jnp.zeros_like(s_ref)

    g_all = _spread_exact(cols_ref[...], sel_ref[...])
    g_ref[...] = g_all
    eg = jnp.exp(g_all)
    qe_ref[...] = (q_ref[...] * eg).astype(BF16)
    ke = k_ref[...] * eg
    for hd in range(heads):
        lanes = slice(hd * dk, (hd + 1) * dk)
        rhs_ref[:, 2 * hd * dk:(2 * hd + 1) * dk] = v_ref[:, lanes].astype(BF16)
        rhs_ref[:, (2 * hd + 1) * dk:(2 * hd + 2) * dk] = ke[:, lanes].astype(BF16)

    row = lax.broadcasted_iota(jnp.int32, (c, c), 0)
    col = lax.broadcasted_iota(jnp.int32, (c, c), 1)
    incl = (row <= col) if reverse else (row >= col)
    head_lanes = [slice(hd * dk, (hd + 1) * dk) for hd in range(heads)]
    n_pair = heads // 2

    for ci in range(group):
        rows = slice(ci * c, (ci + 1) * c)
        for hd, lanes in enumerate(head_lanes):
            dec = _decay(g_ref[rows, hd * dk:hd * dk + c], rows_ref[ci, hd:hd + 1, :], incl)
            gram = lax.dot_general(q_ref[rows, lanes].astype(BF16), k_ref[rows, lanes].astype(BF16),
                                   (((1,), (1,)), ((), ())), preferred_element_type=F32)
            attn_ref[ci * heads + hd] = (gram * dec).astype(BF16)
        for m in range(n_pair):
            width = 2 * dk
            t_pair = t_ref[ci * n_pair + m] * pairs_ref[ci, n_pair + m:n_pair + m + 1, :]
            rhs_pair = _head_pair_block_diag(rhs_ref[rows, 2 * m * width:(2 * m + 1) * width],
                                             rhs_ref[rows, (2 * m + 1) * width:(2 * m + 2) * width])
            sol = jnp.dot(t_pair.astype(BF16), rhs_pair, preferred_element_type=F32)
            for par in range(2):
                lanes = head_lanes[2 * m + par]
                u_ref[rows, lanes] = sol[:, par * width:par * width + dk]
                w_ref[rows, lanes] = sol[:, par * width + dk:(par + 1) * width].astype(BF16)

    for step in range(group):
        ci = group - 1 - step if reverse else step
        rows = slice(ci * c, (ci + 1) * c)
        last = ci * c if reverse else (ci + 1) * c - 1
        states = [s_ref[hd] for hd in range(heads)]
        states_bf = [s.astype(BF16) for s in states]
        w_s = [jnp.dot(w_ref[rows, lanes], s, preferred_element_type=F32)
               for lanes, s in zip(head_lanes, states_bf)]
        q_s = [jnp.dot(qe_ref[rows, lanes], s, preferred_element_type=F32)
               for lanes, s in zip(head_lanes, states_bf)]
        for hd, lanes in enumerate(head_lanes):
            v_new_bf = (u_ref[rows, lanes] - w_s[hd]).astype(BF16)
            o = q_s[hd] + jnp.dot(attn_ref[ci * heads + hd], v_new_bf, preferred_element_type=F32)
            o_ref[rows, lanes] = o + other_ref[rows, lanes] if add_other else o
            g_tot = g_ref[last:last + 1, lanes]
            k_dec_t = (kt_ref[ci, lanes, :] * jnp.exp(g_tot[:, :c] - rows_ref[ci, hd:hd + 1, :])).astype(BF16)
            s_ref[hd] = states[hd] * jnp.exp(g_tot) + jnp.dot(k_dec_t, v_new_bf, preferred_element_type=F32)


def _gdn_scan_block(q, k, v, kt, cols, rows_t, pairs, t, reverse, other=None):
    b, l, d = q.shape
    heads, c, group = GD_HEADS, GD_CHUNK, GD_GROUP
    dk = d // heads
    rows = c * group
    ng = l // rows
    per = group * heads
    dr = int(reverse)
    add_other = other is not None
    kern = functools.partial(_gdn_scan_kernel, heads=heads, dk=dk, c=c, group=group, reverse=reverse,
                             add_other=add_other)
    visit = (lambda g: ng - 1 - g) if reverse else (lambda g: g)
    wide = pl.BlockSpec((None, rows, d), lambda bb, g: (bb, visit(g), 0))
    sel = jnp.repeat(jnp.eye(2 * heads, heads, dtype=BF16), dk, axis=1)
    return pl.pallas_call(
        kern,
        grid=(b, ng),
        in_specs=[wide, wide, wide,
                  pl.BlockSpec((None, group, d, c), lambda bb, g: (bb, visit(g), 0, 0)),
                  pl.BlockSpec((None, None, rows, 2 * heads), lambda bb, g: (dr, bb, visit(g), 0)),
                  pl.BlockSpec((None, None, group, 2 * heads, c), lambda bb, g: (dr, bb, visit(g), 0, 0)),
                  pl.BlockSpec((None, None, group, heads, 2 * c), lambda bb, g: (dr, bb, visit(g), 0, 0)),
                  pl.BlockSpec((per // 2, c, 2 * c), lambda bb, g: (bb * ng + visit(g), 0, 0)),
                  pl.BlockSpec((2 * heads, d), lambda bb, g: (0, 0))] + [wide] * add_other,
        out_specs=wide,
        out_shape=jax.ShapeDtypeStruct((b, l, d), F32),
        scratch_shapes=[pltpu.VMEM((heads, dk, dk), F32), pltpu.VMEM((rows, d), F32),
                        pltpu.VMEM((rows, d), BF16), pltpu.VMEM((rows, 2 * d), BF16),
                        pltpu.VMEM((per, c, c), BF16),
                        pltpu.VMEM((rows, d), F32), pltpu.VMEM((rows, d), BF16)],
        compiler_params=_mixer_params(),
        name="gdn_scan_bwd" if reverse else "gdn_scan_fwd",
    )(q, k, v, kt, cols, rows_t, pairs, t, sel, *([other] if add_other else []))


def _gdn_out_kernel(x_ref, o_ref, z_ref, ng_ref, wout_ref, y_ref, g_ref, *, heads, dk):
    o = o_ref[...]
    z = z_ref[...]
    for hd in range(heads):
        lanes = slice(hd * dk, (hd + 1) * dk)
        oh = o[:, lanes]
        ms = jnp.mean(oh * oh, axis=-1, keepdims=True)
        zh = z[:, lanes]
        g_ref[:, lanes] = (oh * lax.rsqrt(ms + NORM_EPS) * ng_ref[...] * (zh * jax.nn.sigmoid(zh))).astype(BF16)
    y_ref[...] = x_ref[...] + jnp.dot(g_ref[...], wout_ref[...], preferred_element_type=F32)


def _gdn_out_block(x, o, z, norm_g, w_out):
    b, l, d = x.shape
    heads = GD_HEADS
    dk = d // heads
    tm = GD_ROW_TILE
    row = pl.BlockSpec((None, tm, d), lambda bb, i: (bb, i, 0))
    kern = functools.partial(_gdn_out_kernel, heads=heads, dk=dk)
    return pl.pallas_call(
        kern,
        grid=(b, l // tm),
        in_specs=[row, row, row,
                  pl.BlockSpec((1, dk), lambda bb, i: (0, 0)),
                  pl.BlockSpec((d, d), lambda bb, i: (0, 0), **_RESIDENT)],
        out_specs=row,
        out_shape=jax.ShapeDtypeStruct((b, l, d), F32),
        scratch_shapes=[pltpu.VMEM((tm, d), BF16)],
        compiler_params=_mixer_params(),
        name="gdn_out",
    )(x, o, z, norm_g.reshape(1, dk), w_out.astype(BF16))


def _gdn_block(x, gain, w_in, lead, conv_w, a_log, dt_bias, norm_g, w_out):
    q, k, v, z, kt, cols, rows_t, pairs = _gdn_in_block(x, gain, w_in, lead, conv_w, a_log, dt_bias)
    a = _gdn_prep_block(k, cols, pairs)
    o = None
    for dr in range(2):
        o = _gdn_scan_block(q, k, v, kt, cols, rows_t, pairs, _gdn_tsolve_block(a, dr), bool(dr), other=o)
    return _gdn_out_block(x, o, z, norm_g, w_out)


def kernel(x, norms, final_norm, ffn_w_in, ffn_w_out, sc_w_in, sc_conv, sc_w_out, hy_w_in, hy_b_in, hy_conv, hy_conv_b, hy_f_w1, hy_f_b1, hy_f_w2, hy_f_b2, hy_f_w3, hy_f_freq, hy_d, hy_w_out, hy_b_out, gd_w_in, gd_conv, gd_a_log, gd_dt_bias, gd_norm, gd_w_out):
    b, l, d = x.shape
    depth = norms.shape[0]

    def ffn(xx, i, k):
        final = final_norm if (i == depth - 1 and k == 1) else None
        return _ffn_block(xx.reshape(b * l, d), norms[i, 2 * k], ffn_w_in, ffn_w_out, (i, k),
                          final).reshape(b, l, d)

    for i in range(depth):
        m, j = i % N_MIXERS, i // N_MIXERS
        x = ffn(x, i, 0)
        if m == 0:
            x = _sc_mixer_block(x, norms[i, 1], sc_w_in[j], sc_conv[j], sc_w_out[j])
        elif m == 1:
            x = _hyena_block(x, norms[i, 1], hy_w_in[j], hy_b_in[j], hy_conv[j], hy_conv_b[j],
                             hy_f_w1[j], hy_f_b1[j], hy_f_w2[j], hy_f_b2[j], hy_f_w3[j],
                             hy_f_freq[j], hy_d[j], hy_w_out[j], hy_b_out[j])
        else:
            x = _gdn_block(x, norms[i, 1], gd_w_in, (j,), gd_conv[j], gd_a_log[j], gd_dt_bias[j],
                           gd_norm[j], gd_w_out[j])
        x = ffn(x, i, 1)
    return x
```

```python
import functools
import math

import jax
import jax.numpy as jnp
from jax import lax
from jax.experimental import pallas as pl
from jax.experimental.pallas import tpu as pltpu

F32 = jnp.float32
BF16 = jnp.bfloat16

NORM_EPS = 1e-6
N_MIXERS = 3
HY_EMB = 33
HY_BANDS = (HY_EMB - 1) // 2
HY_MAX_DECAY = math.log(1e-2) / 0.3
HY_MIN_DECAY = math.log(1e-2) / 1.5
GD_HEADS = 8
GD_CHUNK = 64

V7X_VMEM_LIMIT_BYTES = 56 * 1024 * 1024
FFN_ROW_TILE = 512
FFN_HIDDEN_TILE = 256


def _ffn_kernel(x_ref, gain_ref, win_ref, wout_ref, *rest, n_hidden_tiles, hidden_tile, final_norm):
    final_ref, o_ref, hid_ref = rest if final_norm else (None,) + rest
    x = x_ref[...]
    xn = _rms_rows(x, gain_ref[...]).astype(BF16)
    f = n_hidden_tiles * hidden_tile
    for j in range(n_hidden_tiles):
        cols = slice(j * hidden_tile, (j + 1) * hidden_tile)
        up_cols = slice(f + j * hidden_tile, f + (j + 1) * hidden_tile)
        g = jnp.dot(xn, win_ref[:, cols].astype(BF16), preferred_element_type=F32)
        u = jnp.dot(xn, win_ref[:, up_cols].astype(BF16), preferred_element_type=F32)
        hid_ref[:, cols] = (g * jax.nn.sigmoid(g) * u).astype(BF16)
    y = x + 0.5 * jnp.dot(hid_ref[...], wout_ref[...].astype(BF16), preferred_element_type=F32)
    o_ref[...] = _rms_rows(y, final_ref[...]) if final_norm else y


def _stacked_weight_spec(w, lead):
    tail = w.shape[len(lead):]
    index = tuple(lead) + (0,) * len(tail)
    return pl.BlockSpec((None,) * len(lead) + tail, lambda *_: index, **_RESIDENT)


def _ffn_block(x2d, gain, w_in, w_out, lead, final_gain=None):
    t, d = x2d.shape
    f = w_out.shape[-2]
    tm = FFN_ROW_TILE
    final_norm = final_gain is not None
    kern = functools.partial(_ffn_kernel, n_hidden_tiles=f // FFN_HIDDEN_TILE, hidden_tile=FFN_HIDDEN_TILE,
                             final_norm=final_norm)
    vec = pl.BlockSpec((1, d), lambda i: (0, 0))
    extra = ([vec], [final_gain.reshape(1, d)]) if final_norm else ([], [])
    return pl.pallas_call(
        kern,
        grid=(t // tm,),
        in_specs=[
            pl.BlockSpec((tm, d), lambda i: (i, 0)),
            vec,
            _stacked_weight_spec(w_in, lead),
            _stacked_weight_spec(w_out, lead),
        ] + extra[0],
        out_specs=pl.BlockSpec((tm, d), lambda i: (i, 0)),
        out_shape=jax.ShapeDtypeStruct((t, d), F32),
        scratch_shapes=[pltpu.VMEM((tm, f), BF16)],
        compiler_params=pltpu.CompilerParams(
            dimension_semantics=("arbitrary",), vmem_limit_bytes=V7X_VMEM_LIMIT_BYTES),
        name="ffn_final" if final_norm else "ffn_block",
    )(x2d, gain.reshape(1, d), w_in, w_out, *extra[1])


MIX_ROW_TILE = 1024
GD_ROW_TILE = 512
HALO = 8


def _rms_rows(x, gain):
    ms = jnp.mean(x * x, axis=-1, keepdims=True)
    return x * lax.rsqrt(ms + NORM_EPS) * gain


def _bf16_terms(x):
    hi = x.astype(BF16)
    rest = x - hi.astype(F32)
    mid = rest.astype(BF16)
    lo = (rest - mid.astype(F32)).astype(BF16)
    return hi, mid, lo


def _halo_specs(tm, d, seq_len):
    per = tm // HALO
    last_blk = seq_len // HALO - 1
    return [
        pl.BlockSpec((None, HALO, d), lambda b, i: (b, jnp.maximum(i * per - 1, 0), 0)),
        pl.BlockSpec((None, tm, d), lambda b, i: (b, i, 0)),
        pl.BlockSpec((None, HALO, d), lambda b, i: (b, jnp.minimum((i + 1) * per, last_blk), 0)),
    ]


def _store_with_halo(s_ref, main, halo, tm):
    i = pl.program_id(1)
    last = pl.num_programs(1) - 1
    s_ref[0:HALO, :] = jnp.where(i > 0, halo[:HALO], 0.0)
    s_ref[HALO:HALO + tm, :] = main
    s_ref[HALO + tm:2 * HALO + tm, :] = jnp.where(i < last, halo[HALO:], 0.0)


def _conv3_from(s_ref, cw, tm):
    return (cw[0:1] * s_ref[HALO - 1:HALO - 1 + tm, :] + cw[1:2] * s_ref[HALO:HALO + tm, :]
            + cw[2:3] * s_ref[HALO + 1:HALO + 1 + tm, :])


_RESIDENT = dict(pipeline_mode=pl.Buffered(1))


def _mixer_params():
    return pltpu.CompilerParams(dimension_semantics=("arbitrary", "arbitrary"),
                                vmem_limit_bytes=V7X_VMEM_LIMIT_BYTES)


def _sc_mixer_kernel(xp_ref, x_ref, xn_ref, gain_ref, win_ref, cw_ref, wout_ref, o_ref, ch_ref, *, tm, d):
    gain = gain_ref[...]
    x = x_ref[...]
    h = _rms_rows(x, gain).astype(BF16)
    c_main = jnp.dot(h, win_ref[:, d:2 * d], preferred_element_type=F32)
    h_main = jnp.dot(h, win_ref[:, 2 * d:], preferred_element_type=F32)
    hh = _rms_rows(jnp.concatenate([xp_ref[...], xn_ref[...]], axis=0), gain).astype(BF16)
    c_halo = jnp.dot(hh, win_ref[:, d:2 * d], preferred_element_type=F32)
    h_halo = jnp.dot(hh, win_ref[:, 2 * d:], preferred_element_type=F32)
    _store_with_halo(ch_ref, c_main * h_main, c_halo * h_halo, tm)
    conv = _conv3_from(ch_ref, cw_ref[...], tm)
    b_main = jnp.dot(h, win_ref[:, :d], preferred_element_type=F32)
    y = jnp.dot((b_main * conv).astype(BF16), wout_ref[...], preferred_element_type=F32)
    o_ref[...] = x + y


def _sc_mixer_block(x, gain, w_in, conv_w, w_out):
    b, l, d = x.shape
    tm = MIX_ROW_TILE
    kern = functools.partial(_sc_mixer_kernel, tm=tm, d=d)
    return pl.pallas_call(
        kern,
        grid=(b, l // tm),
        in_specs=_halo_specs(tm, d, l) + [
            pl.BlockSpec((1, d), lambda bb, i: (0, 0)),
            pl.BlockSpec((d, 3 * d), lambda bb, i: (0, 0), **_RESIDENT),
            pl.BlockSpec((3, d), lambda bb, i: (0, 0)),
            pl.BlockSpec((d, d), lambda bb, i: (0, 0), **_RESIDENT),
        ],
        out_specs=pl.BlockSpec((None, tm, d), lambda bb, i: (bb, i, 0)),
        out_shape=jax.ShapeDtypeStruct((b, l, d), F32),
        scratch_shapes=[pltpu.VMEM((tm + 2 * HALO, d), F32)],
        compiler_params=_mixer_params(),
        name="sc_mixer",
    )(x, x, x, gain.reshape(1, d), w_in.astype(BF16), conv_w, w_out.astype(BF16))


def _hy_in_kernel(xp_ref, x_ref, xn_ref, gain_ref, win_ref, bin_ref, cw_ref, cb_ref, x0_ref, vxt_ref, s_ref,
                  *, tm, d):
    gain = gain_ref[...]
    h = _rms_rows(x_ref[...], gain).astype(BF16)
    hh = _rms_rows(jnp.concatenate([xp_ref[...], xn_ref[...]], axis=0), gain).astype(BF16)
    parts = []
    for j in range(3):
        cols = slice(j * d, (j + 1) * d)
        bias = bin_ref[:, cols]
        main = jnp.dot(h, win_ref[:, cols], preferred_element_type=F32) + bias
        halo = jnp.dot(hh, win_ref[:, cols], preferred_element_type=F32) + bias
        _store_with_halo(s_ref, main, halo, tm)
        parts.append(_conv3_from(s_ref, cw_ref[:, cols], tm) + cb_ref[:, cols])
    x0_ref[...] = parts[0]
    vxt_ref[...] = (parts[2] * parts[1]).T


def _hy_in_block(x, gain, w_in, b_in, conv_w, conv_b):
    b, l, d = x.shape
    tm = MIX_ROW_TILE
    kern = functools.partial(_hy_in_kernel, tm=tm, d=d)
    return pl.pallas_call(
        kern,
        grid=(b, l // tm),
        in_specs=_halo_specs(tm, d, l) + [
            pl.BlockSpec((1, d), lambda bb, i: (0, 0)),
            pl.BlockSpec((d, 3 * d), lambda bb, i: (0, 0), **_RESIDENT),
            pl.BlockSpec((1, 3 * d), lambda bb, i: (0, 0)),
            pl.BlockSpec((3, 3 * d), lambda bb, i: (0, 0)),
            pl.BlockSpec((1, 3 * d), lambda bb, i: (0, 0)),
        ],
        out_specs=[pl.BlockSpec((None, tm, d), lambda bb, i: (bb, i, 0)),
                   pl.BlockSpec((None, d, tm), lambda bb, i: (bb, 0, i))],
        out_shape=[jax.ShapeDtypeStruct((b, l, d), F32), jax.ShapeDtypeStruct((b, d, l), F32)],
        scratch_shapes=[pltpu.VMEM((tm + 2 * HALO, d), F32)],
        compiler_params=_mixer_params(),
        name="hyena_in",
    )(x, x, x, gain.reshape(1, d), w_in.astype(BF16), b_in.reshape(1, 3 * d), conv_w, conv_b.reshape(1, 3 * d))


HY_FEAT_PAD = 64
HY_FILTER_TILE = 2048


def _hy_filter_kernel(zt_ref, w1_ref, b1_ref, w2_ref, b2_ref, freq_ref, w3_ref, delta_ref, hc_ref):
    def t_dot(w, a):
        return lax.dot_general(w, a, (((0,), (0,)), ((), ())), precision=lax.Precision.HIGHEST,
                               preferred_element_type=F32)

    z = zt_ref[...]
    freq = freq_ref[...]
    h1 = jnp.sin(freq * (t_dot(w1_ref[...], z) + b1_ref[...]))
    h2 = jnp.sin(freq * (t_dot(w2_ref[...], h1) + b2_ref[...]))
    h = t_dot(w3_ref[...], h2)
    t_row = z[0:1, :]
    mask_row = z[HY_EMB:HY_EMB + 1, :]
    hc_ref[...] = h * jnp.exp(-t_row * delta_ref[...]) * mask_row


def _hy_filter_features(l):
    p = jnp.arange(2 * l)
    pos = jnp.where(p < l, p, 2 * l - p)
    valid = (p != l).astype(F32)
    pos = jnp.minimum(pos, l - 1).astype(F32)[None, :]
    t = pos / (l - 1)
    w = 2.0 * math.pi * pos / l
    f = jnp.linspace(1e-4, HY_BANDS - 1, HY_BANDS, dtype=F32)[:, None]
    z = jnp.concatenate([t, jnp.cos(f * w), -jnp.sin(f * w), valid[None, :]], axis=0)
    return jnp.pad(z, ((0, HY_FEAT_PAD - z.shape[0]), (0, 0)))


def _hy_filter_block(l, w1, b1, w2, b2, w3, freq):
    order = w1.shape[1]
    d = w3.shape[1] // 2
    tl = min(HY_FILTER_TILE, l)
    half = l // tl
    zt = _hy_filter_features(l)
    w1p = jnp.pad(w1, ((0, HY_FEAT_PAD - w1.shape[0]), (0, 0)))
    delta = jnp.abs(jnp.linspace(HY_MIN_DECAY, HY_MAX_DECAY, d, dtype=F32)).reshape(d, 1)
    col = lambda v: v.reshape(order, 1)
    full = lambda shape: pl.BlockSpec(shape, lambda i: (0,) * len(shape))
    return pl.pallas_call(
        _hy_filter_kernel,
        grid=(2 * half,),
        in_specs=[pl.BlockSpec((HY_FEAT_PAD, tl), lambda i: (0, i)),
                  full((HY_FEAT_PAD, order)), full((order, 1)), full((order, order)), full((order, 1)),
                  full((order, 1)),
                  pl.BlockSpec((order, d), lambda i: (0, i // half)),
                  full((d, 1))],
        out_specs=pl.BlockSpec((d, tl), lambda i: (0, i)),
        out_shape=jax.ShapeDtypeStruct((d, 2 * l), F32),
        compiler_params=pltpu.CompilerParams(dimension_semantics=("arbitrary",),
                                             vmem_limit_bytes=V7X_VMEM_LIMIT_BYTES),
        name="hyena_filter",
    )(zt, w1p, col(b1), w2, col(b2), col(freq), w3, delta)


HY_CONV_CH_TILE = 32


def _dft_tables(n1, n2):
    import numpy as np
    n = n1 * n2
    i1, i2 = np.arange(n1), np.arange(n2)
    a1 = 2.0 * np.pi * np.outer(i1, i1) / n1
    a2 = 2.0 * np.pi * np.outer(i2, i2) / n2
    c1, s1, c2, s2 = np.cos(a1), np.sin(a1), np.cos(a2), np.sin(a2)
    tw = 2.0 * np.pi * np.outer(i2, i1) / n
    f_real = np.concatenate([c1, -s1], axis=1)
    f_cplx = np.block([[c2, -s2], [s2, c2]])
    i_cplx = np.block([[c2, s2], [-s2, c2]])
    i_real = np.concatenate([c1, -s1], axis=0) / n
    as_bf = lambda a: jnp.asarray(a, dtype=F32).astype(BF16)
    as_f = lambda a: jnp.asarray(a, dtype=F32)
    return (as_bf(f_real), as_bf(f_cplx), as_bf(i_cplx), as_bf(i_real),
            as_f(np.cos(tw)), as_f(np.sin(tw)), as_f(np.cos(tw.T)), as_f(np.sin(tw.T)))


def _swap(a):
    return jnp.swapaxes(a, 1, 2)


def _hy_conv_kernel(ut_ref, hc_ref, dbias_ref, freal_ref, fcplx_ref, icplx_ref, ireal_ref, twc_ref, tws_ref,
                    twct_ref, twst_ref, yt_ref, hspec_ref, *, ct, n1, n2):
    rows = ct * n2
    twc = twc_ref[...]
    tws = tws_ref[...]

    def mm(a, m_ref):
        return jnp.dot(a.astype(BF16), m_ref[...], preferred_element_type=F32)

    def forward(x):
        p = mm(_swap(x).reshape(rows, n1), freal_ref)
        pr = p[:, :n1].reshape(ct, n2, n1)
        pi = p[:, n1:].reshape(ct, n2, n1)
        qr = pr * twc + pi * tws
        qi = pi * twc - pr * tws
        q = jnp.concatenate([_swap(qr).reshape(ct * n1, n2), _swap(qi).reshape(ct * n1, n2)], axis=1)
        xs = mm(q, fcplx_ref)
        return xs[:, :n2].reshape(ct, n1, n2), xs[:, n2:].reshape(ct, n1, n2)

    @pl.when(pl.program_id(1) == 0)
    def _():
        hr, hi = forward(hc_ref[...])
        hspec_ref[0] = hr
        hspec_ref[1] = hi

    u = ut_ref[...]
    xr, xi = forward(jnp.concatenate([u, jnp.zeros_like(u)], axis=1))
    hr = hspec_ref[0]
    hi = hspec_ref[1]
    yr = xr * hr - xi * hi
    yi = xr * hi + xi * hr
    r = mm(jnp.concatenate([yr.reshape(ct * n1, n2), yi.reshape(ct * n1, n2)], axis=1), icplx_ref)
    rr = r[:, :n2].reshape(ct, n1, n2)
    ri = r[:, n2:].reshape(ct, n1, n2)
    twct = twct_ref[...]
    twst = twst_ref[...]
    sr = rr * twct - ri * twst
    si = ri * twct + rr * twst
    s = jnp.concatenate([_swap(sr).reshape(rows, n1), _swap(si).reshape(rows, n1)], axis=1)
    z = mm(s, ireal_ref).reshape(ct, n2, n1)
    y = _swap(z)[:, :n1 // 2, :]
    yt_ref[...] = y + u * dbias_ref[...]


def _hy_conv_block(ut, hc, d_bias):
    b, d, l = ut.shape
    n2 = 128
    n1 = 2 * l // n2
    ct = HY_CONV_CH_TILE
    tables = _dft_tables(n1, n2)
    kern = functools.partial(_hy_conv_kernel, ct=ct, n1=n1, n2=n2)
    full = lambda a: pl.BlockSpec(a.shape, lambda c, bb: (0,) * a.ndim)
    yt = pl.pallas_call(
        kern,
        grid=(d // ct, b),
        in_specs=[pl.BlockSpec((None, ct, n1 // 2, n2), lambda c, bb: (bb, c, 0, 0)),
                  pl.BlockSpec((ct, n1, n2), lambda c, bb: (c, 0, 0)),
                  pl.BlockSpec((ct, 1, 1), lambda c, bb: (c, 0, 0))] + [full(t) for t in tables],
        out_specs=pl.BlockSpec((None, ct, n1 // 2, n2), lambda c, bb: (bb, c, 0, 0)),
        out_shape=jax.ShapeDtypeStruct((b, d, n1 // 2, n2), F32),
        scratch_shapes=[pltpu.VMEM((2, ct, n1, n2), F32)],
        compiler_params=_mixer_params(),
        name="hyena_longconv",
    )(ut.reshape(b, d, n1 // 2, n2), hc.reshape(d, n1, n2), d_bias.reshape(d, 1, 1), *tables)
    return yt.reshape(b, d, l)


def _hyena_block(x, gain, w_in, b_in, conv_w, conv_b, f_w1, f_b1, f_w2, f_b2, f_w3, f_freq, d_bias, w_out, b_out):
    l = x.shape[1]
    x0, ut = _hy_in_block(x, gain, w_in, b_in, conv_w, conv_b)
    hc = _hy_filter_block(l, f_w1, f_b1, f_w2, f_b2, f_w3, f_freq)
    yt = _hy_conv_block(ut, hc, d_bias)
    return _hy_out_block(x, x0, yt, w_out, b_out)


def _hy_out_kernel(x_ref, x0_ref, yt_ref, wout_ref, bout_ref, o_ref):
    y = yt_ref[...].T
    z = jnp.dot((y * x0_ref[...]).astype(BF16), wout_ref[...], preferred_element_type=F32)
    o_ref[...] = x_ref[...] + z + bout_ref[...]


def _hy_out_block(x, x0, yt, w_out, b_out):
    b, l, d = x.shape
    tm = MIX_ROW_TILE
    row = pl.BlockSpec((None, tm, d), lambda bb, i: (bb, i, 0))
    return pl.pallas_call(
        _hy_out_kernel,
        grid=(b, l // tm),
        in_specs=[row, row, pl.BlockSpec((None, d, tm), lambda bb, i: (bb, 0, i)),
                  pl.BlockSpec((d, d), lambda bb, i: (0, 0), **_RESIDENT),
                  pl.BlockSpec((1, d), lambda bb, i: (0, 0))],
        out_specs=row,
        out_shape=jax.ShapeDtypeStruct((b, l, d), F32),
        compiler_params=_mixer_params(),
        name="hyena_out",
    )(x, x0, yt, w_out.astype(BF16), b_out.reshape(1, d))


LANES = 128


def _softplus(x):
    return jnp.maximum(x, 0.0) + jnp.log1p(jnp.exp(-jnp.abs(x)))


def _gdn_in_kernel(xp_ref, x_ref, xn_ref, gain_ref, win_ref, cw_ref, alog_ref, dtb_ref, cum_ref, eye_ref,
                   q_ref, k_ref, v_ref, z_ref, kt_ref, cols_ref, rows_ref, pairs_ref, s_ref, *, tm, d, heads):
    hi = lax.Precision.HIGHEST
    gain = gain_ref[...]
    h = _rms_rows(x_ref[...], gain).astype(BF16)
    hh = _rms_rows(jnp.concatenate([xp_ref[...], xn_ref[...]], axis=0), gain).astype(BF16)
    dk = d // heads
    for j, (out_ref, scale) in enumerate(((q_ref, dk ** -0.5), (k_ref, 1.0), (v_ref, None))):
        cols = slice(j * d, (j + 1) * d)
        w_cols = win_ref[:, cols].astype(BF16)
        main = jnp.dot(h, w_cols, preferred_element_type=F32)
        halo = jnp.dot(hh, w_cols, preferred_element_type=F32)
        _store_with_halo(s_ref, main, halo, tm)
        c = _conv3_from(s_ref, cw_ref[:, cols], tm)
        c = c * jax.nn.sigmoid(c)
        if scale is None:
            out_ref[...] = c.astype(BF16)
        else:
            normed = []
            for hd in range(heads):
                lanes = slice(hd * dk, (hd + 1) * dk)
                ch = c[:, lanes]
                ss = jnp.sum(ch * ch, axis=-1, keepdims=True)
                normed.append(ch * (lax.rsqrt(ss + 1e-6) * scale))
                out_ref[:, lanes] = normed[-1].astype(BF16)
            if j == 1:
                kt = jnp.concatenate(normed, axis=1).T
                n_chunk, _, c = kt_ref.shape
                for ci in range(n_chunk):
                    kt_ref[ci] = kt[:, ci * c:(ci + 1) * c]
    z_ref[...] = jnp.dot(h, win_ref[:, 3 * d:4 * d].astype(BF16), preferred_element_type=F32).astype(BF16)
    ab = jnp.dot(h, win_ref[:, 4 * d:].astype(BF16), preferred_element_type=F32)
    nh2 = 2 * heads
    g = -jnp.exp(alog_ref[...]) * _softplus(ab[:, :nh2] + dtb_ref[...])
    beta = jax.nn.sigmoid(ab[:, nh2:2 * nh2])
    g_terms = _bf16_terms(g)
    cumsum = lambda mask: sum(jnp.dot(mask, term, preferred_element_type=F32) for term in g_terms)
    gc_f = cumsum(cum_ref[0])
    gc_b = cumsum(cum_ref[1])
    chunk = tm // rows_ref.shape[1]
    for dr, gc in enumerate((gc_f, gc_b)):
        own = slice(dr * heads, (dr + 1) * heads)
        cols = jnp.concatenate([gc[:, own], beta[:, own]], axis=1)
        cols_ref[dr] = cols
        rows = lax.dot_general(eye_ref[...], cols, (((1,), (1,)), ((), ())), precision=hi,
                               preferred_element_type=F32)
        for ci in range(tm // chunk):
            chunk_rows = rows[:, ci * chunk:(ci + 1) * chunk]
            rows_ref[dr, ci] = chunk_rows
            pairs_ref[dr, ci] = jnp.concatenate(
                [jnp.concatenate([chunk_rows[r:r + 1], chunk_rows[r + 1:r + 2]], axis=1)
                 for r in range(0, nh2, 2)], axis=0)


def _chunk_cumsum_masks(tm, chunk):
    import numpy as np
    r = np.arange(tm)
    same = (r[:, None] // chunk) == (r[None, :] // chunk)
    lower = same & (r[None, :] <= r[:, None])
    upper = same & (r[None, :] >= r[:, None])
    return jnp.asarray(np.stack([lower, upper]).astype(np.float32)).astype(BF16)


def _gdn_in_block(x, gain, w_in, lead, conv_w, a_log, dt_bias):
    b, l, d = x.shape
    heads = GD_HEADS
    nh2 = 2 * heads
    tm = GD_ROW_TILE
    kern = functools.partial(_gdn_in_kernel, tm=tm, d=d, heads=heads)
    row = pl.BlockSpec((None, tm, d), lambda bb, i: (bb, i, 0))
    wide = jax.ShapeDtypeStruct((b, l, d), BF16)
    c = GD_CHUNK
    return pl.pallas_call(
        kern,
        grid=(b, l // tm),
        in_specs=_halo_specs(tm, d, l) + [
            pl.BlockSpec((1, d), lambda bb, i: (0, 0)),
            _stacked_weight_spec(w_in, lead),
            pl.BlockSpec((3, 3 * d), lambda bb, i: (0, 0)),
            pl.BlockSpec((1, nh2), lambda bb, i: (0, 0)),
            pl.BlockSpec((1, nh2), lambda bb, i: (0, 0)),
            pl.BlockSpec((2, tm, tm), lambda bb, i: (0, 0, 0), **_RESIDENT),
            pl.BlockSpec((nh2, nh2), lambda bb, i: (0, 0)),
        ],
        out_specs=[row, row, row, row,
                   pl.BlockSpec((None, tm // c, d, c), lambda bb, i: (bb, i, 0, 0)),
                   pl.BlockSpec((2, None, tm, nh2), lambda bb, i: (0, bb, i, 0)),
                   pl.BlockSpec((2, None, tm // c, nh2, c), lambda bb, i: (0, bb, i, 0, 0)),
                   pl.BlockSpec((2, None, tm // c, heads, 2 * c), lambda bb, i: (0, bb, i, 0, 0))],
        out_shape=[wide, wide, wide, wide, jax.ShapeDtypeStruct((b, l // c, d, c), F32),
                   jax.ShapeDtypeStruct((2, b, l, nh2), F32),
                   jax.ShapeDtypeStruct((2, b, l // c, nh2, c), F32),
                   jax.ShapeDtypeStruct((2, b, l // c, heads, 2 * c), F32)],
        scratch_shapes=[pltpu.VMEM((tm + 2 * HALO, d), F32)],
        compiler_params=_mixer_params(),
        name="gdn_in",
    )(x, x, x, gain.reshape(1, d), w_in, conv_w, a_log.reshape(1, nh2), dt_bias.reshape(1, nh2),
      _chunk_cumsum_masks(tm, GD_CHUNK), jnp.eye(nh2, dtype=F32))


GD_GROUP = 8


def _decay(gc_col, gc_row, after_or_same):
    return jnp.exp(jnp.where(after_or_same, gc_col - gc_row, -jnp.inf))


def _spread_exact(cols, sel):
    hi, mid, lo = (jnp.dot(term, sel, preferred_element_type=F32) for term in _bf16_terms(cols))
    return (hi + mid) + lo


def _head_pair_block_diag(left, right):
    zero = jnp.zeros_like(left)
    return jnp.concatenate([jnp.concatenate([left, zero], axis=1),
                            jnp.concatenate([zero, right], axis=1)], axis=0)


def _gdn_prep_kernel(k_ref, cols_ref, pairs_ref, selg_ref, selb_ref, a_ref, *, heads, dk, c, group):
    direction = pl.program_id(0)
    row = lax.broadcasted_iota(jnp.int32, (c, 2 * c), 0)
    col = lax.broadcasted_iota(jnp.int32, (c, 2 * c), 1) % c
    delta = (row - col) * (1 - 2 * direction)
    cols = cols_ref[...]
    g_pairs = _spread_exact(cols, selg_ref[...])
    b_pairs = _spread_exact(cols, selb_ref[...])
    n_pair = heads // 2
    for ci in range(group):
        rows = slice(ci * c, (ci + 1) * c)
        for m in range(n_pair):
            lanes = slice(2 * c * m, 2 * c * (m + 1))
            kp = k_ref[rows, 2 * dk * m:2 * dk * (m + 1)]
            gram = lax.dot_general(kp, _head_pair_block_diag(kp[:, :dk], kp[:, dk:]),
                                   (((1,), (1,)), ((), ())), preferred_element_type=F32)
            dec = jnp.exp(jnp.where(delta > 0, g_pairs[rows, lanes] - pairs_ref[ci, m:m + 1, :], -jnp.inf))
            a_ref[:, ci * n_pair + m, :] = b_pairs[rows, lanes] * gram * dec


def _pair_selectors(heads, c):
    eye = jnp.eye(2 * heads, heads, dtype=BF16)
    return jnp.repeat(eye, c, axis=1), jnp.repeat(jnp.roll(eye, heads, axis=0), c, axis=1)


def _gdn_prep_block(k, cols, pairs):
    b, l, d = k.shape
    heads, c, group = GD_HEADS, GD_CHUNK, GD_GROUP
    rows = c * group
    ng = l // rows
    kern = functools.partial(_gdn_prep_kernel, heads=heads, dk=d // heads, c=c, group=group)
    per = group * heads // 2
    full = lambda a: pl.BlockSpec(a.shape, lambda dr, bb, g: (0,) * a.ndim)
    selg, selb = _pair_selectors(heads, c)
    return pl.pallas_call(
        kern,
        grid=(2, b, ng),
        in_specs=[pl.BlockSpec((None, rows, d), lambda dr, bb, g: (bb, g, 0)),
                  pl.BlockSpec((None, None, rows, 2 * heads), lambda dr, bb, g: (dr, bb, g, 0)),
                  pl.BlockSpec((None, None, group, heads, 2 * c), lambda dr, bb, g: (dr, bb, g, 0, 0)),
                  full(selg), full(selb)],
        out_specs=pl.BlockSpec((None, c, per, 2 * c), lambda dr, bb, g: (dr, 0, bb * ng + g, 0)),
        out_shape=jax.ShapeDtypeStruct((2, c, b * ng * per, 2 * c), F32),
        compiler_params=pltpu.CompilerParams(dimension_semantics=("arbitrary",) * 3,
                                             vmem_limit_bytes=V7X_VMEM_LIMIT_BYTES),
        name="gdn_prep",
    )(k, cols, pairs, selg, selb)


SUBLANES = 8


def _gdn_tsolve_kernel(a_ref, t_ref, a3_ref, t3_ref, *, c, upper):
    p = a_ref.shape[1]
    for i in range(c):
        slab = a_ref[i].T
        a3_ref[0, i] = slab[:c]
        a3_ref[1, i] = slab[c:]
    t3_ref[...] = jnp.zeros_like(t3_ref)
    ntile = c // SUBLANES
    sub = lax.broadcasted_iota(jnp.int32, (SUBLANES, p), 0)
    order = range(c - 1, -1, -1) if upper else range(c)
    for i in order:
        solved = range(i + 1, c) if upper else range(i)
        lo, hi = (i // SUBLANES, ntile) if upper else (0, i // SUBLANES + 1)
        ti = i // SUBLANES
        for half in range(2):
            acc = [jnp.zeros((SUBLANES, p), F32) for _ in range(lo, hi)]
            for j in solved:
                jlo, jhi = (j // SUBLANES, ntile) if upper else (0, j // SUBLANES + 1)
                coef = a3_ref[half, i, j:j + 1, :]
                tj = t3_ref[half, j, jlo * SUBLANES:jhi * SUBLANES, :]
                for t in range(jlo, jhi):
                    acc[t - lo] = acc[t - lo] - coef * tj[(t - jlo) * SUBLANES:(t - jlo + 1) * SUBLANES]
            acc[ti - lo] = acc[ti - lo] + jnp.where(sub == i % SUBLANES, 1.0, 0.0)
            t3_ref[half, i, lo * SUBLANES:hi * SUBLANES, :] = jnp.concatenate(acc, axis=0)
    for ib in range(0, c, SUBLANES):
        slabs = jnp.stack([jnp.concatenate([t3_ref[0, i], t3_ref[1, i]], axis=0).T
                           for i in range(ib, ib + SUBLANES)], axis=0)
        t_ref[:, ib:ib + SUBLANES, :] = jnp.swapaxes(slabs, 0, 1)


def _gdn_tsolve_block(a, direction):
    _, c, pairs, _ = a.shape
    blk = LANES
    upper = bool(direction)
    kern = functools.partial(_gdn_tsolve_kernel, c=c, upper=upper)
    return pl.pallas_call(
        kern,
        grid=(pairs // blk,),
        in_specs=[pl.BlockSpec((None, c, blk, 2 * c), lambda i: (direction, 0, i, 0))],
        out_specs=pl.BlockSpec((blk, c, 2 * c), lambda i: (i, 0, 0)),
        out_shape=jax.ShapeDtypeStruct((pairs, c, 2 * c), F32),
        scratch_shapes=[pltpu.VMEM((2, c, c, blk), F32), pltpu.VMEM((2, c, c, blk), F32)],
        compiler_params=pltpu.CompilerParams(dimension_semantics=("arbitrary",),
                                             vmem_limit_bytes=V7X_VMEM_LIMIT_BYTES),
        name="gdn_tsolve_bwd" if upper else "gdn_tsolve_fwd",
    )(a)


def _gdn_scan_kernel(q_ref, k_ref, v_ref, kt_ref, cols_ref, rows_ref, pairs_ref, t_ref, sel_ref, *rest,
                     heads, dk, c, group, reverse, add_other):
    other_ref = rest[0] if add_other else None
    o_ref, s_ref, g_ref, qe_ref, rhs_ref, attn_ref, u_ref, w_ref = rest[int(add_other):]

    @pl.when(pl.program_id(1) == 0)
    def _():
        s_ref[...] = jnp.zeros_like(s_ref)

    g_all = _spread_exact(cols_ref[...], sel_ref[...])
    g_ref[...] = g_all
    eg = jnp.exp(g_all)
    qe_ref[...] = (q_ref[...] * eg).astype(BF16)
    ke = k_ref[...] * eg
    for hd in range(heads):
        lanes = slice(hd * dk, (hd + 1) * dk)
        rhs_ref[:, 2 * hd * dk:(2 * hd + 1) * dk] = v_ref[:, lanes]
        rhs_ref[:, (2 * hd + 1) * dk:(2 * hd + 2) * dk] = ke[:, lanes].astype(BF16)

    row = lax.broadcasted_iota(jnp.int32, (c, c), 0)
    col = lax.broadcasted_iota(jnp.int32, (c, c), 1)
    incl = (row <= col) if reverse else (row >= col)
    head_lanes = [slice(hd * dk, (hd + 1) * dk) for hd in range(heads)]
    n_pair = heads // 2

    for ci in range(group):
        rows = slice(ci * c, (ci + 1) * c)
        for hd, lanes in enumerate(head_lanes):
            dec = _decay(g_ref[rows, hd * dk:hd * dk + c], rows_ref[ci, hd:hd + 1, :], incl)
            gram = lax.dot_general(q_ref[rows, lanes], k_ref[rows, lanes],
                                   (((1,), (1,)), ((), ())), preferred_element_type=F32)
            attn_ref[ci * heads + hd] = (gram * dec).astype(BF16)
        for m in range(n_pair):
            width = 2 * dk
            t_pair = t_ref[ci * n_pair + m] * pairs_ref[ci, n_pair + m:n_pair + m + 1, :]
            rhs_pair = _head_pair_block_diag(rhs_ref[rows, 2 * m * width:(2 * m + 1) * width],
                                             rhs_ref[rows, (2 * m + 1) * width:(2 * m + 2) * width])
            sol = jnp.dot(t_pair.astype(BF16), rhs_pair, preferred_element_type=F32)
            for par in range(2):
                lanes = head_lanes[2 * m + par]
                u_ref[rows, lanes] = sol[:, par * width:par * width + dk]
                w_ref[rows, lanes] = sol[:, par * width + dk:(par + 1) * width].astype(BF16)

    for step in range(group):
        ci = group - 1 - step if reverse else step
        rows = slice(ci * c, (ci + 1) * c)
        last = ci * c if reverse else (ci + 1) * c - 1
        states = [s_ref[hd] for hd in range(heads)]
        states_bf = [s.astype(BF16) for s in states]
        w_s = [jnp.dot(w_ref[rows, lanes], s, preferred_element_type=F32)
               for lanes, s in zip(head_lanes, states_bf)]
        q_s = [jnp.dot(qe_ref[rows, lanes], s, preferred_element_type=F32)
               for lanes, s in zip(head_lanes, states_bf)]
        for hd, lanes in enumerate(head_lanes):
            v_new_bf = (u_ref[rows, lanes] - w_s[hd]).astype(BF16)
            o = q_s[hd] + jnp.dot(attn_ref[ci * heads + hd], v_new_bf, preferred_element_type=F32)
            o_ref[rows, lanes] = o + other_ref[rows, lanes] if add_other else o
            g_tot = g_ref[last:last + 1, lanes]
            k_dec_t = (kt_ref[ci, lanes, :] * jnp.exp(g_tot[:, :c] - rows_ref[ci, hd:hd + 1, :])).astype(BF16)
            s_ref[hd] = states[hd] * jnp.exp(g_tot) + jnp.dot(k_dec_t, v_new_bf, preferred_element_type=F32)


def _gdn_scan_block(q, k, v, kt, cols, rows_t, pairs, t, reverse, other=None):
    b, l, d = q.shape
    heads, c, group = GD_HEADS, GD_CHUNK, GD_GROUP
    dk = d // heads
    rows = c * group
    ng = l // rows
    per = group * heads
    dr = int(reverse)
    add_other = other is not None
    kern = functools.partial(_gdn_scan_kernel, heads=heads, dk=dk, c=c, group=group, reverse=reverse,
                             add_other=add_other)
    visit = (lambda g: ng - 1 - g) if reverse else (lambda g: g)
    wide = pl.BlockSpec((None, rows, d), lambda bb, g: (bb, visit(g), 0))
    sel = jnp.repeat(jnp.eye(2 * heads, heads, dtype=BF16), dk, axis=1)
    return pl.pallas_call(
        kern,
        grid=(b, ng),
        in_specs=[wide, wide, wide,
                  pl.BlockSpec((None, group, d, c), lambda bb, g: (bb, visit(g), 0, 0)),
                  pl.BlockSpec((None, None, rows, 2 * heads), lambda bb, g: (dr, bb, visit(g), 0)),
                  pl.BlockSpec((None, None, group, 2 * heads, c), lambda bb, g: (dr, bb, visit(g), 0, 0)),
                  pl.BlockSpec((None, None, group, heads, 2 * c), lambda bb, g: (dr, bb, visit(g), 0, 0)),
                  pl.BlockSpec((per // 2, c, 2 * c), lambda bb, g: (bb * ng + visit(g), 0, 0)),
                  pl.BlockSpec((2 * heads, d), lambda bb, g: (0, 0))] + [wide] * add_other,
        out_specs=wide,
        out_shape=jax.ShapeDtypeStruct((b, l, d), F32),
        scratch_shapes=[pltpu.VMEM((heads, dk, dk), F32), pltpu.VMEM((rows, d), F32),
                        pltpu.VMEM((rows, d), BF16), pltpu.VMEM((rows, 2 * d), BF16),
                        pltpu.VMEM((per, c, c), BF16),
                        pltpu.VMEM((rows, d), F32), pltpu.VMEM((rows, d), BF16)],
        compiler_params=_mixer_params(),
        name="gdn_scan_bwd" if reverse else "gdn_scan_fwd",
    )(q, k, v, kt, cols, rows_t, pairs, t, sel, *([other] if add_other else []))


def _gdn_out_kernel(x_ref, o_ref, z_ref, ng_ref, wout_ref, y_ref, g_ref, *, heads, dk):
    o = o_ref[...]
    z = z_ref[...].astype(F32)
    for hd in range(heads):
        lanes = slice(hd * dk, (hd + 1) * dk)
        oh = o[:, lanes]
        ms = jnp.mean(oh * oh, axis=-1, keepdims=True)
        zh = z[:, lanes]
        g_ref[:, lanes] = (oh * lax.rsqrt(ms + NORM_EPS) * ng_ref[...] * (zh * jax.nn.sigmoid(zh))).astype(BF16)
    y_ref[...] = x_ref[...] + jnp.dot(g_ref[...], wout_ref[...], preferred_element_type=F32)


def _gdn_out_block(x, o, z, norm_g, w_out):
    b, l, d = x.shape
    heads = GD_HEADS
    dk = d // heads
    tm = GD_ROW_TILE
    row = pl.BlockSpec((None, tm, d), lambda bb, i: (bb, i, 0))
    kern = functools.partial(_gdn_out_kernel, heads=heads, dk=dk)
    return pl.pallas_call(
        kern,
        grid=(b, l // tm),
        in_specs=[row, row, row,
                  pl.BlockSpec((1, dk), lambda bb, i: (0, 0)),
                  pl.BlockSpec((d, d), lambda bb, i: (0, 0), **_RESIDENT)],
        out_specs=row,
        out_shape=jax.ShapeDtypeStruct((b, l, d), F32),
        scratch_shapes=[pltpu.VMEM((tm, d), BF16)],
        compiler_params=_mixer_params(),
        name="gdn_out",
    )(x, o, z, norm_g.reshape(1, dk), w_out.astype(BF16))


def _gdn_block(x, gain, w_in, lead, conv_w, a_log, dt_bias, norm_g, w_out):
    q, k, v, z, kt, cols, rows_t, pairs = _gdn_in_block(x, gain, w_in, lead, conv_w, a_log, dt_bias)
    a = _gdn_prep_block(k, cols, pairs)
    o = None
    for dr in range(2):
        o = _gdn_scan_block(q, k, v, kt, cols, rows_t, pairs, _gdn_tsolve_block(a, dr), bool(dr), other=o)
    return _gdn_out_block(x, o, z, norm_g, w_out)


def kernel(x, norms, final_norm, ffn_w_in, ffn_w_out, sc_w_in, sc_conv, sc_w_out, hy_w_in, hy_b_in, hy_conv, hy_conv_b, hy_f_w1, hy_f_b1, hy_f_w2, hy_f_b2, hy_f_w3, hy_f_freq, hy_d, hy_w_out, hy_b_out, gd_w_in, gd_conv, gd_a_log, gd_dt_bias, gd_norm, gd_w_out):
    b, l, d = x.shape
    depth = norms.shape[0]

    def ffn(xx, i, k):
        final = final_norm if (i == depth - 1 and k == 1) else None
        return _ffn_block(xx.reshape(b * l, d), norms[i, 2 * k], ffn_w_in, ffn_w_out, (i, k),
                          final).reshape(b, l, d)

    for i in range(depth):
        m, j = i % N_MIXERS, i // N_MIXERS
        x = ffn(x, i, 0)
        if m == 0:
            x = _sc_mixer_block(x, norms[i, 1], sc_w_in[j], sc_conv[j], sc_w_out[j])
        elif m == 1:
            x = _hyena_block(x, norms[i, 1], hy_w_in[j], hy_b_in[j], hy_conv[j], hy_conv_b[j],
                             hy_f_w1[j], hy_f_b1[j], hy_f_w2[j], hy_f_b2[j], hy_f_w3[j],
                             hy_f_freq[j], hy_d[j], hy_w_out[j], hy_b_out[j])
        else:
            x = _gdn_block(x, norms[i, 1], gd_w_in, (j,), gd_conv[j], gd_a_log[j], gd_dt_bias[j],
                           gd_norm[j], gd_w_out[j])
        x = ffn(x, i, 1)
    return x
```

```python
import functools
import math

import jax
import jax.numpy as jnp
from jax import lax
from jax.experimental import pallas as pl
from jax.experimental.pallas import tpu as pltpu

F32 = jnp.float32
BF16 = jnp.bfloat16

NORM_EPS = 1e-6
N_MIXERS = 3
HY_EMB = 33
HY_BANDS = (HY_EMB - 1) // 2
HY_MAX_DECAY = math.log(1e-2) / 0.3
HY_MIN_DECAY = math.log(1e-2) / 1.5
GD_HEADS = 8
GD_CHUNK = 64

V7X_VMEM_LIMIT_BYTES = 56 * 1024 * 1024
FFN_ROW_TILE = 512
FFN_HIDDEN_TILE = 256


def _ffn_kernel(x_ref, gain_ref, win_ref, wout_ref, *rest, n_hidden_tiles, hidden_tile, final_norm):
    final_ref, o_ref, hid_ref = rest if final_norm else (None,) + rest
    x = x_ref[...]
    xn = _rms_rows(x, gain_ref[...]).astype(BF16)
    f = n_hidden_tiles * hidden_tile
    for j in range(n_hidden_tiles):
        cols = slice(j * hidden_tile, (j + 1) * hidden_tile)
        up_cols = slice(f + j * hidden_tile, f + (j + 1) * hidden_tile)
        g = jnp.dot(xn, win_ref[:, cols].astype(BF16), preferred_element_type=F32)
        u = jnp.dot(xn, win_ref[:, up_cols].astype(BF16), preferred_element_type=F32)
        hid_ref[:, cols] = (g * jax.nn.sigmoid(g) * u).astype(BF16)
    y = x + 0.5 * jnp.dot(hid_ref[...], wout_ref[...].astype(BF16), preferred_element_type=F32)
    o_ref[...] = _rms_rows(y, final_ref[...]) if final_norm else y


def _stacked_weight_spec(w, lead):
    tail = w.shape[len(lead):]
    index = tuple(lead) + (0,) * len(tail)
    return pl.BlockSpec((None,) * len(lead) + tail, lambda *_: index, **_RESIDENT)


def _ffn_block(x2d, gain, w_in, w_out, lead, final_gain=None):
    t, d = x2d.shape
    f = w_out.shape[-2]
    tm = FFN_ROW_TILE
    final_norm = final_gain is not None
    kern = functools.partial(_ffn_kernel, n_hidden_tiles=f // FFN_HIDDEN_TILE, hidden_tile=FFN_HIDDEN_TILE,
                             final_norm=final_norm)
    vec = pl.BlockSpec((1, d), lambda i: (0, 0))
    extra = ([vec], [final_gain.reshape(1, d)]) if final_norm else ([], [])
    return pl.pallas_call(
        kern,
        grid=(t // tm,),
        in_specs=[
            pl.BlockSpec((tm, d), lambda i: (i, 0)),
            vec,
            _stacked_weight_spec(w_in, lead),
            _stacked_weight_spec(w_out, lead),
        ] + extra[0],
        out_specs=pl.BlockSpec((tm, d), lambda i: (i, 0)),
        out_shape=jax.ShapeDtypeStruct((t, d), F32),
        scratch_shapes=[pltpu.VMEM((tm, f), BF16)],
        compiler_params=pltpu.CompilerParams(
            dimension_semantics=("arbitrary",), vmem_limit_bytes=V7X_VMEM_LIMIT_BYTES),
        name="ffn_final" if final_norm else "ffn_block",
    )(x2d, gain.reshape(1, d), w_in, w_out, *extra[1])


MIX_ROW_TILE = 1024
GD_ROW_TILE = 512
HALO = 8


def _rms_rows(x, gain):
    ms = jnp.mean(x * x, axis=-1, keepdims=True)
    return x * lax.rsqrt(ms + NORM_EPS) * gain


def _bf16_terms(x):
    hi = x.astype(BF16)
    rest = x - hi.astype(F32)
    mid = rest.astype(BF16)
    lo = (rest - mid.astype(F32)).astype(BF16)
    return hi, mid, lo


def _halo_specs(tm, d, seq_len):
    per = tm // HALO
    last_blk = seq_len // HALO - 1
    return [
        pl.BlockSpec((None, HALO, d), lambda b, i: (b, jnp.maximum(i * per - 1, 0), 0)),
        pl.BlockSpec((None, tm, d), lambda b, i: (b, i, 0)),
        pl.BlockSpec((None, HALO, d), lambda b, i: (b, jnp.minimum((i + 1) * per, last_blk), 0)),
    ]


def _store_with_halo(s_ref, main, halo, tm):
    i = pl.program_id(1)
    last = pl.num_programs(1) - 1
    s_ref[0:HALO, :] = jnp.where(i > 0, halo[:HALO], 0.0)
    s_ref[HALO:HALO + tm, :] = main
    s_ref[HALO + tm:2 * HALO + tm, :] = jnp.where(i < last, halo[HALO:], 0.0)


def _conv3_from(s_ref, cw, tm):
    return (cw[0:1] * s_ref[HALO - 1:HALO - 1 + tm, :] + cw[1:2] * s_ref[HALO:HALO + tm, :]
            + cw[2:3] * s_ref[HALO + 1:HALO + 1 + tm, :])


_RESIDENT = dict(pipeline_mode=pl.Buffered(1))


def _mixer_params():
    return pltpu.CompilerParams(dimension_semantics=("arbitrary", "arbitrary"),
                                vmem_limit_bytes=V7X_VMEM_LIMIT_BYTES)


def _sc_mixer_kernel(xp_ref, x_ref, xn_ref, gain_ref, win_ref, cw_ref, wout_ref, o_ref, ch_ref, *, tm, d):
    gain = gain_ref[...]
    x = x_ref[...]
    h = _rms_rows(x, gain).astype(BF16)
    w_c = win_ref[:, d:2 * d].astype(BF16)
    w_h = win_ref[:, 2 * d:].astype(BF16)
    c_main = jnp.dot(h, w_c, preferred_element_type=F32)
    h_main = jnp.dot(h, w_h, preferred_element_type=F32)
    hh = _rms_rows(jnp.concatenate([xp_ref[...], xn_ref[...]], axis=0), gain).astype(BF16)
    c_halo = jnp.dot(hh, w_c, preferred_element_type=F32)
    h_halo = jnp.dot(hh, w_h, preferred_element_type=F32)
    _store_with_halo(ch_ref, c_main * h_main, c_halo * h_halo, tm)
    conv = _conv3_from(ch_ref, cw_ref[...], tm)
    b_main = jnp.dot(h, win_ref[:, :d].astype(BF16), preferred_element_type=F32)
    y = jnp.dot((b_main * conv).astype(BF16), wout_ref[...].astype(BF16), preferred_element_type=F32)
    o_ref[...] = x + y


def _sc_mixer_block(x, gain, w_in, conv_w, w_out, lead):
    b, l, d = x.shape
    tm = MIX_ROW_TILE
    kern = functools.partial(_sc_mixer_kernel, tm=tm, d=d)
    return pl.pallas_call(
        kern,
        grid=(b, l // tm),
        in_specs=_halo_specs(tm, d, l) + [
            pl.BlockSpec((1, d), lambda bb, i: (0, 0)),
            _stacked_weight_spec(w_in, lead),
            pl.BlockSpec((3, d), lambda bb, i: (0, 0)),
            _stacked_weight_spec(w_out, lead),
        ],
        out_specs=pl.BlockSpec((None, tm, d), lambda bb, i: (bb, i, 0)),
        out_shape=jax.ShapeDtypeStruct((b, l, d), F32),
        scratch_shapes=[pltpu.VMEM((tm + 2 * HALO, d), F32)],
        compiler_params=_mixer_params(),
        name="sc_mixer",
    )(x, x, x, gain.reshape(1, d), w_in, conv_w, w_out)


def _hy_in_kernel(xp_ref, x_ref, xn_ref, gain_ref, win_ref, bin_ref, cw_ref, cb_ref, x0_ref, vxt_ref, s_ref,
                  *, tm, d):
    gain = gain_ref[...]
    h = _rms_rows(x_ref[...], gain).astype(BF16)
    hh = _rms_rows(jnp.concatenate([xp_ref[...], xn_ref[...]], axis=0), gain).astype(BF16)
    parts = []
    for j in range(3):
        cols = slice(j * d, (j + 1) * d)
        bias = bin_ref[:, cols]
        w_cols = win_ref[:, cols].astype(BF16)
        main = jnp.dot(h, w_cols, preferred_element_type=F32) + bias
        halo = jnp.dot(hh, w_cols, preferred_element_type=F32) + bias
        _store_with_halo(s_ref, main, halo, tm)
        parts.append(_conv3_from(s_ref, cw_ref[:, cols], tm) + cb_ref[:, cols])
    x0_ref[...] = parts[0]
    vxt_ref[...] = (parts[2] * parts[1]).T


def _hy_in_block(x, gain, w_in, lead, b_in, conv_w, conv_b):
    b, l, d = x.shape
    tm = MIX_ROW_TILE
    kern = functools.partial(_hy_in_kernel, tm=tm, d=d)
    return pl.pallas_call(
        kern,
        grid=(b, l // tm),
        in_specs=_halo_specs(tm, d, l) + [
            pl.BlockSpec((1, d), lambda bb, i: (0, 0)),
            _stacked_weight_spec(w_in, lead),
            pl.BlockSpec((1, 3 * d), lambda bb, i: (0, 0)),
            pl.BlockSpec((3, 3 * d), lambda bb, i: (0, 0)),
            pl.BlockSpec((1, 3 * d), lambda bb, i: (0, 0)),
        ],
        out_specs=[pl.BlockSpec((None, tm, d), lambda bb, i: (bb, i, 0)),
                   pl.BlockSpec((None, d, tm), lambda bb, i: (bb, 0, i))],
        out_shape=[jax.ShapeDtypeStruct((b, l, d), F32), jax.ShapeDtypeStruct((b, d, l), F32)],
        scratch_shapes=[pltpu.VMEM((tm + 2 * HALO, d), F32)],
        compiler_params=_mixer_params(),
        name="hyena_in",
    )(x, x, x, gain.reshape(1, d), w_in, b_in.reshape(1, 3 * d), conv_w, conv_b.reshape(1, 3 * d))


HY_FEAT_PAD = 64
HY_FILTER_TILE = 2048


def _hy_filter_kernel(zt_ref, w1_ref, b1_ref, w2_ref, b2_ref, freq_ref, w3_ref, delta_ref, hc_ref):
    def t_dot(w, a):
        return lax.dot_general(w, a, (((0,), (0,)), ((), ())), precision=lax.Precision.HIGHEST,
                               preferred_element_type=F32)

    z = zt_ref[...]
    freq = freq_ref[...]
    h1 = jnp.sin(freq * (t_dot(w1_ref[...], z) + b1_ref[...]))
    h2 = jnp.sin(freq * (t_dot(w2_ref[...], h1) + b2_ref[...]))
    h = t_dot(w3_ref[...], h2)
    t_row = z[0:1, :]
    mask_row = z[HY_EMB:HY_EMB + 1, :]
    hc_ref[...] = h * jnp.exp(-t_row * delta_ref[...]) * mask_row


def _hy_filter_features(l):
    p = jnp.arange(2 * l)
    pos = jnp.where(p < l, p, 2 * l - p)
    valid = (p != l).astype(F32)
    pos = jnp.minimum(pos, l - 1).astype(F32)[None, :]
    t = pos / (l - 1)
    w = 2.0 * math.pi * pos / l
    f = jnp.linspace(1e-4, HY_BANDS - 1, HY_BANDS, dtype=F32)[:, None]
    z = jnp.concatenate([t, jnp.cos(f * w), -jnp.sin(f * w), valid[None, :]], axis=0)
    return jnp.pad(z, ((0, HY_FEAT_PAD - z.shape[0]), (0, 0)))


def _hy_filter_block(l, w1, b1, w2, b2, w3, freq):
    order = w1.shape[1]
    d = w3.shape[1] // 2
    tl = min(HY_FILTER_TILE, l)
    half = l // tl
    zt = _hy_filter_features(l)
    w1p = jnp.pad(w1, ((0, HY_FEAT_PAD - w1.shape[0]), (0, 0)))
    delta = jnp.abs(jnp.linspace(HY_MIN_DECAY, HY_MAX_DECAY, d, dtype=F32)).reshape(d, 1)
    col = lambda v: v.reshape(order, 1)
    full = lambda shape: pl.BlockSpec(shape, lambda i: (0,) * len(shape))
    return pl.pallas_call(
        _hy_filter_kernel,
        grid=(2 * half,),
        in_specs=[pl.BlockSpec((HY_FEAT_PAD, tl), lambda i: (0, i)),
                  full((HY_FEAT_PAD, order)), full((order, 1)), full((order, order)), full((order, 1)),
                  full((order, 1)),
                  pl.BlockSpec((order, d), lambda i: (0, i // half)),
                  full((d, 1))],
        out_specs=pl.BlockSpec((d, tl), lambda i: (0, i)),
        out_shape=jax.ShapeDtypeStruct((d, 2 * l), F32),
        compiler_params=pltpu.CompilerParams(dimension_semantics=("arbitrary",),
                                             vmem_limit_bytes=V7X_VMEM_LIMIT_BYTES),
        name="hyena_filter",
    )(zt, w1p, col(b1), w2, col(b2), col(freq), w3, delta)


HY_CONV_CH_TILE = 32


def _dft_tables(n1, n2):
    import numpy as np
    n = n1 * n2
    i1, i2 = np.arange(n1), np.arange(n2)
    a1 = 2.0 * np.pi * np.outer(i1, i1) / n1
    a2 = 2.0 * np.pi * np.outer(i2, i2) / n2
    c1, s1, c2, s2 = np.cos(a1), np.sin(a1), np.cos(a2), np.sin(a2)
    tw = 2.0 * np.pi * np.outer(i2, i1) / n
    f_real = np.concatenate([c1, -s1], axis=1)
    f_cplx = np.block([[c2, -s2], [s2, c2]])
    i_cplx = np.block([[c2, s2], [-s2, c2]])
    i_real = np.concatenate([c1, -s1], axis=0) / n
    as_bf = lambda a: jnp.asarray(a, dtype=F32).astype(BF16)
    as_f = lambda a: jnp.asarray(a, dtype=F32)
    return (as_bf(f_real), as_bf(f_cplx), as_bf(i_cplx), as_bf(i_real),
            as_f(np.cos(tw)), as_f(np.sin(tw)), as_f(np.cos(tw.T)), as_f(np.sin(tw.T)))


def _swap(a):
    return jnp.swapaxes(a, 1, 2)


def _hy_conv_kernel(ut_ref, hc_ref, dbias_ref, freal_ref, fcplx_ref, icplx_ref, ireal_ref, twc_ref, tws_ref,
                    twct_ref, twst_ref, yt_ref, hspec_ref, *, ct, n1, n2):
    rows = ct * n2
    twc = twc_ref[...]
    tws = tws_ref[...]

    def mm(a, m_ref):
        return jnp.dot(a.astype(BF16), m_ref[...], preferred_element_type=F32)

    def forward(x):
        p = mm(_swap(x).reshape(rows, n1), freal_ref)
        pr = p[:, :n1].reshape(ct, n2, n1)
        pi = p[:, n1:].reshape(ct, n2, n1)
        qr = pr * twc + pi * tws
        qi = pi * twc - pr * tws
        q = jnp.concatenate([_swap(qr).reshape(ct * n1, n2), _swap(qi).reshape(ct * n1, n2)], axis=1)
        xs = mm(q, fcplx_ref)
        return xs[:, :n2].reshape(ct, n1, n2), xs[:, n2:].reshape(ct, n1, n2)

    @pl.when(pl.program_id(1) == 0)
    def _():
        hr, hi = forward(hc_ref[...])
        hspec_ref[0] = hr
        hspec_ref[1] = hi

    u = ut_ref[...]
    xr, xi = forward(jnp.concatenate([u, jnp.zeros_like(u)], axis=1))
    hr = hspec_ref[0]
    hi = hspec_ref[1]
    yr = xr * hr - xi * hi
    yi = xr * hi + xi * hr
    r = mm(jnp.concatenate([yr.reshape(ct * n1, n2), yi.reshape(ct * n1, n2)], axis=1), icplx_ref)
    rr = r[:, :n2].reshape(ct, n1, n2)
    ri = r[:, n2:].reshape(ct, n1, n2)
    twct = twct_ref[...]
    twst = twst_ref[...]
    sr = rr * twct - ri * twst
    si = ri * twct + rr * twst
    s = jnp.concatenate([_swap(sr).reshape(rows, n1), _swap(si).reshape(rows, n1)], axis=1)
    z = mm(s, ireal_ref).reshape(ct, n2, n1)
    y = _swap(z)[:, :n1 // 2, :]
    yt_ref[...] = y + u * dbias_ref[...]


def _hy_conv_block(ut, hc, d_bias):
    b, d, l = ut.shape
    n2 = 128
    n1 = 2 * l // n2
    ct = HY_CONV_CH_TILE
    tables = _dft_tables(n1, n2)
    kern = functools.partial(_hy_conv_kernel, ct=ct, n1=n1, n2=n2)
    full = lambda a: pl.BlockSpec(a.shape, lambda c, bb: (0,) * a.ndim)
    yt = pl.pallas_call(
        kern,
        grid=(d // ct, b),
        in_specs=[pl.BlockSpec((None, ct, n1 // 2, n2), lambda c, bb: (bb, c, 0, 0)),
                  pl.BlockSpec((ct, n1, n2), lambda c, bb: (c, 0, 0)),
                  pl.BlockSpec((ct, 1, 1), lambda c, bb: (c, 0, 0))] + [full(t) for t in tables],
        out_specs=pl.BlockSpec((None, ct, n1 // 2, n2), lambda c, bb: (bb, c, 0, 0)),
        out_shape=jax.ShapeDtypeStruct((b, d, n1 // 2, n2), F32),
        scratch_shapes=[pltpu.VMEM((2, ct, n1, n2), F32)],
        compiler_params=_mixer_params(),
        name="hyena_longconv",
    )(ut.reshape(b, d, n1 // 2, n2), hc.reshape(d, n1, n2), d_bias.reshape(d, 1, 1), *tables)
    return yt.reshape(b, d, l)


def _hyena_block(x, gain, lead, w_in, b_in, conv_w, conv_b, f_w1, f_b1, f_w2, f_b2, f_w3, f_freq, d_bias, w_out, b_out):
    l = x.shape[1]
    x0, ut = _hy_in_block(x, gain, w_in, lead, b_in, conv_w, conv_b)
    hc = _hy_filter_block(l, f_w1, f_b1, f_w2, f_b2, f_w3, f_freq)
    yt = _hy_conv_block(ut, hc, d_bias)
    return _hy_out_block(x, x0, yt, w_out, lead, b_out)


def _hy_out_kernel(x_ref, x0_ref, yt_ref, wout_ref, bout_ref, o_ref):
    y = yt_ref[...].T
    z = jnp.dot((y * x0_ref[...]).astype(BF16), wout_ref[...].astype(BF16), preferred_element_type=F32)
    o_ref[...] = x_ref[...] + z + bout_ref[...]


def _hy_out_block(x, x0, yt, w_out, lead, b_out):
    b, l, d = x.shape
    tm = MIX_ROW_TILE
    row = pl.BlockSpec((None, tm, d), lambda bb, i: (bb, i, 0))
    return pl.pallas_call(
        _hy_out_kernel,
        grid=(b, l // tm),
        in_specs=[row, row, pl.BlockSpec((None, d, tm), lambda bb, i: (bb, 0, i)),
                  _stacked_weight_spec(w_out, lead),
                  pl.BlockSpec((1, d), lambda bb, i: (0, 0))],
        out_specs=row,
        out_shape=jax.ShapeDtypeStruct((b, l, d), F32),
        compiler_params=_mixer_params(),
        name="hyena_out",
    )(x, x0, yt, w_out, b_out.reshape(1, d))


LANES = 128


def _softplus(x):
    return jnp.maximum(x, 0.0) + jnp.log1p(jnp.exp(-jnp.abs(x)))


def _gdn_in_kernel(xp_ref, x_ref, xn_ref, gain_ref, win_ref, cw_ref, alog_ref, dtb_ref, cum_ref, eye_ref,
                   q_ref, k_ref, v_ref, z_ref, kt_ref, cols_ref, rows_ref, pairs_ref, s_ref, *, tm, d, heads):
    hi = lax.Precision.HIGHEST
    gain = gain_ref[...]
    h = _rms_rows(x_ref[...], gain).astype(BF16)
    hh = _rms_rows(jnp.concatenate([xp_ref[...], xn_ref[...]], axis=0), gain).astype(BF16)
    dk = d // heads
    for j, (out_ref, scale) in enumerate(((q_ref, dk ** -0.5), (k_ref, 1.0), (v_ref, None))):
        cols = slice(j * d, (j + 1) * d)
        w_cols = win_ref[:, cols].astype(BF16)
        main = jnp.dot(h, w_cols, preferred_element_type=F32)
        halo = jnp.dot(hh, w_cols, preferred_element_type=F32)
        _store_with_halo(s_ref, main, halo, tm)
        c = _conv3_from(s_ref, cw_ref[:, cols], tm)
        c = c * jax.nn.sigmoid(c)
        if scale is None:
            out_ref[...] = c.astype(BF16)
        else:
            normed = []
            for hd in range(heads):
                lanes = slice(hd * dk, (hd + 1) * dk)
                ch = c[:, lanes]
                ss = jnp.sum(ch * ch, axis=-1, keepdims=True)
                normed.append(ch * (lax.rsqrt(ss + 1e-6) * scale))
                out_ref[:, lanes] = normed[-1].astype(BF16)
            if j == 1:
                kt = jnp.concatenate(normed, axis=1).T
                n_chunk, _, c = kt_ref.shape
                for ci in range(n_chunk):
                    kt_ref[ci] = kt[:, ci * c:(ci + 1) * c]
    z_ref[...] = jnp.dot(h, win_ref[:, 3 * d:4 * d].astype(BF16), preferred_element_type=F32).astype(BF16)
    ab = jnp.dot(h, win_ref[:, 4 * d:].astype(BF16), preferred_element_type=F32)
    nh2 = 2 * heads
    g = -jnp.exp(alog_ref[...]) * _softplus(ab[:, :nh2] + dtb_ref[...])
    beta = jax.nn.sigmoid(ab[:, nh2:2 * nh2])
    g_terms = _bf16_terms(g)
    cumsum = lambda mask: sum(jnp.dot(mask, term, preferred_element_type=F32) for term in g_terms)
    gc_f = cumsum(cum_ref[0])
    gc_b = cumsum(cum_ref[1])
    chunk = tm // rows_ref.shape[1]
    for dr, gc in enumerate((gc_f, gc_b)):
        own = slice(dr * heads, (dr + 1) * heads)
        cols = jnp.concatenate([gc[:, own], beta[:, own]], axis=1)
        cols_ref[dr] = cols
        rows = lax.dot_general(eye_ref[...], cols, (((1,), (1,)), ((), ())), precision=hi,
                               preferred_element_type=F32)
        for ci in range(tm // chunk):
            chunk_rows = rows[:, ci * chunk:(ci + 1) * chunk]
            rows_ref[dr, ci] = chunk_rows
            pairs_ref[dr, ci] = jnp.concatenate(
                [jnp.concatenate([chunk_rows[r:r + 1], chunk_rows[r + 1:r + 2]], axis=1)
                 for r in range(0, nh2, 2)], axis=0)


def _chunk_cumsum_masks(tm, chunk):
    import numpy as np
    r = np.arange(tm)
    same = (r[:, None] // chunk) == (r[None, :] // chunk)
    lower = same & (r[None, :] <= r[:, None])
    upper = same & (r[None, :] >= r[:, None])
    return jnp.asarray(np.stack([lower, upper]).astype(np.float32)).astype(BF16)


def _gdn_in_block(x, gain, w_in, lead, conv_w, a_log, dt_bias):
    b, l, d = x.shape
    heads = GD_HEADS
    nh2 = 2 * heads
    tm = GD_ROW_TILE
    kern = functools.partial(_gdn_in_kernel, tm=tm, d=d, heads=heads)
    row = pl.BlockSpec((None, tm, d), lambda bb, i: (bb, i, 0))
    wide = jax.ShapeDtypeStruct((b, l, d), BF16)
    c = GD_CHUNK
    return pl.pallas_call(
        kern,
        grid=(b, l // tm),
        in_specs=_halo_specs(tm, d, l) + [
            pl.BlockSpec((1, d), lambda bb, i: (0, 0)),
            _stacked_weight_spec(w_in, lead),
            pl.BlockSpec((3, 3 * d), lambda bb, i: (0, 0)),
            pl.BlockSpec((1, nh2), lambda bb, i: (0, 0)),
            pl.BlockSpec((1, nh2), lambda bb, i: (0, 0)),
            pl.BlockSpec((2, tm, tm), lambda bb, i: (0, 0, 0), **_RESIDENT),
            pl.BlockSpec((nh2, nh2), lambda bb, i: (0, 0)),
        ],
        out_specs=[row, row, row, row,
                   pl.BlockSpec((None, tm // c, d, c), lambda bb, i: (bb, i, 0, 0)),
                   pl.BlockSpec((2, None, tm, nh2), lambda bb, i: (0, bb, i, 0)),
                   pl.BlockSpec((2, None, tm // c, nh2, c), lambda bb, i: (0, bb, i, 0, 0)),
                   pl.BlockSpec((2, None, tm // c, heads, 2 * c), lambda bb, i: (0, bb, i, 0, 0))],
        out_shape=[wide, wide, wide, wide, jax.ShapeDtypeStruct((b, l // c, d, c), F32),
                   jax.ShapeDtypeStruct((2, b, l, nh2), F32),
                   jax.ShapeDtypeStruct((2, b, l // c, nh2, c), F32),
                   jax.ShapeDtypeStruct((2, b, l // c, heads, 2 * c), F32)],
        scratch_shapes=[pltpu.VMEM((tm + 2 * HALO, d), F32)],
        compiler_params=_mixer_params(),
        name="gdn_in",
    )(x, x, x, gain.reshape(1, d), w_in, conv_w, a_log.reshape(1, nh2), dt_bias.reshape(1, nh2),
      _chunk_cumsum_masks(tm, GD_CHUNK), jnp.eye(nh2, dtype=F32))


GD_GROUP = 8


def _decay(gc_col, gc_row, after_or_same):
    return jnp.exp(jnp.where(after_or_same, gc_col - gc_row, -jnp.inf))


def _spread_exact(cols, sel):
    hi, mid, lo = (jnp.dot(term, sel, preferred_element_type=F32) for term in _bf16_terms(cols))
    return (hi + mid) + lo


def _head_pair_block_diag(left, right):
    zero = jnp.zeros_like(left)
    return jnp.concatenate([jnp.concatenate([left, zero], axis=1),
                            jnp.concatenate([zero, right], axis=1)], axis=0)


def _gdn_prep_kernel(k_ref, cols_ref, pairs_ref, selg_ref, selb_ref, a_ref, *, heads, dk, c, group):
    direction = pl.program_id(0)
    row = lax.broadcasted_iota(jnp.int32, (c, 2 * c), 0)
    col = lax.broadcasted_iota(jnp.int32, (c, 2 * c), 1) % c
    delta = (row - col) * (1 - 2 * direction)
    cols = cols_ref[...]
    g_pairs = _spread_exact(cols, selg_ref[...])
    b_pairs = _spread_exact(cols, selb_ref[...])
    n_pair = heads // 2
    for ci in range(group):
        rows = slice(ci * c, (ci + 1) * c)
        for m in range(n_pair):
            lanes = slice(2 * c * m, 2 * c * (m + 1))
            kp = k_ref[rows, 2 * dk * m:2 * dk * (m + 1)]
            gram = lax.dot_general(kp, _head_pair_block_diag(kp[:, :dk], kp[:, dk:]),
                                   (((1,), (1,)), ((), ())), preferred_element_type=F32)
            dec = jnp.exp(jnp.where(delta > 0, g_pairs[rows, lanes] - pairs_ref[ci, m:m + 1, :], -jnp.inf))
            a_ref[:, ci * n_pair + m, :] = b_pairs[rows, lanes] * gram * dec


def _pair_selectors(heads, c):
    eye = jnp.eye(2 * heads, heads, dtype=BF16)
    return jnp.repeat(eye, c, axis=1), jnp.repeat(jnp.roll(eye, heads, axis=0), c, axis=1)


def _gdn_prep_block(k, cols, pairs):
    b, l, d = k.shape
    heads, c, group = GD_HEADS, GD_CHUNK, GD_GROUP
    rows = c * group
    ng = l // rows
    kern = functools.partial(_gdn_prep_kernel, heads=heads, dk=d // heads, c=c, group=group)
    per = group * heads // 2
    full = lambda a: pl.BlockSpec(a.shape, lambda dr, bb, g: (0,) * a.ndim)
    selg, selb = _pair_selectors(heads, c)
    return pl.pallas_call(
        kern,
        grid=(2, b, ng),
        in_specs=[pl.BlockSpec((None, rows, d), lambda dr, bb, g: (bb, g, 0)),
                  pl.BlockSpec((None, None, rows, 2 * heads), lambda dr, bb, g: (dr, bb, g, 0)),
                  pl.BlockSpec((None, None, group, heads, 2 * c), lambda dr, bb, g: (dr, bb, g, 0, 0)),
                  full(selg), full(selb)],
        out_specs=pl.BlockSpec((None, c, per, 2 * c), lambda dr, bb, g: (dr, 0, bb * ng + g, 0)),
        out_shape=jax.ShapeDtypeStruct((2, c, b * ng * per, 2 * c), F32),
        compiler_params=pltpu.CompilerParams(dimension_semantics=("arbitrary",) * 3,
                                             vmem_limit_bytes=V7X_VMEM_LIMIT_BYTES),
        name="gdn_prep",
    )(k, cols, pairs, selg, selb)


SUBLANES = 8


def _gdn_tsolve_kernel(a_ref, t_ref, a3_ref, t3_ref, *, c, upper):
    p = a_ref.shape[1]
    for i in range(c):
        slab = a_ref[i].T
        a3_ref[0, i] = slab[:c]
        a3_ref[1, i] = slab[c:]
    t3_ref[...] = jnp.zeros_like(t3_ref)
    ntile = c // SUBLANES
    sub = lax.broadcasted_iota(jnp.int32, (SUBLANES, p), 0)
    order = range(c - 1, -1, -1) if upper else range(c)
    for i in order:
        solved = range(i + 1, c) if upper else range(i)
        lo, hi = (i // SUBLANES, ntile) if upper else (0, i // SUBLANES + 1)
        ti = i // SUBLANES
        for half in range(2):
            acc = [jnp.zeros((SUBLANES, p), F32) for _ in range(lo, hi)]
            for j in solved:
                jlo, jhi = (j // SUBLANES, ntile) if upper else (0, j // SUBLANES + 1)
                coef = a3_ref[half, i, j:j + 1, :]
                tj = t3_ref[half, j, jlo * SUBLANES:jhi * SUBLANES, :]
                for t in range(jlo, jhi):
                    acc[t - lo] = acc[t - lo] - coef * tj[(t - jlo) * SUBLANES:(t - jlo + 1) * SUBLANES]
            acc[ti - lo] = acc[ti - lo] + jnp.where(sub == i % SUBLANES, 1.0, 0.0)
            t3_ref[half, i, lo * SUBLANES:hi * SUBLANES, :] = jnp.concatenate(acc, axis=0)
    for ib in range(0, c, SUBLANES):
        slabs = jnp.stack([jnp.concatenate([t3_ref[0, i], t3_ref[1, i]], axis=0).T
                           for i in range(ib, ib + SUBLANES)], axis=0)
        t_ref[:, ib:ib + SUBLANES, :] = jnp.swapaxes(slabs, 0, 1)


def _gdn_tsolve_block(a, direction):
    _, c, pairs, _ = a.shape
    blk = LANES
    upper = bool(direction)
    kern = functools.partial(_gdn_tsolve_kernel, c=c, upper=upper)
    return pl.pallas_call(
        kern,
        grid=(pairs // blk,),
        in_specs=[pl.BlockSpec((None, c, blk, 2 * c), lambda i: (direction, 0, i, 0))],
        out_specs=pl.BlockSpec((blk, c, 2 * c), lambda i: (i, 0, 0)),
        out_shape=jax.ShapeDtypeStruct((pairs, c, 2 * c), F32),
        scratch_shapes=[pltpu.VMEM((2, c, c, blk), F32), pltpu.VMEM((2, c, c, blk), F32)],
        compiler_params=pltpu.CompilerParams(dimension_semantics=("arbitrary",),
                                             vmem_limit_bytes=V7X_VMEM_LIMIT_BYTES),
        name="gdn_tsolve_bwd" if upper else "gdn_tsolve_fwd",
    )(a)


def _gdn_scan_kernel(q_ref, k_ref, v_ref, kt_ref, cols_ref, rows_ref, pairs_ref, t_ref, sel_ref, *rest,
                     heads, dk, c, group, reverse, add_other):
    other_ref = rest[0] if add_other else None
    o_ref, s_ref, g_ref, qe_ref, rhs_ref, attn_ref, u_ref, w_ref = rest[int(add_other):]

    @pl.when(pl.program_id(1) == 0)
    def _():
        s_ref[...] = jnp.zeros_like(s_ref)

    g_all = _spread_exact(cols_ref[...], sel_ref[...])
    g_ref[...] = g_all
    eg = jnp.exp(g_all)
    qe_ref[...] = (q_ref[...] * eg).astype(BF16)
    ke = k_ref[...] * eg
    for hd in range(heads):
        lanes = slice(hd * dk, (hd + 1) * dk)
        rhs_ref[:, 2 * hd * dk:(2 * hd + 1) * dk] = v_ref[:, lanes]
        rhs_ref[:, (2 * hd + 1) * dk:(2 * hd + 2) * dk] = ke[:, lanes].astype(BF16)

    row = lax.broadcasted_iota(jnp.int32, (c, c), 0)
    col = lax.broadcasted_iota(jnp.int32, (c, c), 1)
    incl = (row <= col) if reverse else (row >= col)
    head_lanes = [slice(hd * dk, (hd + 1) * dk) for hd in range(heads)]
    n_pair = heads // 2

    for ci in range(group):
        rows = slice(ci * c, (ci + 1) * c)
        for hd, lanes in enumerate(head_lanes):
            dec = _decay(g_ref[rows, hd * dk:hd * dk + c], rows_ref[ci, hd:hd + 1, :], incl)
            gram = lax.dot_general(q_ref[rows, lanes], k_ref[rows, lanes],
                                   (((1,), (1,)), ((), ())), preferred_element_type=F32)
            attn_ref[ci * heads + hd] = (gram * dec).astype(BF16)
        for m in range(n_pair):
            width = 2 * dk
            t_pair = t_ref[ci * n_pair + m] * pairs_ref[ci, n_pair + m:n_pair + m + 1, :]
            rhs_pair = _head_pair_block_diag(rhs_ref[rows, 2 * m * width:(2 * m + 1) * width],
                                             rhs_ref[rows, (2 * m + 1) * width:(2 * m + 2) * width])
            sol = jnp.dot(t_pair.astype(BF16), rhs_pair, preferred_element_type=F32)
            for par in range(2):
                lanes = head_lanes[2 * m + par]
                u_ref[rows, lanes] = sol[:, par * width:par * width + dk]
                w_ref[rows, lanes] = sol[:, par * width + dk:(par + 1) * width].astype(BF16)

    for step in range(group):
        ci = group - 1 - step if reverse else step
        rows = slice(ci * c, (ci + 1) * c)
        last = ci * c if reverse else (ci + 1) * c - 1
        states = [s_ref[hd] for hd in range(heads)]
        states_bf = [s.astype(BF16) for s in states]
        w_s = [jnp.dot(w_ref[rows, lanes], s, preferred_element_type=F32)
               for lanes, s in zip(head_lanes, states_bf)]
        q_s = [jnp.dot(qe_ref[rows, lanes], s, preferred_element_type=F32)
               for lanes, s in zip(head_lanes, states_bf)]
        for hd, lanes in enumerate(head_lanes):
            v_new_bf = (u_ref[rows, lanes] - w_s[hd]).astype(BF16)
            o = q_s[hd] + jnp.dot(attn_ref[ci * heads + hd], v_new_bf, preferred_element_type=F32)
            o_ref[rows, lanes] = o + other_ref[rows, lanes] if add_other else o
            g_tot = g_ref[last:last + 1, lanes]
            k_dec_t = (kt_ref[ci, lanes, :] * jnp.exp(g_tot[:, :c] - rows_ref[ci, hd:hd + 1, :])).astype(BF16)
            s_ref[hd] = states[hd] * jnp.exp(g_tot) + jnp.dot(k_dec_t, v_new_bf, preferred_element_type=F32)


def _gdn_scan_block(q, k, v, kt, cols, rows_t, pairs, t, reverse, other=None):
    b, l, d = q.shape
    heads, c, group = GD_HEADS, GD_CHUNK, GD_GROUP
    dk = d // heads
    rows = c * group
    ng = l // rows
    per = group * heads
    dr = int(reverse)
    add_other = other is not None
    kern = functools.partial(_gdn_scan_kernel, heads=heads, dk=dk, c=c, group=group, reverse=reverse,
                             add_other=add_other)
    visit = (lambda g: ng - 1 - g) if reverse else (lambda g: g)
    wide = pl.BlockSpec((None, rows, d), lambda bb, g: (bb, visit(g), 0))
    sel = jnp.repeat(jnp.eye(2 * heads, heads, dtype=BF16), dk, axis=1)
    return pl.pallas_call(
        kern,
        grid=(b, ng),
        in_specs=[wide, wide, wide,
                  pl.BlockSpec((None, group, d, c), lambda bb, g: (bb, visit(g), 0, 0)),
                  pl.BlockSpec((None, None, rows, 2 * heads), lambda bb, g: (dr, bb, visit(g), 0)),
                  pl.BlockSpec((None, None, group, 2 * heads, c), lambda bb, g: (dr, bb, visit(g), 0, 0)),
                  pl.BlockSpec((None, None, group, heads, 2 * c), lambda bb, g: (dr, bb, visit(g), 0, 0)),
                  pl.BlockSpec((per // 2, c, 2 * c), lambda bb, g: (bb * ng + visit(g), 0, 0)),
                  pl.BlockSpec((2 * heads, d), lambda bb, g: (0, 0))] + [wide] * add_other,
        out_specs=wide,
        out_shape=jax.ShapeDtypeStruct((b, l, d), F32),
        scratch_shapes=[pltpu.VMEM((heads, dk, dk), F32), pltpu.VMEM((rows, d), F32),
                        pltpu.VMEM((rows, d), BF16), pltpu.VMEM((rows, 2 * d), BF16),
                        pltpu.VMEM((per, c, c), BF16),
                        pltpu.VMEM((rows, d), F32), pltpu.VMEM((rows, d), BF16)],
        compiler_params=_mixer_params(),
        name="gdn_scan_bwd" if reverse else "gdn_scan_fwd",
    )(q, k, v, kt, cols, rows_t, pairs, t, sel, *([other] if add_other else []))


def _gdn_out_kernel(x_ref, o_ref, z_ref, ng_ref, wout_ref, y_ref, g_ref, *, heads, dk):
    o = o_ref[...]
    z = z_ref[...].astype(F32)
    for hd in range(heads):
        lanes = slice(hd * dk, (hd + 1) * dk)
        oh = o[:, lanes]
        ms = jnp.mean(oh * oh, axis=-1, keepdims=True)
        zh = z[:, lanes]
        g_ref[:, lanes] = (oh * lax.rsqrt(ms + NORM_EPS) * ng_ref[...] * (zh * jax.nn.sigmoid(zh))).astype(BF16)
    y_ref[...] = x_ref[...] + jnp.dot(g_ref[...], wout_ref[...].astype(BF16), preferred_element_type=F32)


def _gdn_out_block(x, o, z, norm_g, w_out, lead):
    b, l, d = x.shape
    heads = GD_HEADS
    dk = d // heads
    tm = GD_ROW_TILE
    row = pl.BlockSpec((None, tm, d), lambda bb, i: (bb, i, 0))
    kern = functools.partial(_gdn_out_kernel, heads=heads, dk=dk)
    return pl.pallas_call(
        kern,
        grid=(b, l // tm),
        in_specs=[row, row, row,
                  pl.BlockSpec((1, dk), lambda bb, i: (0, 0)),
                  _stacked_weight_spec(w_out, lead)],
        out_specs=row,
        out_shape=jax.ShapeDtypeStruct((b, l, d), F32),
        scratch_shapes=[pltpu.VMEM((tm, d), BF16)],
        compiler_params=_mixer_params(),
        name="gdn_out",
    )(x, o, z, norm_g.reshape(1, dk), w_out)


def _gdn_block(x, gain, w_in, lead, conv_w, a_log, dt_bias, norm_g, w_out):
    q, k, v, z, kt, cols, rows_t, pairs = _gdn_in_block(x, gain, w_in, lead, conv_w, a_log, dt_bias)
    a = _gdn_prep_block(k, cols, pairs)
    o = None
    for dr in range(2):
        o = _gdn_scan_block(q, k, v, kt, cols, rows_t, pairs, _gdn_tsolve_block(a, dr), bool(dr), other=o)
    return _gdn_out_block(x, o, z, norm_g, w_out, lead)


def kernel(x, norms, final_norm, ffn_w_in, ffn_w_out, sc_w_in, sc_conv, sc_w_out, hy_w_in, hy_b_in, hy_conv, hy_conv_b, hy_f_w1, hy_f_b1, hy_f_w2, hy_f_b2, hy_f_w3, hy_f_freq, hy_d, hy_w_out, hy_b_out, gd_w_in, gd_conv, gd_a_log, gd_dt_bias, gd_norm, gd_w_out):
    b, l, d = x.shape
    depth = norms.shape[0]

    def ffn(xx, i, k):
        final = final_norm if (i == depth - 1 and k == 1) else None
        return _ffn_block(xx.reshape(b * l, d), norms[i, 2 * k], ffn_w_in, ffn_w_out, (i, k),
                          final).reshape(b, l, d)

    for i in range(depth):
        m, j = i % N_MIXERS, i // N_MIXERS
        x = ffn(x, i, 0)
        if m == 0:
            x = _sc_mixer_block(x, norms[i, 1], sc_w_in, sc_conv[j], sc_w_out, (j,))
        elif m == 1:
            x = _hyena_block(x, norms[i, 1], (j,), hy_w_in, hy_b_in[j], hy_conv[j], hy_conv_b[j],
                             hy_f_w1[j], hy_f_b1[j], hy_f_w2[j], hy_f_b2[j], hy_f_w3[j],
                             hy_f_freq[j], hy_d[j], hy_w_out, hy_b_out[j])
        else:
            x = _gdn_block(x, norms[i, 1], gd_w_in, (j,), gd_conv[j], gd_a_log[j], gd_dt_bias[j],
                           gd_norm[j], gd_w_out)
        x = ffn(x, i, 1)
    return x
```

```python
import functools
import math

import jax
import jax.numpy as jnp
from jax import lax
from jax.experimental import pallas as pl
from jax.experimental.pallas import tpu as pltpu

F32 = jnp.float32
BF16 = jnp.bfloat16

NORM_EPS = 1e-6
N_MIXERS = 3
HY_EMB = 33
HY_BANDS = (HY_EMB - 1) // 2
HY_MAX_DECAY = math.log(1e-2) / 0.3
HY_MIN_DECAY = math.log(1e-2) / 1.5
GD_HEADS = 8
GD_CHUNK = 64

V7X_VMEM_LIMIT_BYTES = 56 * 1024 * 1024
FFN_ROW_TILE = 512
FFN_HIDDEN_TILE = 256


def _ffn_kernel(x_ref, gain_ref, win_ref, wout_ref, *rest, n_hidden_tiles, hidden_tile, final_norm):
    final_ref, o_ref, hid_ref = rest if final_norm else (None,) + rest
    x = x_ref[...]
    xn = _rms_rows(x, gain_ref[...]).astype(BF16)
    f = n_hidden_tiles * hidden_tile
    for j in range(n_hidden_tiles):
        cols = slice(j * hidden_tile, (j + 1) * hidden_tile)
        up_cols = slice(f + j * hidden_tile, f + (j + 1) * hidden_tile)
        g = jnp.dot(xn, win_ref[:, cols].astype(BF16), preferred_element_type=F32)
        u = jnp.dot(xn, win_ref[:, up_cols].astype(BF16), preferred_element_type=F32)
        hid_ref[:, cols] = (g * jax.nn.sigmoid(g) * u).astype(BF16)
    y = x + 0.5 * jnp.dot(hid_ref[...], wout_ref[...].astype(BF16), preferred_element_type=F32)
    o_ref[...] = _rms_rows(y, final_ref[...]) if final_norm else y


def _stacked_weight_spec(w, lead):
    tail = w.shape[len(lead):]
    index = tuple(lead) + (0,) * len(tail)
    return pl.BlockSpec((None,) * len(lead) + tail, lambda *_: index, **_RESIDENT)


def _ffn_block(x2d, gain, w_in, w_out, lead, final_gain=None):
    t, d = x2d.shape
    f = w_out.shape[-2]
    tm = FFN_ROW_TILE
    final_norm = final_gain is not None
    kern = functools.partial(_ffn_kernel, n_hidden_tiles=f // FFN_HIDDEN_TILE, hidden_tile=FFN_HIDDEN_TILE,
                             final_norm=final_norm)
    vec = pl.BlockSpec((1, d), lambda i: (0, 0))
    extra = ([vec], [final_gain.reshape(1, d)]) if final_norm else ([], [])
    return pl.pallas_call(
        kern,
        grid=(t // tm,),
        in_specs=[
            pl.BlockSpec((tm, d), lambda i: (i, 0)),
            vec,
            _stacked_weight_spec(w_in, lead),
            _stacked_weight_spec(w_out, lead),
        ] + extra[0],
        out_specs=pl.BlockSpec((tm, d), lambda i: (i, 0)),
        out_shape=jax.ShapeDtypeStruct((t, d), F32),
        scratch_shapes=[pltpu.VMEM((tm, f), BF16)],
        compiler_params=pltpu.CompilerParams(
            dimension_semantics=("arbitrary",), vmem_limit_bytes=V7X_VMEM_LIMIT_BYTES),
        name="ffn_final" if final_norm else "ffn_block",
    )(x2d, gain.reshape(1, d), w_in, w_out, *extra[1])


MIX_ROW_TILE = 1024
GD_ROW_TILE = 512
HALO = 8


def _rms_rows(x, gain):
    ms = jnp.mean(x * x, axis=-1, keepdims=True)
    return x * lax.rsqrt(ms + NORM_EPS) * gain


def _bf16_terms(x):
    hi = x.astype(BF16)
    rest = x - hi.astype(F32)
    mid = rest.astype(BF16)
    lo = (rest - mid.astype(F32)).astype(BF16)
    return hi, mid, lo


def _halo_specs(tm, d, seq_len):
    per = tm // HALO
    last_blk = seq_len // HALO - 1
    return [
        pl.BlockSpec((None, HALO, d), lambda b, i: (b, jnp.maximum(i * per - 1, 0), 0)),
        pl.BlockSpec((None, tm, d), lambda b, i: (b, i, 0)),
        pl.BlockSpec((None, HALO, d), lambda b, i: (b, jnp.minimum((i + 1) * per, last_blk), 0)),
    ]


def _store_with_halo(s_ref, main, halo, tm):
    i = pl.program_id(1)
    last = pl.num_programs(1) - 1
    s_ref[0:HALO, :] = jnp.where(i > 0, halo[:HALO], 0.0)
    s_ref[HALO:HALO + tm, :] = main
    s_ref[HALO + tm:2 * HALO + tm, :] = jnp.where(i < last, halo[HALO:], 0.0)


def _conv3_from(s_ref, cw, tm):
    return (cw[0:1] * s_ref[HALO - 1:HALO - 1 + tm, :] + cw[1:2] * s_ref[HALO:HALO + tm, :]
            + cw[2:3] * s_ref[HALO + 1:HALO + 1 + tm, :])


_RESIDENT = dict(pipeline_mode=pl.Buffered(1))


def _mixer_params():
    return pltpu.CompilerParams(dimension_semantics=("arbitrary", "arbitrary"),
                                vmem_limit_bytes=V7X_VMEM_LIMIT_BYTES)


def _sc_mixer_kernel(xp_ref, x_ref, xn_ref, gain_ref, win_ref, cw_ref, wout_ref, o_ref, ch_ref, *, tm, d):
    gain = gain_ref[...]
    x = x_ref[...]
    h = _rms_rows(x, gain).astype(BF16)
    w_c = win_ref[:, d:2 * d].astype(BF16)
    w_h = win_ref[:, 2 * d:].astype(BF16)
    c_main = jnp.dot(h, w_c, preferred_element_type=F32)
    h_main = jnp.dot(h, w_h, preferred_element_type=F32)
    hh = _rms_rows(jnp.concatenate([xp_ref[...], xn_ref[...]], axis=0), gain).astype(BF16)
    c_halo = jnp.dot(hh, w_c, preferred_element_type=F32)
    h_halo = jnp.dot(hh, w_h, preferred_element_type=F32)
    _store_with_halo(ch_ref, c_main * h_main, c_halo * h_halo, tm)
    conv = _conv3_from(ch_ref, cw_ref[...], tm)
    b_main = jnp.dot(h, win_ref[:, :d].astype(BF16), preferred_element_type=F32)
    y = jnp.dot((b_main * conv).astype(BF16), wout_ref[...].astype(BF16), preferred_element_type=F32)
    o_ref[...] = x + y


def _sc_mixer_block(x, gain, w_in, conv_w, w_out, lead):
    b, l, d = x.shape
    tm = MIX_ROW_TILE
    kern = functools.partial(_sc_mixer_kernel, tm=tm, d=d)
    return pl.pallas_call(
        kern,
        grid=(b, l // tm),
        in_specs=_halo_specs(tm, d, l) + [
            pl.BlockSpec((1, d), lambda bb, i: (0, 0)),
            _stacked_weight_spec(w_in, lead),
            pl.BlockSpec((3, d), lambda bb, i: (0, 0)),
            _stacked_weight_spec(w_out, lead),
        ],
        out_specs=pl.BlockSpec((None, tm, d), lambda bb, i: (bb, i, 0)),
        out_shape=jax.ShapeDtypeStruct((b, l, d), F32),
        scratch_shapes=[pltpu.VMEM((tm + 2 * HALO, d), F32)],
        compiler_params=_mixer_params(),
        name="sc_mixer",
    )(x, x, x, gain.reshape(1, d), w_in, conv_w, w_out)


def _hy_in_kernel(xp_ref, x_ref, xn_ref, gain_ref, win_ref, bin_ref, cw_ref, cb_ref, x0_ref, vxt_ref, s_ref,
                  *, tm, d):
    gain = gain_ref[...]
    h = _rms_rows(x_ref[...], gain).astype(BF16)
    hh = _rms_rows(jnp.concatenate([xp_ref[...], xn_ref[...]], axis=0), gain).astype(BF16)
    parts = []
    for j in range(3):
        cols = slice(j * d, (j + 1) * d)
        bias = bin_ref[:, cols]
        w_cols = win_ref[:, cols].astype(BF16)
        main = jnp.dot(h, w_cols, preferred_element_type=F32) + bias
        halo = jnp.dot(hh, w_cols, preferred_element_type=F32) + bias
        _store_with_halo(s_ref, main, halo, tm)
        parts.append(_conv3_from(s_ref, cw_ref[:, cols], tm) + cb_ref[:, cols])
    x0_ref[...] = parts[0]
    vxt_ref[...] = (parts[2] * parts[1]).T


def _hy_in_block(x, gain, w_in, lead, b_in, conv_w, conv_b):
    b, l, d = x.shape
    tm = MIX_ROW_TILE
    kern = functools.partial(_hy_in_kernel, tm=tm, d=d)
    return pl.pallas_call(
        kern,
        grid=(b, l // tm),
        in_specs=_halo_specs(tm, d, l) + [
            pl.BlockSpec((1, d), lambda bb, i: (0, 0)),
            _stacked_weight_spec(w_in, lead),
            pl.BlockSpec((1, 3 * d), lambda bb, i: (0, 0)),
            pl.BlockSpec((3, 3 * d), lambda bb, i: (0, 0)),
            pl.BlockSpec((1, 3 * d), lambda bb, i: (0, 0)),
        ],
        out_specs=[pl.BlockSpec((None, tm, d), lambda bb, i: (bb, i, 0)),
                   pl.BlockSpec((None, d, tm), lambda bb, i: (bb, 0, i))],
        out_shape=[jax.ShapeDtypeStruct((b, l, d), F32), jax.ShapeDtypeStruct((b, d, l), F32)],
        scratch_shapes=[pltpu.VMEM((tm + 2 * HALO, d), F32)],
        compiler_params=_mixer_params(),
        name="hyena_in",
    )(x, x, x, gain.reshape(1, d), w_in, b_in.reshape(1, 3 * d), conv_w, conv_b.reshape(1, 3 * d))


HY_FEAT_PAD = 64
HY_FILTER_TILE = 2048


def _hy_filter_kernel(zt_ref, w1_ref, b1_ref, w2_ref, b2_ref, freq_ref, w3_ref, delta_ref, hc_ref):
    def t_dot(w, a):
        return lax.dot_general(w, a, (((0,), (0,)), ((), ())), precision=lax.Precision.HIGHEST,
                               preferred_element_type=F32)

    z = zt_ref[...]
    freq = freq_ref[...]
    h1 = jnp.sin(freq * (t_dot(w1_ref[...], z) + b1_ref[...]))
    h2 = jnp.sin(freq * (t_dot(w2_ref[...], h1) + b2_ref[...]))
    h = t_dot(w3_ref[...], h2)
    t_row = z[0:1, :]
    mask_row = z[HY_EMB:HY_EMB + 1, :]
    hc_ref[...] = h * jnp.exp(-t_row * delta_ref[...]) * mask_row


def _hy_filter_features(l):
    p = jnp.arange(2 * l)
    pos = jnp.where(p < l, p, 2 * l - p)
    valid = (p != l).astype(F32)
    pos = jnp.minimum(pos, l - 1).astype(F32)[None, :]
    t = pos / (l - 1)
    w = 2.0 * math.pi * pos / l
    f = jnp.linspace(1e-4, HY_BANDS - 1, HY_BANDS, dtype=F32)[:, None]
    z = jnp.concatenate([t, jnp.cos(f * w), -jnp.sin(f * w), valid[None, :]], axis=0)
    return jnp.pad(z, ((0, HY_FEAT_PAD - z.shape[0]), (0, 0)))


def _hy_filter_block(l, w1, b1, w2, b2, w3, freq):
    order = w1.shape[1]
    d = w3.shape[1] // 2
    tl = min(HY_FILTER_TILE, l)
    half = l // tl
    zt = _hy_filter_features(l)
    w1p = jnp.pad(w1, ((0, HY_FEAT_PAD - w1.shape[0]), (0, 0)))
    delta = jnp.abs(jnp.linspace(HY_MIN_DECAY, HY_MAX_DECAY, d, dtype=F32)).reshape(d, 1)
    col = lambda v: v.reshape(order, 1)
    full = lambda shape: pl.BlockSpec(shape, lambda i: (0,) * len(shape))
    return pl.pallas_call(
        _hy_filter_kernel,
        grid=(2 * half,),
        in_specs=[pl.BlockSpec((HY_FEAT_PAD, tl), lambda i: (0, i)),
                  full((HY_FEAT_PAD, order)), full((order, 1)), full((order, order)), full((order, 1)),
                  full((order, 1)),
                  pl.BlockSpec((order, d), lambda i: (0, i // half)),
                  full((d, 1))],
        out_specs=pl.BlockSpec((d, tl), lambda i: (0, i)),
        out_shape=jax.ShapeDtypeStruct((d, 2 * l), F32),
        compiler_params=pltpu.CompilerParams(dimension_semantics=("arbitrary",),
                                             vmem_limit_bytes=V7X_VMEM_LIMIT_BYTES),
        name="hyena_filter",
    )(zt, w1p, col(b1), w2, col(b2), col(freq), w3, delta)


HY_CONV_CH_TILE = 32


def _dft_tables(n1, n2):
    import numpy as np
    n = n1 * n2
    i1, i2 = np.arange(n1), np.arange(n2)
    a1 = 2.0 * np.pi * np.outer(i1, i1) / n1
    a2 = 2.0 * np.pi * np.outer(i2, i2) / n2
    c1, s1, c2, s2 = np.cos(a1), np.sin(a1), np.cos(a2), np.sin(a2)
    tw = 2.0 * np.pi * np.outer(i2, i1) / n
    f_real = np.concatenate([c1, -s1], axis=1)
    f_cplx = np.block([[c2, -s2], [s2, c2]])
    i_cplx = np.block([[c2, s2], [-s2, c2]])
    i_real = np.concatenate([c1, -s1], axis=0) / n
    as_bf = lambda a: jnp.asarray(a, dtype=F32).astype(BF16)
    as_f = lambda a: jnp.asarray(a, dtype=F32)
    return (as_bf(f_real), as_bf(f_cplx), as_bf(i_cplx), as_bf(i_real),
            as_f(np.cos(tw)), as_f(np.sin(tw)), as_f(np.cos(tw.T)), as_f(np.sin(tw.T)))


def _swap(a):
    return jnp.swapaxes(a, 1, 2)


def _hy_conv_kernel(ut_ref, hc_ref, dbias_ref, freal_ref, fcplx_ref, icplx_ref, ireal_ref, twc_ref, tws_ref,
                    twct_ref, twst_ref, yt_ref, hspec_ref, *, ct, n1, n2):
    rows = ct * n2
    twc = twc_ref[...]
    tws = tws_ref[...]

    def mm(a, m_ref):
        return jnp.dot(a.astype(BF16), m_ref[...], preferred_element_type=F32)

    def forward(x):
        p = mm(_swap(x).reshape(rows, n1), freal_ref)
        pr = p[:, :n1].reshape(ct, n2, n1)
        pi = p[:, n1:].reshape(ct, n2, n1)
        qr = pr * twc + pi * tws
        qi = pi * twc - pr * tws
        q = jnp.concatenate([_swap(qr).reshape(ct * n1, n2), _swap(qi).reshape(ct * n1, n2)], axis=1)
        xs = mm(q, fcplx_ref)
        return xs[:, :n2].reshape(ct, n1, n2), xs[:, n2:].reshape(ct, n1, n2)

    @pl.when(pl.program_id(1) == 0)
    def _():
        hr, hi = forward(hc_ref[...])
        hspec_ref[0] = hr
        hspec_ref[1] = hi

    u = ut_ref[...]
    xr, xi = forward(jnp.concatenate([u, jnp.zeros_like(u)], axis=1))
    hr = hspec_ref[0]
    hi = hspec_ref[1]
    yr = xr * hr - xi * hi
    yi = xr * hi + xi * hr
    r = mm(jnp.concatenate([yr.reshape(ct * n1, n2), yi.reshape(ct * n1, n2)], axis=1), icplx_ref)
    rr = r[:, :n2].reshape(ct, n1, n2)
    ri = r[:, n2:].reshape(ct, n1, n2)
    twct = twct_ref[...]
    twst = twst_ref[...]
    sr = rr * twct - ri * twst
    si = ri * twct + rr * twst
    s = jnp.concatenate([_swap(sr).reshape(rows, n1), _swap(si).reshape(rows, n1)], axis=1)
    z = mm(s, ireal_ref).reshape(ct, n2, n1)
    y = _swap(z)[:, :n1 // 2, :]
    yt_ref[...] = y + u * dbias_ref[...]


def _hy_conv_block(ut, hc, d_bias):
    b, d, l = ut.shape
    n2 = 128
    n1 = 2 * l // n2
    ct = HY_CONV_CH_TILE
    tables = _dft_tables(n1, n2)
    kern = functools.partial(_hy_conv_kernel, ct=ct, n1=n1, n2=n2)
    full = lambda a: pl.BlockSpec(a.shape, lambda c, bb: (0,) * a.ndim)
    yt = pl.pallas_call(
        kern,
        grid=(d // ct, b),
        in_specs=[pl.BlockSpec((None, ct, n1 // 2, n2), lambda c, bb: (bb, c, 0, 0)),
                  pl.BlockSpec((ct, n1, n2), lambda c, bb: (c, 0, 0)),
                  pl.BlockSpec((ct, 1, 1), lambda c, bb: (c, 0, 0))] + [full(t) for t in tables],
        out_specs=pl.BlockSpec((None, ct, n1 // 2, n2), lambda c, bb: (bb, c, 0, 0)),
        out_shape=jax.ShapeDtypeStruct((b, d, n1 // 2, n2), F32),
        scratch_shapes=[pltpu.VMEM((2, ct, n1, n2), F32)],
        compiler_params=_mixer_params(),
        name="hyena_longconv",
    )(ut.reshape(b, d, n1 // 2, n2), hc.reshape(d, n1, n2), d_bias.reshape(d, 1, 1), *tables)
    return yt


def _hyena_block(x, gain, lead, w_in, b_in, conv_w, conv_b, f_w1, f_b1, f_w2, f_b2, f_w3, f_freq, d_bias, w_out, b_out):
    l = x.shape[1]
    x0, ut = _hy_in_block(x, gain, w_in, lead, b_in, conv_w, conv_b)
    hc = _hy_filter_block(l, f_w1, f_b1, f_w2, f_b2, f_w3, f_freq)
    yt = _hy_conv_block(ut, hc, d_bias)
    return _hy_out_block(x, x0, yt, w_out, lead, b_out)


def _hy_out_kernel(x_ref, x0_ref, yt_ref, wout_ref, bout_ref, o_ref):
    slabs = jnp.swapaxes(yt_ref[...], 0, 1)
    y = jnp.concatenate([slabs[j].T for j in range(slabs.shape[0])], axis=0)
    z = jnp.dot((y * x0_ref[...]).astype(BF16), wout_ref[...].astype(BF16), preferred_element_type=F32)
    o_ref[...] = x_ref[...] + z + bout_ref[...]


def _hy_out_block(x, x0, yt, w_out, lead, b_out):
    b, l, d = x.shape
    tm = MIX_ROW_TILE
    slab = yt.shape[-1]
    row = pl.BlockSpec((None, tm, d), lambda bb, i: (bb, i, 0))
    return pl.pallas_call(
        _hy_out_kernel,
        grid=(b, l // tm),
        in_specs=[row, row, pl.BlockSpec((None, d, tm // slab, slab), lambda bb, i: (bb, 0, i, 0)),
                  _stacked_weight_spec(w_out, lead),
                  pl.BlockSpec((1, d), lambda bb, i: (0, 0))],
        out_specs=row,
        out_shape=jax.ShapeDtypeStruct((b, l, d), F32),
        compiler_params=_mixer_params(),
        name="hyena_out",
    )(x, x0, yt, w_out, b_out.reshape(1, d))


LANES = 128


def _softplus(x):
    return jnp.maximum(x, 0.0) + jnp.log1p(jnp.exp(-jnp.abs(x)))


def _gdn_in_kernel(xp_ref, x_ref, xn_ref, gain_ref, win_ref, cw_ref, alog_ref, dtb_ref, cum_ref, eye_ref,
                   q_ref, k_ref, v_ref, z_ref, kt_ref, cols_ref, rows_ref, pairs_ref, s_ref, *, tm, d, heads):
    hi = lax.Precision.HIGHEST
    gain = gain_ref[...]
    h = _rms_rows(x_ref[...], gain).astype(BF16)
    hh = _rms_rows(jnp.concatenate([xp_ref[...], xn_ref[...]], axis=0), gain).astype(BF16)
    dk = d // heads
    for j, (out_ref, scale) in enumerate(((q_ref, dk ** -0.5), (k_ref, 1.0), (v_ref, None))):
        cols = slice(j * d, (j + 1) * d)
        w_cols = win_ref[:, cols].astype(BF16)
        main = jnp.dot(h, w_cols, preferred_element_type=F32)
        halo = jnp.dot(hh, w_cols, preferred_element_type=F32)
        _store_with_halo(s_ref, main, halo, tm)
        c = _conv3_from(s_ref, cw_ref[:, cols], tm)
        c = c * jax.nn.sigmoid(c)
        if scale is None:
            out_ref[...] = c.astype(BF16)
        else:
            normed = []
            for hd in range(heads):
                lanes = slice(hd * dk, (hd + 1) * dk)
                ch = c[:, lanes]
                ss = jnp.sum(ch * ch, axis=-1, keepdims=True)
                normed.append(ch * (lax.rsqrt(ss + 1e-6) * scale))
                out_ref[:, lanes] = normed[-1].astype(BF16)
            if j == 1:
                kt = jnp.concatenate(normed, axis=1).T
                n_chunk, _, c = kt_ref.shape
                for ci in range(n_chunk):
                    kt_ref[ci] = kt[:, ci * c:(ci + 1) * c]
    z_ref[...] = jnp.dot(h, win_ref[:, 3 * d:4 * d].astype(BF16), preferred_element_type=F32).astype(BF16)
    ab = jnp.dot(h, win_ref[:, 4 * d:].astype(BF16), preferred_element_type=F32)
    nh2 = 2 * heads
    g = -jnp.exp(alog_ref[...]) * _softplus(ab[:, :nh2] + dtb_ref[...])
    beta = jax.nn.sigmoid(ab[:, nh2:2 * nh2])
    g_terms = _bf16_terms(g)
    cumsum = lambda mask: sum(jnp.dot(mask, term, preferred_element_type=F32) for term in g_terms)
    gc_f = cumsum(cum_ref[0])
    gc_b = cumsum(cum_ref[1])
    chunk = tm // rows_ref.shape[1]
    for dr, gc in enumerate((gc_f, gc_b)):
        own = slice(dr * heads, (dr + 1) * heads)
        cols = jnp.concatenate([gc[:, own], beta[:, own]], axis=1)
        cols_ref[dr] = cols
        rows = lax.dot_general(eye_ref[...], cols, (((1,), (1,)), ((), ())), precision=hi,
                               preferred_element_type=F32)
        for ci in range(tm // chunk):
            chunk_rows = rows[:, ci * chunk:(ci + 1) * chunk]
            rows_ref[dr, ci] = chunk_rows
            pairs_ref[dr, ci] = jnp.concatenate(
                [jnp.concatenate([chunk_rows[r:r + 1], chunk_rows[r + 1:r + 2]], axis=1)
                 for r in range(0, nh2, 2)], axis=0)


def _chunk_cumsum_masks(tm, chunk):
    import numpy as np
    r = np.arange(tm)
    same = (r[:, None] // chunk) == (r[None, :] // chunk)
    lower = same & (r[None, :] <= r[:, None])
    upper = same & (r[None, :] >= r[:, None])
    return jnp.asarray(np.stack([lower, upper]).astype(np.float32)).astype(BF16)


def _gdn_in_block(x, gain, w_in, lead, conv_w, a_log, dt_bias):
    b, l, d = x.shape
    heads = GD_HEADS
    nh2 = 2 * heads
    tm = GD_ROW_TILE
    kern = functools.partial(_gdn_in_kernel, tm=tm, d=d, heads=heads)
    row = pl.BlockSpec((None, tm, d), lambda bb, i: (bb, i, 0))
    wide = jax.ShapeDtypeStruct((b, l, d), BF16)
    c = GD_CHUNK
    return pl.pallas_call(
        kern,
        grid=(b, l // tm),
        in_specs=_halo_specs(tm, d, l) + [
            pl.BlockSpec((1, d), lambda bb, i: (0, 0)),
            _stacked_weight_spec(w_in, lead),
            pl.BlockSpec((3, 3 * d), lambda bb, i: (0, 0)),
            pl.BlockSpec((1, nh2), lambda bb, i: (0, 0)),
            pl.BlockSpec((1, nh2), lambda bb, i: (0, 0)),
            pl.BlockSpec((2, tm, tm), lambda bb, i: (0, 0, 0), **_RESIDENT),
            pl.BlockSpec((nh2, nh2), lambda bb, i: (0, 0)),
        ],
        out_specs=[row, row, row, row,
                   pl.BlockSpec((None, tm // c, d, c), lambda bb, i: (bb, i, 0, 0)),
                   pl.BlockSpec((2, None, tm, nh2), lambda bb, i: (0, bb, i, 0)),
                   pl.BlockSpec((2, None, tm // c, nh2, c), lambda bb, i: (0, bb, i, 0, 0)),
                   pl.BlockSpec((2, None, tm // c, heads, 2 * c), lambda bb, i: (0, bb, i, 0, 0))],
        out_shape=[wide, wide, wide, wide, jax.ShapeDtypeStruct((b, l // c, d, c), F32),
                   jax.ShapeDtypeStruct((2, b, l, nh2), F32),
                   jax.ShapeDtypeStruct((2, b, l // c, nh2, c), F32),
                   jax.ShapeDtypeStruct((2, b, l // c, heads, 2 * c), F32)],
        scratch_shapes=[pltpu.VMEM((tm + 2 * HALO, d), F32)],
        compiler_params=_mixer_params(),
        name="gdn_in",
    )(x, x, x, gain.reshape(1, d), w_in, conv_w, a_log.reshape(1, nh2), dt_bias.reshape(1, nh2),
      _chunk_cumsum_masks(tm, GD_CHUNK), jnp.eye(nh2, dtype=F32))


GD_GROUP = 8


def _decay(gc_col, gc_row, after_or_same):
    return jnp.exp(jnp.where(after_or_same, gc_col - gc_row, -jnp.inf))


def _spread_exact(cols, sel):
    hi, mid, lo = (jnp.dot(term, sel, preferred_element_type=F32) for term in _bf16_terms(cols))
    return (hi + mid) + lo


def _head_pair_block_diag(left, right):
    zero = jnp.zeros_like(left)
    return jnp.concatenate([jnp.concatenate([left, zero], axis=1),
                            jnp.concatenate([zero, right], axis=1)], axis=0)


def _gdn_prep_kernel(k_ref, cols_ref, pairs_ref, selg_ref, selb_ref, a_ref, *, heads, dk, c, group):
    direction = pl.program_id(0)
    row = lax.broadcasted_iota(jnp.int32, (c, 2 * c), 0)
    col = lax.broadcasted_iota(jnp.int32, (c, 2 * c), 1) % c
    delta = (row - col) * (1 - 2 * direction)
    cols = cols_ref[...]
    g_pairs = _spread_exact(cols, selg_ref[...])
    b_pairs = _spread_exact(cols, selb_ref[...])
    n_pair = heads // 2
    for ci in range(group):
        rows = slice(ci * c, (ci + 1) * c)
        for m in range(n_pair):
            lanes = slice(2 * c * m, 2 * c * (m + 1))
            kp = k_ref[rows, 2 * dk * m:2 * dk * (m + 1)]
            gram = lax.dot_general(kp, _head_pair_block_diag(kp[:, :dk], kp[:, dk:]),
                                   (((1,), (1,)), ((), ())), preferred_element_type=F32)
            dec = jnp.exp(jnp.where(delta > 0, g_pairs[rows, lanes] - pairs_ref[ci, m:m + 1, :], -jnp.inf))
            a_ref[:, ci * n_pair + m, :] = b_pairs[rows, lanes] * gram * dec


def _pair_selectors(heads, c):
    eye = jnp.eye(2 * heads, heads, dtype=BF16)
    return jnp.repeat(eye, c, axis=1), jnp.repeat(jnp.roll(eye, heads, axis=0), c, axis=1)


def _gdn_prep_block(k, cols, pairs):
    b, l, d = k.shape
    heads, c, group = GD_HEADS, GD_CHUNK, GD_GROUP
    rows = c * group
    ng = l // rows
    kern = functools.partial(_gdn_prep_kernel, heads=heads, dk=d // heads, c=c, group=group)
    per = group * heads // 2
    full = lambda a: pl.BlockSpec(a.shape, lambda dr, bb, g: (0,) * a.ndim)
    selg, selb = _pair_selectors(heads, c)
    return pl.pallas_call(
        kern,
        grid=(2, b, ng),
        in_specs=[pl.BlockSpec((None, rows, d), lambda dr, bb, g: (bb, g, 0)),
                  pl.BlockSpec((None, None, rows, 2 * heads), lambda dr, bb, g: (dr, bb, g, 0)),
                  pl.BlockSpec((None, None, group, heads, 2 * c), lambda dr, bb, g: (dr, bb, g, 0, 0)),
                  full(selg), full(selb)],
        out_specs=pl.BlockSpec((None, c, per, 2 * c), lambda dr, bb, g: (dr, 0, bb * ng + g, 0)),
        out_shape=jax.ShapeDtypeStruct((2, c, b * ng * per, 2 * c), F32),
        compiler_params=pltpu.CompilerParams(dimension_semantics=("arbitrary",) * 3,
                                             vmem_limit_bytes=V7X_VMEM_LIMIT_BYTES),
        name="gdn_prep",
    )(k, cols, pairs, selg, selb)


SUBLANES = 8


def _gdn_tsolve_kernel(a_ref, t_ref, a3_ref, t3_ref, *, c, upper):
    p = a_ref.shape[1]
    for i in range(c):
        slab = a_ref[i].T
        a3_ref[0, i] = slab[:c]
        a3_ref[1, i] = slab[c:]
    t3_ref[...] = jnp.zeros_like(t3_ref)
    ntile = c // SUBLANES
    sub = lax.broadcasted_iota(jnp.int32, (SUBLANES, p), 0)
    order = range(c - 1, -1, -1) if upper else range(c)
    for i in order:
        solved = range(i + 1, c) if upper else range(i)
        lo, hi = (i // SUBLANES, ntile) if upper else (0, i // SUBLANES + 1)
        ti = i // SUBLANES
        for half in range(2):
            acc = [jnp.zeros((SUBLANES, p), F32) for _ in range(lo, hi)]
            for j in solved:
                jlo, jhi = (j // SUBLANES, ntile) if upper else (0, j // SUBLANES + 1)
                coef = a3_ref[half, i, j:j + 1, :]
                tj = t3_ref[half, j, jlo * SUBLANES:jhi * SUBLANES, :]
                for t in range(jlo, jhi):
                    acc[t - lo] = acc[t - lo] - coef * tj[(t - jlo) * SUBLANES:(t - jlo + 1) * SUBLANES]
            acc[ti - lo] = acc[ti - lo] + jnp.where(sub == i % SUBLANES, 1.0, 0.0)
            t3_ref[half, i, lo * SUBLANES:hi * SUBLANES, :] = jnp.concatenate(acc, axis=0)
    for ib in range(0, c, SUBLANES):
        slabs = jnp.stack([jnp.concatenate([t3_ref[0, i], t3_ref[1, i]], axis=0).T
                           for i in range(ib, ib + SUBLANES)], axis=0)
        t_ref[:, ib:ib + SUBLANES, :] = jnp.swapaxes(slabs, 0, 1)


def _gdn_tsolve_block(a, direction):
    _, c, pairs, _ = a.shape
    blk = LANES
    upper = bool(direction)
    kern = functools.partial(_gdn_tsolve_kernel, c=c, upper=upper)
    return pl.pallas_call(
        kern,
        grid=(pairs // blk,),
        in_specs=[pl.BlockSpec((None, c, blk, 2 * c), lambda i: (direction, 0, i, 0))],
        out_specs=pl.BlockSpec((blk, c, 2 * c), lambda i: (i, 0, 0)),
        out_shape=jax.ShapeDtypeStruct((pairs, c, 2 * c), F32),
        scratch_shapes=[pltpu.VMEM((2, c, c, blk), F32), pltpu.VMEM((2, c, c, blk), F32)],
        compiler_params=pltpu.CompilerParams(dimension_semantics=("arbitrary",),
                                             vmem_limit_bytes=V7X_VMEM_LIMIT_BYTES),
        name="gdn_tsolve_bwd" if upper else "gdn_tsolve_fwd",
    )(a)


def _gdn_scan_kernel(q_ref, k_ref, v_ref, kt_ref, cols_ref, rows_ref, pairs_ref, t_ref, sel_ref, *rest,
                     heads, dk, c, group, reverse, add_other):
    other_ref = rest[0] if add_other else None
    o_ref, s_ref, g_ref, qe_ref, rhs_ref, attn_ref, u_ref, w_ref = rest[int(add_other):]

    @pl.when(pl.program_id(1) == 0)
    def _():
        s_ref[...] = jnp.zeros_like(s_ref)

    g_all = _spread_exact(cols_ref[...], sel_ref[...])
    g_ref[...] = g_all
    eg = jnp.exp(g_all)
    qe_ref[...] = (q_ref[...] * eg).astype(BF16)
    ke = k_ref[...] * eg
    for hd in range(heads):
        lanes = slice(hd * dk, (hd + 1) * dk)
        rhs_ref[:, 2 * hd * dk:(2 * hd + 1) * dk] = v_ref[:, lanes]
        rhs_ref[:, (2 * hd + 1) * dk:(2 * hd + 2) * dk] = ke[:, lanes].astype(BF16)

    row = lax.broadcasted_iota(jnp.int32, (c, c), 0)
    col = lax.broadcasted_iota(jnp.int32, (c, c), 1)
    incl = (row <= col) if reverse else (row >= col)
    head_lanes = [slice(hd * dk, (hd + 1) * dk) for hd in range(heads)]
    n_pair = heads // 2

    for ci in range(group):
        rows = slice(ci * c, (ci + 1) * c)
        for hd, lanes in enumerate(head_lanes):
            dec = _decay(g_ref[rows, hd * dk:hd * dk + c], rows_ref[ci, hd:hd + 1, :], incl)
            gram = lax.dot_general(q_ref[rows, lanes], k_ref[rows, lanes],
                                   (((1,), (1,)), ((), ())), preferred_element_type=F32)
            attn_ref[ci * heads + hd] = (gram * dec).astype(BF16)
        for m in range(n_pair):
            width = 2 * dk
            t_pair = t_ref[ci * n_pair + m] * pairs_ref[ci, n_pair + m:n_pair + m + 1, :]
            rhs_pair = _head_pair_block_diag(rhs_ref[rows, 2 * m * width:(2 * m + 1) * width],
                                             rhs_ref[rows, (2 * m + 1) * width:(2 * m + 2) * width])
            sol = jnp.dot(t_pair.astype(BF16), rhs_pair, preferred_element_type=F32)
            for par in range(2):
                lanes = head_lanes[2 * m + par]
                u_ref[rows, lanes] = sol[:, par * width:par * width + dk]
                w_ref[rows, lanes] = sol[:, par * width + dk:(par + 1) * width].astype(BF16)

    for step in range(group):
        ci = group - 1 - step if reverse else step
        rows = slice(ci * c, (ci + 1) * c)
        last = ci * c if reverse else (ci + 1) * c - 1
        states = [s_ref[hd] for hd in range(heads)]
        states_bf = [s.astype(BF16) for s in states]
        w_s = [jnp.dot(w_ref[rows, lanes], s, preferred_element_type=F32)
               for lanes, s in zip(head_lanes, states_bf)]
        q_s = [jnp.dot(qe_ref[rows, lanes], s, preferred_element_type=F32)
               for lanes, s in zip(head_lanes, states_bf)]
        for hd, lanes in enumerate(head_lanes):
            v_new_bf = (u_ref[rows, lanes] - w_s[hd]).astype(BF16)
            o = q_s[hd] + jnp.dot(attn_ref[ci * heads + hd], v_new_bf, preferred_element_type=F32)
            o_ref[rows, lanes] = o + other_ref[rows, lanes] if add_other else o
            g_tot = g_ref[last:last + 1, lanes]
            k_dec_t = (kt_ref[ci, lanes, :] * jnp.exp(g_tot[:, :c] - rows_ref[ci, hd:hd + 1, :])).astype(BF16)
            s_ref[hd] = states[hd] * jnp.exp(g_tot) + jnp.dot(k_dec_t, v_new_bf, preferred_element_type=F32)


def _gdn_scan_block(q, k, v, kt, cols, rows_t, pairs, t, reverse, other=None):
    b, l, d = q.shape
    heads, c, group = GD_HEADS, GD_CHUNK, GD_GROUP
    dk = d // heads
    rows = c * group
    ng = l // rows
    per = group * heads
    dr = int(reverse)
    add_other = other is not None
    kern = functools.partial(_gdn_scan_kernel, heads=heads, dk=dk, c=c, group=group, reverse=reverse,
                             add_other=add_other)
    visit = (lambda g: ng - 1 - g) if reverse else (lambda g: g)
    wide = pl.BlockSpec((None, rows, d), lambda bb, g: (bb, visit(g), 0))
    sel = jnp.repeat(jnp.eye(2 * heads, heads, dtype=BF16), dk, axis=1)
    return pl.pallas_call(
        kern,
        grid=(b, ng),
        in_specs=[wide, wide, wide,
                  pl.BlockSpec((None, group, d, c), lambda bb, g: (bb, visit(g), 0, 0)),
                  pl.BlockSpec((None, None, rows, 2 * heads), lambda bb, g: (dr, bb, visit(g), 0)),
                  pl.BlockSpec((None, None, group, 2 * heads, c), lambda bb, g: (dr, bb, visit(g), 0, 0)),
                  pl.BlockSpec((None, None, group, heads, 2 * c), lambda bb, g: (dr, bb, visit(g), 0, 0)),
                  pl.BlockSpec((per // 2, c, 2 * c), lambda bb, g: (bb * ng + visit(g), 0, 0)),
                  pl.BlockSpec((2 * heads, d), lambda bb, g: (0, 0))] + [wide] * add_other,
        out_specs=wide,
        out_shape=jax.ShapeDtypeStruct((b, l, d), F32),
        scratch_shapes=[pltpu.VMEM((heads, dk, dk), F32), pltpu.VMEM((rows, d), F32),
                        pltpu.VMEM((rows, d), BF16), pltpu.VMEM((rows, 2 * d), BF16),
                        pltpu.VMEM((per, c, c), BF16),
                        pltpu.VMEM((rows, d), F32), pltpu.VMEM((rows, d), BF16)],
        compiler_params=_mixer_params(),
        name="gdn_scan_bwd" if reverse else "gdn_scan_fwd",
    )(q, k, v, kt, cols, rows_t, pairs, t, sel, *([other] if add_other else []))


def _gdn_out_kernel(x_ref, o_ref, z_ref, ng_ref, wout_ref, y_ref, g_ref, *, heads, dk):
    o = o_ref[...]
    z = z_ref[...].astype(F32)
    for hd in range(heads):
        lanes = slice(hd * dk, (hd + 1) * dk)
        oh = o[:, lanes]
        ms = jnp.mean(oh * oh, axis=-1, keepdims=True)
        zh = z[:, lanes]
        g_ref[:, lanes] = (oh * lax.rsqrt(ms + NORM_EPS) * ng_ref[...] * (zh * jax.nn.sigmoid(zh))).astype(BF16)
    y_ref[...] = x_ref[...] + jnp.dot(g_ref[...], wout_ref[...].astype(BF16), preferred_element_type=F32)


def _gdn_out_block(x, o, z, norm_g, w_out, lead):
    b, l, d = x.shape
    heads = GD_HEADS
    dk = d // heads
    tm = GD_ROW_TILE
    row = pl.BlockSpec((None, tm, d), lambda bb, i: (bb, i, 0))
    kern = functools.partial(_gdn_out_kernel, heads=heads, dk=dk)
    return pl.pallas_call(
        kern,
        grid=(b, l // tm),
        in_specs=[row, row, row,
                  pl.BlockSpec((1, dk), lambda bb, i: (0, 0)),
                  _stacked_weight_spec(w_out, lead)],
        out_specs=row,
        out_shape=jax.ShapeDtypeStruct((b, l, d), F32),
        scratch_shapes=[pltpu.VMEM((tm, d), BF16)],
        compiler_params=_mixer_params(),
        name="gdn_out",
    )(x, o, z, norm_g.reshape(1, dk), w_out)


def _gdn_block(x, gain, w_in, lead, conv_w, a_log, dt_bias, norm_g, w_out):
    q, k, v, z, kt, cols, rows_t, pairs = _gdn_in_block(x, gain, w_in, lead, conv_w, a_log, dt_bias)
    a = _gdn_prep_block(k, cols, pairs)
    o = None
    for dr in range(2):
        o = _gdn_scan_block(q, k, v, kt, cols, rows_t, pairs, _gdn_tsolve_block(a, dr), bool(dr), other=o)
    return _gdn_out_block(x, o, z, norm_g, w_out, lead)


def kernel(x, norms, final_norm, ffn_w_in, ffn_w_out, sc_w_in, sc_conv, sc_w_out, hy_w_in, hy_b_in, hy_conv, hy_conv_b, hy_f_w1, hy_f_b1, hy_f_w2, hy_f_b2, hy_f_w3, hy_f_freq, hy_d, hy_w_out, hy_b_out, gd_w_in, gd_conv, gd_a_log, gd_dt_bias, gd_norm, gd_w_out):
    b, l, d = x.shape
    depth = norms.shape[0]

    def ffn(xx, i, k):
        final = final_norm if (i == depth - 1 and k == 1) else None
        return _ffn_block(xx.reshape(b * l, d), norms[i, 2 * k], ffn_w_in, ffn_w_out, (i, k),
                          final).reshape(b, l, d)

    for i in range(depth):
        m, j = i % N_MIXERS, i // N_MIXERS
        x = ffn(x, i, 0)
        if m == 0:
            x = _sc_mixer_block(x, norms[i, 1], sc_w_in, sc_conv[j], sc_w_out, (j,))
        elif m == 1:
            x = _hyena_block(x, norms[i, 1], (j,), hy_w_in, hy_b_in[j], hy_conv[j], hy_conv_b[j],
                             hy_f_w1[j], hy_f_b1[j], hy_f_w2[j], hy_f_b2[j], hy_f_w3[j],
                             hy_f_freq[j], hy_d[j], hy_w_out, hy_b_out[j])
        else:
            x = _gdn_block(x, norms[i, 1], gd_w_in, (j,), gd_conv[j], gd_a_log[j], gd_dt_bias[j],
                           gd_norm[j], gd_w_out)
        x = ffn(x, i, 1)
    return x
```

```python
import functools
import math

import jax
import jax.numpy as jnp
from jax import lax
from jax.experimental import pallas as pl
from jax.experimental.pallas import tpu as pltpu

F32 = jnp.float32
BF16 = jnp.bfloat16

NORM_EPS = 1e-6
N_MIXERS = 3
HY_EMB = 33
HY_BANDS = (HY_EMB - 1) // 2
HY_MAX_DECAY = math.log(1e-2) / 0.3
HY_MIN_DECAY = math.log(1e-2) / 1.5
GD_HEADS = 8
GD_CHUNK = 64

V7X_VMEM_LIMIT_BYTES = 56 * 1024 * 1024
FFN_ROW_TILE = 512
FFN_HIDDEN_TILE = 256
FFN_WEIGHT_LOOKAHEAD = 2


def _ffn_kernel(x_ref, gain_ref, win_hbm, wout_hbm, *rest, lead, n_hidden_tiles, hidden_tile, final_norm):
    final_ref, o_ref, win_ref, wout_ref, hid_ref, sem = rest if final_norm else (None,) + rest
    f = n_hidden_tiles * hidden_tile

    def weight_copies(j):
        cols = slice(j * hidden_tile, (j + 1) * hidden_tile)
        up_cols = slice(f + j * hidden_tile, f + (j + 1) * hidden_tile)
        return (pltpu.make_async_copy(win_hbm.at[(*lead, slice(None), cols)], win_ref.at[:, cols], sem.at[0, j]),
                pltpu.make_async_copy(win_hbm.at[(*lead, slice(None), up_cols)], win_ref.at[:, up_cols],
                                      sem.at[1, j]),
                pltpu.make_async_copy(wout_hbm.at[(*lead, cols, slice(None))], wout_ref.at[cols, :], sem.at[2, j]))

    def start_weights(j):
        for copy in weight_copies(j):
            copy.start()

    def step(load_weights):
        x = x_ref[...]
        xn = _rms_rows(x, gain_ref[...]).astype(BF16)
        if load_weights:
            for j in range(min(FFN_WEIGHT_LOOKAHEAD, n_hidden_tiles)):
                start_weights(j)
        for j in range(n_hidden_tiles):
            if load_weights:
                for copy in weight_copies(j):
                    copy.wait()
                if j + FFN_WEIGHT_LOOKAHEAD < n_hidden_tiles:
                    start_weights(j + FFN_WEIGHT_LOOKAHEAD)
            cols = slice(j * hidden_tile, (j + 1) * hidden_tile)
            up_cols = slice(f + j * hidden_tile, f + (j + 1) * hidden_tile)
            g = jnp.dot(xn, win_ref[:, cols].astype(BF16), preferred_element_type=F32)
            u = jnp.dot(xn, win_ref[:, up_cols].astype(BF16), preferred_element_type=F32)
            hid_ref[:, cols] = (g * jax.nn.sigmoid(g) * u).astype(BF16)
        y = x + 0.5 * jnp.dot(hid_ref[...], wout_ref[...].astype(BF16), preferred_element_type=F32)
        o_ref[...] = _rms_rows(y, final_ref[...]) if final_norm else y

    first = pl.program_id(0) == 0
    pl.when(first)(functools.partial(step, True))
    pl.when(jnp.logical_not(first))(functools.partial(step, False))


def _stacked_weight_spec(w, lead):
    tail = w.shape[len(lead):]
    index = tuple(lead) + (0,) * len(tail)
    return pl.BlockSpec((None,) * len(lead) + tail, lambda *_: index, **_RESIDENT)


def _ffn_block(x2d, gain, w_in, w_out, lead, final_gain=None):
    t, d = x2d.shape
    f = w_out.shape[-2]
    tm = FFN_ROW_TILE
    final_norm = final_gain is not None
    n_hidden_tiles = f // FFN_HIDDEN_TILE
    kern = functools.partial(_ffn_kernel, lead=tuple(lead), n_hidden_tiles=n_hidden_tiles,
                             hidden_tile=FFN_HIDDEN_TILE, final_norm=final_norm)
    vec = pl.BlockSpec((1, d), lambda i: (0, 0))
    extra = ([vec], [final_gain.reshape(1, d)]) if final_norm else ([], [])
    return pl.pallas_call(
        kern,
        grid=(t // tm,),
        in_specs=[
            pl.BlockSpec((tm, d), lambda i: (i, 0)),
            vec,
            pl.BlockSpec(memory_space=pltpu.HBM),
            pl.BlockSpec(memory_space=pltpu.HBM),
        ] + extra[0],
        out_specs=pl.BlockSpec((tm, d), lambda i: (i, 0)),
        out_shape=jax.ShapeDtypeStruct((t, d), F32),
        scratch_shapes=[pltpu.VMEM(w_in.shape[-2:], F32), pltpu.VMEM(w_out.shape[-2:], F32),
                        pltpu.VMEM((tm, f), BF16), pltpu.SemaphoreType.DMA((3, n_hidden_tiles))],
        compiler_params=pltpu.CompilerParams(
            dimension_semantics=("arbitrary",), vmem_limit_bytes=V7X_VMEM_LIMIT_BYTES),
        name="ffn_final" if final_norm else "ffn_block",
    )(x2d, gain.reshape(1, d), w_in, w_out, *extra[1])


MIX_ROW_TILE = 1024
GD_ROW_TILE = 512
HALO = 8


def _rms_rows(x, gain):
    ms = jnp.mean(x * x, axis=-1, keepdims=True)
    return x * lax.rsqrt(ms + NORM_EPS) * gain


def _bf16_terms(x):
    hi = x.astype(BF16)
    rest = x - hi.astype(F32)
    mid = rest.astype(BF16)
    lo = (rest - mid.astype(F32)).astype(BF16)
    return hi, mid, lo


def _halo_specs(tm, d, seq_len):
    per = tm // HALO
    last_blk = seq_len // HALO - 1
    return [
        pl.BlockSpec((None, HALO, d), lambda b, i: (b, jnp.maximum(i * per - 1, 0), 0)),
        pl.BlockSpec((None, tm, d), lambda b, i: (b, i, 0)),
        pl.BlockSpec((None, HALO, d), lambda b, i: (b, jnp.minimum((i + 1) * per, last_blk), 0)),
    ]


def _store_with_halo(s_ref, main, halo, tm):
    i = pl.program_id(1)
    last = pl.num_programs(1) - 1
    s_ref[0:HALO, :] = jnp.where(i > 0, halo[:HALO], 0.0)
    s_ref[HALO:HALO + tm, :] = main
    s_ref[HALO + tm:2 * HALO + tm, :] = jnp.where(i < last, halo[HALO:], 0.0)


def _conv3_from(s_ref, cw, tm):
    return (cw[0:1] * s_ref[HALO - 1:HALO - 1 + tm, :] + cw[1:2] * s_ref[HALO:HALO + tm, :]
            + cw[2:3] * s_ref[HALO + 1:HALO + 1 + tm, :])


_RESIDENT = dict(pipeline_mode=pl.Buffered(1))


def _mixer_params():
    return pltpu.CompilerParams(dimension_semantics=("arbitrary", "arbitrary"),
                                vmem_limit_bytes=V7X_VMEM_LIMIT_BYTES)


def _sc_mixer_kernel(xp_ref, x_ref, xn_ref, gain_ref, win_ref, cw_ref, wout_ref, o_ref, ch_ref, *, tm, d):
    gain = gain_ref[...]
    x = x_ref[...]
    h = _rms_rows(x, gain).astype(BF16)
    w_c = win_ref[:, d:2 * d].astype(BF16)
    w_h = win_ref[:, 2 * d:].astype(BF16)
    c_main = jnp.dot(h, w_c, preferred_element_type=F32)
    h_main = jnp.dot(h, w_h, preferred_element_type=F32)
    hh = _rms_rows(jnp.concatenate([xp_ref[...], xn_ref[...]], axis=0), gain).astype(BF16)
    c_halo = jnp.dot(hh, w_c, preferred_element_type=F32)
    h_halo = jnp.dot(hh, w_h, preferred_element_type=F32)
    _store_with_halo(ch_ref, c_main * h_main, c_halo * h_halo, tm)
    conv = _conv3_from(ch_ref, cw_ref[...], tm)
    b_main = jnp.dot(h, win_ref[:, :d].astype(BF16), preferred_element_type=F32)
    y = jnp.dot((b_main * conv).astype(BF16), wout_ref[...].astype(BF16), preferred_element_type=F32)
    o_ref[...] = x + y


def _sc_mixer_block(x, gain, w_in, conv_w, w_out, lead):
    b, l, d = x.shape
    tm = MIX_ROW_TILE
    kern = functools.partial(_sc_mixer_kernel, tm=tm, d=d)
    return pl.pallas_call(
        kern,
        grid=(b, l // tm),
        in_specs=_halo_specs(tm, d, l) + [
            pl.BlockSpec((1, d), lambda bb, i: (0, 0)),
            _stacked_weight_spec(w_in, lead),
            pl.BlockSpec((3, d), lambda bb, i: (0, 0)),
            _stacked_weight_spec(w_out, lead),
        ],
        out_specs=pl.BlockSpec((None, tm, d), lambda bb, i: (bb, i, 0)),
        out_shape=jax.ShapeDtypeStruct((b, l, d), F32),
        scratch_shapes=[pltpu.VMEM((tm + 2 * HALO, d), F32)],
        compiler_params=_mixer_params(),
        name="sc_mixer",
    )(x, x, x, gain.reshape(1, d), w_in, conv_w, w_out)


def _hy_in_kernel(xp_ref, x_ref, xn_ref, gain_ref, win_ref, bin_ref, cw_ref, cb_ref, x0_ref, vxt_ref, s_ref,
                  *, tm, d):
    gain = gain_ref[...]
    h = _rms_rows(x_ref[...], gain).astype(BF16)
    hh = _rms_rows(jnp.concatenate([xp_ref[...], xn_ref[...]], axis=0), gain).astype(BF16)
    parts = []
    for j in range(3):
        cols = slice(j * d, (j + 1) * d)
        bias = bin_ref[:, cols]
        w_cols = win_ref[:, cols].astype(BF16)
        main = jnp.dot(h, w_cols, preferred_element_type=F32) + bias
        halo = jnp.dot(hh, w_cols, preferred_element_type=F32) + bias
        _store_with_halo(s_ref, main, halo, tm)
        parts.append(_conv3_from(s_ref, cw_ref[:, cols], tm) + cb_ref[:, cols])
    x0_ref[...] = parts[0]
    vxt_ref[...] = (parts[2] * parts[1]).T


def _hy_in_block(x, gain, w_in, lead, b_in, conv_w, conv_b):
    b, l, d = x.shape
    tm = MIX_ROW_TILE
    kern = functools.partial(_hy_in_kernel, tm=tm, d=d)
    return pl.pallas_call(
        kern,
        grid=(b, l // tm),
        in_specs=_halo_specs(tm, d, l) + [
            pl.BlockSpec((1, d), lambda bb, i: (0, 0)),
            _stacked_weight_spec(w_in, lead),
            pl.BlockSpec((1, 3 * d), lambda bb, i: (0, 0)),
            pl.BlockSpec((3, 3 * d), lambda bb, i: (0, 0)),
            pl.BlockSpec((1, 3 * d), lambda bb, i: (0, 0)),
        ],
        out_specs=[pl.BlockSpec((None, tm, d), lambda bb, i: (bb, i, 0)),
                   pl.BlockSpec((None, d, tm), lambda bb, i: (bb, 0, i))],
        out_shape=[jax.ShapeDtypeStruct((b, l, d), F32), jax.ShapeDtypeStruct((b, d, l), F32)],
        scratch_shapes=[pltpu.VMEM((tm + 2 * HALO, d), F32)],
        compiler_params=_mixer_params(),
        name="hyena_in",
    )(x, x, x, gain.reshape(1, d), w_in, b_in.reshape(1, 3 * d), conv_w, conv_b.reshape(1, 3 * d))


HY_FEAT_PAD = 64
HY_FILTER_TILE = 2048


def _hy_filter_kernel(zt_ref, w1_ref, b1_ref, w2_ref, b2_ref, freq_ref, w3_ref, delta_ref, hc_ref):
    def t_dot(w, a):
        return lax.dot_general(w, a, (((0,), (0,)), ((), ())), precision=lax.Precision.HIGHEST,
                               preferred_element_type=F32)

    z = zt_ref[...]
    freq = freq_ref[...]
    h1 = jnp.sin(freq * (t_dot(w1_ref[...], z) + b1_ref[...]))
    h2 = jnp.sin(freq * (t_dot(w2_ref[...], h1) + b2_ref[...]))
    h = t_dot(w3_ref[...], h2)
    t_row = z[0:1, :]
    mask_row = z[HY_EMB:HY_EMB + 1, :]
    hc_ref[...] = h * jnp.exp(-t_row * delta_ref[...]) * mask_row


def _hy_filter_features(l):
    p = jnp.arange(2 * l)
    pos = jnp.where(p < l, p, 2 * l - p)
    valid = (p != l).astype(F32)
    pos = jnp.minimum(pos, l - 1).astype(F32)[None, :]
    t = pos / (l - 1)
    w = 2.0 * math.pi * pos / l
    f = jnp.linspace(1e-4, HY_BANDS - 1, HY_BANDS, dtype=F32)[:, None]
    z = jnp.concatenate([t, jnp.cos(f * w), -jnp.sin(f * w), valid[None, :]], axis=0)
    return jnp.pad(z, ((0, HY_FEAT_PAD - z.shape[0]), (0, 0)))


def _hy_filter_block(l, w1, b1, w2, b2, w3, freq):
    order = w1.shape[1]
    d = w3.shape[1] // 2
    tl = min(HY_FILTER_TILE, l)
    half = l // tl
    zt = _hy_filter_features(l)
    w1p = jnp.pad(w1, ((0, HY_FEAT_PAD - w1.shape[0]), (0, 0)))
    delta = jnp.abs(jnp.linspace(HY_MIN_DECAY, HY_MAX_DECAY, d, dtype=F32)).reshape(d, 1)
    col = lambda v: v.reshape(order, 1)
    full = lambda shape: pl.BlockSpec(shape, lambda i: (0,) * len(shape))
    return pl.pallas_call(
        _hy_filter_kernel,
        grid=(2 * half,),
        in_specs=[pl.BlockSpec((HY_FEAT_PAD, tl), lambda i: (0, i)),
                  full((HY_FEAT_PAD, order)), full((order, 1)), full((order, order)), full((order, 1)),
                  full((order, 1)),
                  pl.BlockSpec((order, d), lambda i: (0, i // half)),
                  full((d, 1))],
        out_specs=pl.BlockSpec((d, tl), lambda i: (0, i)),
        out_shape=jax.ShapeDtypeStruct((d, 2 * l), F32),
        compiler_params=pltpu.CompilerParams(dimension_semantics=("arbitrary",),
                                             vmem_limit_bytes=V7X_VMEM_LIMIT_BYTES),
        name="hyena_filter",
    )(zt, w1p, col(b1), w2, col(b2), col(freq), w3, delta)


HY_CONV_CH_TILE = 32


def _dft_tables(n1, n2):
    import numpy as np
    n = n1 * n2
    i1, i2 = np.arange(n1), np.arange(n2)
    a1 = 2.0 * np.pi * np.outer(i1, i1) / n1
    a2 = 2.0 * np.pi * np.outer(i2, i2) / n2
    c1, s1, c2, s2 = np.cos(a1), np.sin(a1), np.cos(a2), np.sin(a2)
    tw = 2.0 * np.pi * np.outer(i2, i1) / n
    f_real = np.concatenate([c1, -s1], axis=1)
    f_cplx = np.block([[c2, -s2], [s2, c2]])
    i_cplx = np.block([[c2, s2], [-s2, c2]])
    i_real = np.concatenate([c1, -s1], axis=0) / n
    as_bf = lambda a: jnp.asarray(a, dtype=F32).astype(BF16)
    as_f = lambda a: jnp.asarray(a, dtype=F32)
    return (as_bf(f_real), as_bf(f_cplx), as_bf(i_cplx), as_bf(i_real),
            as_f(np.cos(tw)), as_f(np.sin(tw)), as_f(np.cos(tw.T)), as_f(np.sin(tw.T)))


def _swap(a):
    return jnp.swapaxes(a, 1, 2)


def _hy_conv_kernel(ut_ref, hc_ref, dbias_ref, freal_ref, fcplx_ref, icplx_ref, ireal_ref, twc_ref, tws_ref,
                    twct_ref, twst_ref, yt_ref, hspec_ref, *, ct, n1, n2):
    rows = ct * n2
    twc = twc_ref[...]
    tws = tws_ref[...]

    def mm(a, m_ref):
        return jnp.dot(a.astype(BF16), m_ref[...], preferred_element_type=F32)

    def forward(x):
        p = mm(_swap(x).reshape(rows, n1), freal_ref)
        pr = p[:, :n1].reshape(ct, n2, n1)
        pi = p[:, n1:].reshape(ct, n2, n1)
        qr = pr * twc + pi * tws
        qi = pi * twc - pr * tws
        q = jnp.concatenate([_swap(qr).reshape(ct * n1, n2), _swap(qi).reshape(ct * n1, n2)], axis=1)
        xs = mm(q, fcplx_ref)
        return xs[:, :n2].reshape(ct, n1, n2), xs[:, n2:].reshape(ct, n1, n2)

    @pl.when(pl.program_id(1) == 0)
    def _():
        hr, hi = forward(hc_ref[...])
        hspec_ref[0] = hr
        hspec_ref[1] = hi

    u = ut_ref[...]
    xr, xi = forward(jnp.concatenate([u, jnp.zeros_like(u)], axis=1))
    hr = hspec_ref[0]
    hi = hspec_ref[1]
    yr = xr * hr - xi * hi
    yi = xr * hi + xi * hr
    r = mm(jnp.concatenate([yr.reshape(ct * n1, n2), yi.reshape(ct * n1, n2)], axis=1), icplx_ref)
    rr = r[:, :n2].reshape(ct, n1, n2)
    ri = r[:, n2:].reshape(ct, n1, n2)
    twct = twct_ref[...]
    twst = twst_ref[...]
    sr = rr * twct - ri * twst
    si = ri * twct + rr * twst
    s = jnp.concatenate([_swap(sr).reshape(rows, n1), _swap(si).reshape(rows, n1)], axis=1)
    z = mm(s, ireal_ref).reshape(ct, n2, n1)
    y = _swap(z)[:, :n1 // 2, :]
    yt_ref[...] = y + u * dbias_ref[...]


def _hy_conv_block(ut, hc, d_bias):
    b, d, l = ut.shape
    n2 = 128
    n1 = 2 * l // n2
    ct = HY_CONV_CH_TILE
    tables = _dft_tables(n1, n2)
    kern = functools.partial(_hy_conv_kernel, ct=ct, n1=n1, n2=n2)
    full = lambda a: pl.BlockSpec(a.shape, lambda c, bb: (0,) * a.ndim)
    yt = pl.pallas_call(
        kern,
        grid=(d // ct, b),
        in_specs=[pl.BlockSpec((None, ct, n1 // 2, n2), lambda c, bb: (bb, c, 0, 0)),
                  pl.BlockSpec((ct, n1, n2), lambda c, bb: (c, 0, 0)),
                  pl.BlockSpec((ct, 1, 1), lambda c, bb: (c, 0, 0))] + [full(t) for t in tables],
        out_specs=pl.BlockSpec((None, ct, n1 // 2, n2), lambda c, bb: (bb, c, 0, 0)),
        out_shape=jax.ShapeDtypeStruct((b, d, n1 // 2, n2), F32),
        scratch_shapes=[pltpu.VMEM((2, ct, n1, n2), F32)],
        compiler_params=_mixer_params(),
        name="hyena_longconv",
    )(ut.reshape(b, d, n1 // 2, n2), hc.reshape(d, n1, n2), d_bias.reshape(d, 1, 1), *tables)
    return yt


def _hyena_block(x, gain, lead, w_in, b_in, conv_w, conv_b, f_w1, f_b1, f_w2, f_b2, f_w3, f_freq, d_bias, w_out, b_out):
    l = x.shape[1]
    x0, ut = _hy_in_block(x, gain, w_in, lead, b_in, conv_w, conv_b)
    hc = _hy_filter_block(l, f_w1, f_b1, f_w2, f_b2, f_w3, f_freq)
    yt = _hy_conv_block(ut, hc, d_bias)
    return _hy_out_block(x, x0, yt, w_out, lead, b_out)


def _hy_out_kernel(x_ref, x0_ref, yt_ref, wout_ref, bout_ref, o_ref):
    slabs = jnp.swapaxes(yt_ref[...], 0, 1)
    y = jnp.concatenate([slabs[j].T for j in range(slabs.shape[0])], axis=0)
    z = jnp.dot((y * x0_ref[...]).astype(BF16), wout_ref[...].astype(BF16), preferred_element_type=F32)
    o_ref[...] = x_ref[...] + z + bout_ref[...]


def _hy_out_block(x, x0, yt, w_out, lead, b_out):
    b, l, d = x.shape
    tm = MIX_ROW_TILE
    slab = yt.shape[-1]
    row = pl.BlockSpec((None, tm, d), lambda bb, i: (bb, i, 0))
    return pl.pallas_call(
        _hy_out_kernel,
        grid=(b, l // tm),
        in_specs=[row, row, pl.BlockSpec((None, d, tm // slab, slab), lambda bb, i: (bb, 0, i, 0)),
                  _stacked_weight_spec(w_out, lead),
                  pl.BlockSpec((1, d), lambda bb, i: (0, 0))],
        out_specs=row,
        out_shape=jax.ShapeDtypeStruct((b, l, d), F32),
        compiler_params=_mixer_params(),
        name="hyena_out",
    )(x, x0, yt, w_out, b_out.reshape(1, d))


LANES = 128


def _softplus(x):
    return jnp.maximum(x, 0.0) + jnp.log1p(jnp.exp(-jnp.abs(x)))


def _gdn_in_kernel(xp_ref, x_ref, xn_ref, gain_ref, win_ref, cw_ref, alog_ref, dtb_ref, cum_ref, eye_ref,
                   q_ref, k_ref, v_ref, z_ref, kt_ref, cols_ref, rows_ref, pairs_ref, s_ref, *, tm, d, heads):
    hi = lax.Precision.HIGHEST
    gain = gain_ref[...]
    h = _rms_rows(x_ref[...], gain).astype(BF16)
    hh = _rms_rows(jnp.concatenate([xp_ref[...], xn_ref[...]], axis=0), gain).astype(BF16)
    dk = d // heads
    for j, (out_ref, scale) in enumerate(((q_ref, dk ** -0.5), (k_ref, 1.0), (v_ref, None))):
        cols = slice(j * d, (j + 1) * d)
        w_cols = win_ref[:, cols].astype(BF16)
        main = jnp.dot(h, w_cols, preferred_element_type=F32)
        halo = jnp.dot(hh, w_cols, preferred_element_type=F32)
        _store_with_halo(s_ref, main, halo, tm)
        c = _conv3_from(s_ref, cw_ref[:, cols], tm)
        c = c * jax.nn.sigmoid(c)
        if scale is None:
            out_ref[...] = c.astype(BF16)
        else:
            normed = []
            for hd in range(heads):
                lanes = slice(hd * dk, (hd + 1) * dk)
                ch = c[:, lanes]
                ss = jnp.sum(ch * ch, axis=-1, keepdims=True)
                normed.append(ch * (lax.rsqrt(ss + 1e-6) * scale))
                out_ref[:, lanes] = normed[-1].astype(BF16)
            if j == 1:
                kt = jnp.concatenate(normed, axis=1).T
                n_chunk, _, c = kt_ref.shape
                for ci in range(n_chunk):
                    kt_ref[ci] = kt[:, ci * c:(ci + 1) * c]
    z_ref[...] = jnp.dot(h, win_ref[:, 3 * d:4 * d].astype(BF16), preferred_element_type=F32).astype(BF16)
    ab = jnp.dot(h, win_ref[:, 4 * d:].astype(BF16), preferred_element_type=F32)
    nh2 = 2 * heads
    g = -jnp.exp(alog_ref[...]) * _softplus(ab[:, :nh2] + dtb_ref[...])
    beta = jax.nn.sigmoid(ab[:, nh2:2 * nh2])
    g_terms = _bf16_terms(g)
    cumsum = lambda mask: sum(jnp.dot(mask, term, preferred_element_type=F32) for term in g_terms)
    gc_f = cumsum(cum_ref[0])
    gc_b = cumsum(cum_ref[1])
    chunk = tm // rows_ref.shape[1]
    for dr, gc in enumerate((gc_f, gc_b)):
        own = slice(dr * heads, (dr + 1) * heads)
        cols = jnp.concatenate([gc[:, own], beta[:, own]], axis=1)
        cols_ref[dr] = cols
        rows = lax.dot_general(eye_ref[...], cols, (((1,), (1,)), ((), ())), precision=hi,
                               preferred_element_type=F32)
        for ci in range(tm // chunk):
            chunk_rows = rows[:, ci * chunk:(ci + 1) * chunk]
            rows_ref[dr, ci] = chunk_rows
            pairs_ref[dr, ci] = jnp.concatenate(
                [jnp.concatenate([chunk_rows[r:r + 1], chunk_rows[r + 1:r + 2]], axis=1)
                 for r in range(0, nh2, 2)], axis=0)


def _chunk_cumsum_masks(tm, chunk):
    import numpy as np
    r = np.arange(tm)
    same = (r[:, None] // chunk) == (r[None, :] // chunk)
    lower = same & (r[None, :] <= r[:, None])
    upper = same & (r[None, :] >= r[:, None])
    return jnp.asarray(np.stack([lower, upper]).astype(np.float32)).astype(BF16)


def _gdn_in_block(x, gain, w_in, lead, conv_w, a_log, dt_bias):
    b, l, d = x.shape
    heads = GD_HEADS
    nh2 = 2 * heads
    tm = GD_ROW_TILE
    kern = functools.partial(_gdn_in_kernel, tm=tm, d=d, heads=heads)
    row = pl.BlockSpec((None, tm, d), lambda bb, i: (bb, i, 0))
    wide = jax.ShapeDtypeStruct((b, l, d), BF16)
    c = GD_CHUNK
    return pl.pallas_call(
        kern,
        grid=(b, l // tm),
        in_specs=_halo_specs(tm, d, l) + [
            pl.BlockSpec((1, d), lambda bb, i: (0, 0)),
            _stacked_weight_spec(w_in, lead),
            pl.BlockSpec((3, 3 * d), lambda bb, i: (0, 0)),
            pl.BlockSpec((1, nh2), lambda bb, i: (0, 0)),
            pl.BlockSpec((1, nh2), lambda bb, i: (0, 0)),
            pl.BlockSpec((2, tm, tm), lambda bb, i: (0, 0, 0), **_RESIDENT),
            pl.BlockSpec((nh2, nh2), lambda bb, i: (0, 0)),
        ],
        out_specs=[row, row, row, row,
                   pl.BlockSpec((None, tm // c, d, c), lambda bb, i: (bb, i, 0, 0)),
                   pl.BlockSpec((2, None, tm, nh2), lambda bb, i: (0, bb, i, 0)),
                   pl.BlockSpec((2, None, tm // c, nh2, c), lambda bb, i: (0, bb, i, 0, 0)),
                   pl.BlockSpec((2, None, tm // c, heads, 2 * c), lambda bb, i: (0, bb, i, 0, 0))],
        out_shape=[wide, wide, wide, wide, jax.ShapeDtypeStruct((b, l // c, d, c), F32),
                   jax.ShapeDtypeStruct((2, b, l, nh2), F32),
                   jax.ShapeDtypeStruct((2, b, l // c, nh2, c), F32),
                   jax.ShapeDtypeStruct((2, b, l // c, heads, 2 * c), F32)],
        scratch_shapes=[pltpu.VMEM((tm + 2 * HALO, d), F32)],
        compiler_params=_mixer_params(),
        name="gdn_in",
    )(x, x, x, gain.reshape(1, d), w_in, conv_w, a_log.reshape(1, nh2), dt_bias.reshape(1, nh2),
      _chunk_cumsum_masks(tm, GD_CHUNK), jnp.eye(nh2, dtype=F32))


GD_GROUP = 8


def _decay(gc_col, gc_row, after_or_same):
    return jnp.exp(jnp.where(after_or_same, gc_col - gc_row, -jnp.inf))


def _spread_exact(cols, sel):
    hi, mid, lo = (jnp.dot(term, sel, preferred_element_type=F32) for term in _bf16_terms(cols))
    return (hi + mid) + lo


def _head_pair_block_diag(left, right):
    zero = jnp.zeros_like(left)
    return jnp.concatenate([jnp.concatenate([left, zero], axis=1),
                            jnp.concatenate([zero, right], axis=1)], axis=0)


def _gdn_prep_kernel(k_ref, cols_ref, pairs_ref, selg_ref, selb_ref, a_ref, *, heads, dk, c, group):
    direction = pl.program_id(0)
    row = lax.broadcasted_iota(jnp.int32, (c, 2 * c), 0)
    col = lax.broadcasted_iota(jnp.int32, (c, 2 * c), 1) % c
    delta = (row - col) * (1 - 2 * direction)
    cols = cols_ref[...]
    g_pairs = _spread_exact(cols, selg_ref[...])
    b_pairs = _spread_exact(cols, selb_ref[...])
    n_pair = heads // 2
    for ci in range(group):
        rows = slice(ci * c, (ci + 1) * c)
        for m in range(n_pair):
            lanes = slice(2 * c * m, 2 * c * (m + 1))
            kp = k_ref[rows, 2 * dk * m:2 * dk * (m + 1)]
            gram = lax.dot_general(kp, _head_pair_block_diag(kp[:, :dk], kp[:, dk:]),
                                   (((1,), (1,)), ((), ())), preferred_element_type=F32)
            dec = jnp.exp(jnp.where(delta > 0, g_pairs[rows, lanes] - pairs_ref[ci, m:m + 1, :], -jnp.inf))
            a_ref[:, ci * n_pair + m, :] = b_pairs[rows, lanes] * gram * dec


def _pair_selectors(heads, c):
    eye = jnp.eye(2 * heads, heads, dtype=BF16)
    return jnp.repeat(eye, c, axis=1), jnp.repeat(jnp.roll(eye, heads, axis=0), c, axis=1)


def _gdn_prep_block(k, cols, pairs):
    b, l, d = k.shape
    heads, c, group = GD_HEADS, GD_CHUNK, GD_GROUP
    rows = c * group
    ng = l // rows
    kern = functools.partial(_gdn_prep_kernel, heads=heads, dk=d // heads, c=c, group=group)
    per = group * heads // 2
    full = lambda a: pl.BlockSpec(a.shape, lambda dr, bb, g: (0,) * a.ndim)
    selg, selb = _pair_selectors(heads, c)
    return pl.pallas_call(
        kern,
        grid=(2, b, ng),
        in_specs=[pl.BlockSpec((None, rows, d), lambda dr, bb, g: (bb, g, 0)),
                  pl.BlockSpec((None, None, rows, 2 * heads), lambda dr, bb, g: (dr, bb, g, 0)),
                  pl.BlockSpec((None, None, group, heads, 2 * c), lambda dr, bb, g: (dr, bb, g, 0, 0)),
                  full(selg), full(selb)],
        out_specs=pl.BlockSpec((None, c, per, 2 * c), lambda dr, bb, g: (dr, 0, bb * ng + g, 0)),
        out_shape=jax.ShapeDtypeStruct((2, c, b * ng * per, 2 * c), F32),
        compiler_params=pltpu.CompilerParams(dimension_semantics=("arbitrary",) * 3,
                                             vmem_limit_bytes=V7X_VMEM_LIMIT_BYTES),
        name="gdn_prep",
    )(k, cols, pairs, selg, selb)


SUBLANES = 8


def _gdn_tsolve_kernel(a_ref, t_ref, a3_ref, t3_ref, *, c, upper):
    p = a_ref.shape[1]
    for i in range(c):
        slab = a_ref[i].T
        a3_ref[0, i] = slab[:c]
        a3_ref[1, i] = slab[c:]
    t3_ref[...] = jnp.zeros_like(t3_ref)
    ntile = c // SUBLANES
    sub = lax.broadcasted_iota(jnp.int32, (SUBLANES, p), 0)
    order = range(c - 1, -1, -1) if upper else range(c)
    for i in order:
        solved = range(i + 1, c) if upper else range(i)
        lo, hi = (i // SUBLANES, ntile) if upper else (0, i // SUBLANES + 1)
        ti = i // SUBLANES
        for half in range(2):
            acc = [jnp.zeros((SUBLANES, p), F32) for _ in range(lo, hi)]
            for j in solved:
                jlo, jhi = (j // SUBLANES, ntile) if upper else (0, j // SUBLANES + 1)
                coef = a3_ref[half, i, j:j + 1, :]
                tj = t3_ref[half, j, jlo * SUBLANES:jhi * SUBLANES, :]
                for t in range(jlo, jhi):
                    acc[t - lo] = acc[t - lo] - coef * tj[(t - jlo) * SUBLANES:(t - jlo + 1) * SUBLANES]
            acc[ti - lo] = acc[ti - lo] + jnp.where(sub == i % SUBLANES, 1.0, 0.0)
            t3_ref[half, i, lo * SUBLANES:hi * SUBLANES, :] = jnp.concatenate(acc, axis=0)
    for ib in range(0, c, SUBLANES):
        slabs = jnp.stack([jnp.concatenate([t3_ref[0, i], t3_ref[1, i]], axis=0).T
                           for i in range(ib, ib + SUBLANES)], axis=0)
        t_ref[:, ib:ib + SUBLANES, :] = jnp.swapaxes(slabs, 0, 1)


def _gdn_tsolve_block(a, direction):
    _, c, pairs, _ = a.shape
    blk = LANES
    upper = bool(direction)
    kern = functools.partial(_gdn_tsolve_kernel, c=c, upper=upper)
    return pl.pallas_call(
        kern,
        grid=(pairs // blk,),
        in_specs=[pl.BlockSpec((None, c, blk, 2 * c), lambda i: (direction, 0, i, 0))],
        out_specs=pl.BlockSpec((blk, c, 2 * c), lambda i: (i, 0, 0)),
        out_shape=jax.ShapeDtypeStruct((pairs, c, 2 * c), F32),
        scratch_shapes=[pltpu.VMEM((2, c, c, blk), F32), pltpu.VMEM((2, c, c, blk), F32)],
        compiler_params=pltpu.CompilerParams(dimension_semantics=("arbitrary",),
                                             vmem_limit_bytes=V7X_VMEM_LIMIT_BYTES),
        name="gdn_tsolve_bwd" if upper else "gdn_tsolve_fwd",
    )(a)


def _gdn_scan_kernel(q_ref, k_ref, v_ref, kt_ref, cols_ref, rows_ref, pairs_ref, t_ref, sel_ref, *rest,
                     heads, dk, c, group, reverse, add_other):
    other_ref = rest[0] if add_other else None
    o_ref, s_ref, g_ref, qe_ref, rhs_ref, attn_ref, u_ref, w_ref = rest[int(add_other):]

    @pl.when(pl.program_id(1) == 0)
    def _():
        s_ref[...] = jnp.zeros_like(s_ref)

    g_all = _spread_exact(cols_ref[...], sel_ref[...])
    g_ref[...] = g_all
    eg = jnp.exp(g_all)
    qe_ref[...] = (q_ref[...] * eg).astype(BF16)
    ke = k_ref[...] * eg
    for hd in range(heads):
        lanes = slice(hd * dk, (hd + 1) * dk)
        rhs_ref[:, 2 * hd * dk:(2 * hd + 1) * dk] = v_ref[:, lanes]
        rhs_ref[:, (2 * hd + 1) * dk:(2 * hd + 2) * dk] = ke[:, lanes].astype(BF16)

    row = lax.broadcasted_iota(jnp.int32, (c, c), 0)
    col = lax.broadcasted_iota(jnp.int32, (c, c), 1)
    incl = (row <= col) if reverse else (row >= col)
    head_lanes = [slice(hd * dk, (hd + 1) * dk) for hd in range(heads)]
    n_pair = heads // 2

    for ci in range(group):
        rows = slice(ci * c, (ci + 1) * c)
        for hd, lanes in enumerate(head_lanes):
            dec = _decay(g_ref[rows, hd * dk:hd * dk + c], rows_ref[ci, hd:hd + 1, :], incl)
            gram = lax.dot_general(q_ref[rows, lanes], k_ref[rows, lanes],
                                   (((1,), (1,)), ((), ())), preferred_element_type=F32)
            attn_ref[ci * heads + hd] = (gram * dec).astype(BF16)
        for m in range(n_pair):
            width = 2 * dk
            t_pair = t_ref[ci * n_pair + m] * pairs_ref[ci, n_pair + m:n_pair + m + 1, :]
            rhs_pair = _head_pair_block_diag(rhs_ref[rows, 2 * m * width:(2 * m + 1) * width],
                                             rhs_ref[rows, (2 * m + 1) * width:(2 * m + 2) * width])
            sol = jnp.dot(t_pair.astype(BF16), rhs_pair, preferred_element_type=F32)
            for par in range(2):
                lanes = head_lanes[2 * m + par]
                u_ref[rows, lanes] = sol[:, par * width:par * width + dk]
                w_ref[rows, lanes] = sol[:, par * width + dk:(par + 1) * width].astype(BF16)

    for step in range(group):
        ci = group - 1 - step if reverse else step
        rows = slice(ci * c, (ci + 1) * c)
        last = ci * c if reverse else (ci + 1) * c - 1
        states = [s_ref[hd] for hd in range(heads)]
        states_bf = [s.astype(BF16) for s in states]
        w_s = [jnp.dot(w_ref[rows, lanes], s, preferred_element_type=F32)
               for lanes, s in zip(head_lanes, states_bf)]
        q_s = [jnp.dot(qe_ref[rows, lanes], s, preferred_element_type=F32)
               for lanes, s in zip(head_lanes, states_bf)]
        for hd, lanes in enumerate(head_lanes):
            v_new_bf = (u_ref[rows, lanes] - w_s[hd]).astype(BF16)
            o = q_s[hd] + jnp.dot(attn_ref[ci * heads + hd], v_new_bf, preferred_element_type=F32)
            o_ref[rows, lanes] = o + other_ref[rows, lanes] if add_other else o
            g_tot = g_ref[last:last + 1, lanes]
            k_dec_t = (kt_ref[ci, lanes, :] * jnp.exp(g_tot[:, :c] - rows_ref[ci, hd:hd + 1, :])).astype(BF16)
            s_ref[hd] = states[hd] * jnp.exp(g_tot) + jnp.dot(k_dec_t, v_new_bf, preferred_element_type=F32)


def _gdn_scan_block(q, k, v, kt, cols, rows_t, pairs, t, reverse, other=None):
    b, l, d = q.shape
    heads, c, group = GD_HEADS, GD_CHUNK, GD_GROUP
    dk = d // heads
    rows = c * group
    ng = l // rows
    per = group * heads
    dr = int(reverse)
    add_other = other is not None
    kern = functools.partial(_gdn_scan_kernel, heads=heads, dk=dk, c=c, group=group, reverse=reverse,
                             add_other=add_other)
    visit = (lambda g: ng - 1 - g) if reverse else (lambda g: g)
    wide = pl.BlockSpec((None, rows, d), lambda bb, g: (bb, visit(g), 0))
    sel = jnp.repeat(jnp.eye(2 * heads, heads, dtype=BF16), dk, axis=1)
    return pl.pallas_call(
        kern,
        grid=(b, ng),
        in_specs=[wide, wide, wide,
                  pl.BlockSpec((None, group, d, c), lambda bb, g: (bb, visit(g), 0, 0)),
                  pl.BlockSpec((None, None, rows, 2 * heads), lambda bb, g: (dr, bb, visit(g), 0)),
                  pl.BlockSpec((None, None, group, 2 * heads, c), lambda bb, g: (dr, bb, visit(g), 0, 0)),
                  pl.BlockSpec((None, None, group, heads, 2 * c), lambda bb, g: (dr, bb, visit(g), 0, 0)),
                  pl.BlockSpec((per // 2, c, 2 * c), lambda bb, g: (bb * ng + visit(g), 0, 0)),
                  pl.BlockSpec((2 * heads, d), lambda bb, g: (0, 0))] + [wide] * add_other,
        out_specs=wide,
        out_shape=jax.ShapeDtypeStruct((b, l, d), F32),
        scratch_shapes=[pltpu.VMEM((heads, dk, dk), F32), pltpu.VMEM((rows, d), F32),
                        pltpu.VMEM((rows, d), BF16), pltpu.VMEM((rows, 2 * d), BF16),
                        pltpu.VMEM((per, c, c), BF16),
                        pltpu.VMEM((rows, d), F32), pltpu.VMEM((rows, d), BF16)],
        compiler_params=_mixer_params(),
        name="gdn_scan_bwd" if reverse else "gdn_scan_fwd",
    )(q, k, v, kt, cols, rows_t, pairs, t, sel, *([other] if add_other else []))


def _gdn_out_kernel(x_ref, o_ref, z_ref, ng_ref, wout_ref, y_ref, g_ref, *, heads, dk):
    o = o_ref[...]
    z = z_ref[...].astype(F32)
    for hd in range(heads):
        lanes = slice(hd * dk, (hd + 1) * dk)
        oh = o[:, lanes]
        ms = jnp.mean(oh * oh, axis=-1, keepdims=True)
        zh = z[:, lanes]
        g_ref[:, lanes] = (oh * lax.rsqrt(ms + NORM_EPS) * ng_ref[...] * (zh * jax.nn.sigmoid(zh))).astype(BF16)
    y_ref[...] = x_ref[...] + jnp.dot(g_ref[...], wout_ref[...].astype(BF16), preferred_element_type=F32)


def _gdn_out_block(x, o, z, norm_g, w_out, lead):
    b, l, d = x.shape
    heads = GD_HEADS
    dk = d // heads
    tm = GD_ROW_TILE
    row = pl.BlockSpec((None, tm, d), lambda bb, i: (bb, i, 0))
    kern = functools.partial(_gdn_out_kernel, heads=heads, dk=dk)
    return pl.pallas_call(
        kern,
        grid=(b, l // tm),
        in_specs=[row, row, row,
                  pl.BlockSpec((1, dk), lambda bb, i: (0, 0)),
                  _stacked_weight_spec(w_out, lead)],
        out_specs=row,
        out_shape=jax.ShapeDtypeStruct((b, l, d), F32),
        scratch_shapes=[pltpu.VMEM((tm, d), BF16)],
        compiler_params=_mixer_params(),
        name="gdn_out",
    )(x, o, z, norm_g.reshape(1, dk), w_out)


def _gdn_block(x, gain, w_in, lead, conv_w, a_log, dt_bias, norm_g, w_out):
    q, k, v, z, kt, cols, rows_t, pairs = _gdn_in_block(x, gain, w_in, lead, conv_w, a_log, dt_bias)
    a = _gdn_prep_block(k, cols, pairs)
    o = None
    for dr in range(2):
        o = _gdn_scan_block(q, k, v, kt, cols, rows_t, pairs, _gdn_tsolve_block(a, dr), bool(dr), other=o)
    return _gdn_out_block(x, o, z, norm_g, w_out, lead)


def kernel(x, norms, final_norm, ffn_w_in, ffn_w_out, sc_w_in, sc_conv, sc_w_out, hy_w_in, hy_b_in, hy_conv, hy_conv_b, hy_f_w1, hy_f_b1, hy_f_w2, hy_f_b2, hy_f_w3, hy_f_freq, hy_d, hy_w_out, hy_b_out, gd_w_in, gd_conv, gd_a_log, gd_dt_bias, gd_norm, gd_w_out):
    b, l, d = x.shape
    depth = norms.shape[0]

    def ffn(xx, i, k):
        final = final_norm if (i == depth - 1 and k == 1) else None
        return _ffn_block(xx.reshape(b * l, d), norms[i, 2 * k], ffn_w_in, ffn_w_out, (i, k),
                          final).reshape(b, l, d)

    for i in range(depth):
        m, j = i % N_MIXERS, i // N_MIXERS
        x = ffn(x, i, 0)
        if m == 0:
            x = _sc_mixer_block(x, norms[i, 1], sc_w_in, sc_conv[j], sc_w_out, (j,))
        elif m == 1:
            x = _hyena_block(x, norms[i, 1], (j,), hy_w_in, hy_b_in[j], hy_conv[j], hy_conv_b[j],
                             hy_f_w1[j], hy_f_b1[j], hy_f_w2[j], hy_f_b2[j], hy_f_w3[j],
                             hy_f_freq[j], hy_d[j], hy_w_out, hy_b_out[j])
        else:
            x = _gdn_block(x, norms[i, 1], gd_w_in, (j,), gd_conv[j], gd_a_log[j], gd_dt_bias[j],
                           gd_norm[j], gd_w_out)
        x = ffn(x, i, 1)
    return x
```

```python
import functools
import math

import jax
import jax.numpy as jnp
from jax import lax
from jax.experimental import pallas as pl
from jax.experimental.pallas import tpu as pltpu

F32 = jnp.float32
BF16 = jnp.bfloat16

NORM_EPS = 1e-6
N_MIXERS = 3
HY_EMB = 33
HY_BANDS = (HY_EMB - 1) // 2
HY_MAX_DECAY = math.log(1e-2) / 0.3
HY_MIN_DECAY = math.log(1e-2) / 1.5
GD_HEADS = 8
GD_CHUNK = 64

V7X_VMEM_LIMIT_BYTES = 56 * 1024 * 1024
FFN_ROW_TILE = 1024
FFN_HIDDEN_TILE = 256
FFN_WEIGHT_LOOKAHEAD = 2


def _ffn_kernel(x_ref, gain_ref, win_hbm, wout_hbm, *rest, lead, n_hidden_tiles, hidden_tile, final_norm):
    final_ref, o_ref, win_ref, wout_ref, gate_stage, up_stage, out_stage, hid_ref, sem = (
        rest if final_norm else (None,) + rest)
    f = n_hidden_tiles * hidden_tile

    def weight_copies(j):
        cols = slice(j * hidden_tile, (j + 1) * hidden_tile)
        up_cols = slice(f + j * hidden_tile, f + (j + 1) * hidden_tile)
        slot = j % FFN_WEIGHT_LOOKAHEAD
        return (pltpu.make_async_copy(win_hbm.at[(*lead, slice(None), cols)], gate_stage.at[slot], sem.at[0, slot]),
                pltpu.make_async_copy(win_hbm.at[(*lead, slice(None), up_cols)], up_stage.at[slot], sem.at[1, slot]),
                pltpu.make_async_copy(wout_hbm.at[(*lead, cols, slice(None))], out_stage.at[slot], sem.at[2, slot]))

    def start_weights(j):
        for copy in weight_copies(j):
            copy.start()

    def step(load_weights):
        x = x_ref[...]
        xn = _rms_rows(x, gain_ref[...]).astype(BF16)
        if load_weights:
            for j in range(min(FFN_WEIGHT_LOOKAHEAD, n_hidden_tiles)):
                start_weights(j)
        for j in range(n_hidden_tiles):
            cols = slice(j * hidden_tile, (j + 1) * hidden_tile)
            up_cols = slice(f + j * hidden_tile, f + (j + 1) * hidden_tile)
            if load_weights:
                slot = j % FFN_WEIGHT_LOOKAHEAD
                for copy in weight_copies(j):
                    copy.wait()
                win_ref[:, cols] = gate_stage[slot].astype(BF16)
                win_ref[:, up_cols] = up_stage[slot].astype(BF16)
                wout_ref[cols, :] = out_stage[slot].astype(BF16)
                if j + FFN_WEIGHT_LOOKAHEAD < n_hidden_tiles:
                    start_weights(j + FFN_WEIGHT_LOOKAHEAD)
            g = jnp.dot(xn, win_ref[:, cols], preferred_element_type=F32)
            u = jnp.dot(xn, win_ref[:, up_cols], preferred_element_type=F32)
            hid_ref[:, cols] = (g * jax.nn.sigmoid(g) * u).astype(BF16)
        y = x + 0.5 * jnp.dot(hid_ref[...], wout_ref[...], preferred_element_type=F32)
        o_ref[...] = _rms_rows(y, final_ref[...]) if final_norm else y

    first = pl.program_id(0) == 0
    pl.when(first)(functools.partial(step, True))
    pl.when(jnp.logical_not(first))(functools.partial(step, False))


def _stacked_weight_spec(w, lead):
    tail = w.shape[len(lead):]
    index = tuple(lead) + (0,) * len(tail)
    return pl.BlockSpec((None,) * len(lead) + tail, lambda *_: index, **_RESIDENT)


def _ffn_block(x2d, gain, w_in, w_out, lead, final_gain=None):
    t, d = x2d.shape
    f = w_out.shape[-2]
    tm = FFN_ROW_TILE
    final_norm = final_gain is not None
    n_hidden_tiles = f // FFN_HIDDEN_TILE
    kern = functools.partial(_ffn_kernel, lead=tuple(lead), n_hidden_tiles=n_hidden_tiles,
                             hidden_tile=FFN_HIDDEN_TILE, final_norm=final_norm)
    vec = pl.BlockSpec((1, d), lambda i: (0, 0))
    extra = ([vec], [final_gain.reshape(1, d)]) if final_norm else ([], [])
    return pl.pallas_call(
        kern,
        grid=(t // tm,),
        in_specs=[
            pl.BlockSpec((tm, d), lambda i: (i, 0)),
            vec,
            pl.BlockSpec(memory_space=pltpu.HBM),
            pl.BlockSpec(memory_space=pltpu.HBM),
        ] + extra[0],
        out_specs=pl.BlockSpec((tm, d), lambda i: (i, 0)),
        out_shape=jax.ShapeDtypeStruct((t, d), F32),
        scratch_shapes=[pltpu.VMEM(w_in.shape[-2:], BF16), pltpu.VMEM(w_out.shape[-2:], BF16),
                        pltpu.VMEM((FFN_WEIGHT_LOOKAHEAD, d, FFN_HIDDEN_TILE), F32),
                        pltpu.VMEM((FFN_WEIGHT_LOOKAHEAD, d, FFN_HIDDEN_TILE), F32),
                        pltpu.VMEM((FFN_WEIGHT_LOOKAHEAD, FFN_HIDDEN_TILE, d), F32),
                        pltpu.VMEM((tm, f), BF16), pltpu.SemaphoreType.DMA((3, FFN_WEIGHT_LOOKAHEAD))],
        compiler_params=pltpu.CompilerParams(
            dimension_semantics=("arbitrary",), vmem_limit_bytes=V7X_VMEM_LIMIT_BYTES),
        name="ffn_final" if final_norm else "ffn_block",
    )(x2d, gain.reshape(1, d), w_in, w_out, *extra[1])


MIX_ROW_TILE = 1024
GD_ROW_TILE = 512
HALO = 8


def _rms_rows(x, gain):
    ms = jnp.mean(x * x, axis=-1, keepdims=True)
    return x * lax.rsqrt(ms + NORM_EPS) * gain


def _bf16_terms(x):
    hi = x.astype(BF16)
    rest = x - hi.astype(F32)
    mid = rest.astype(BF16)
    lo = (rest - mid.astype(F32)).astype(BF16)
    return hi, mid, lo


def _halo_specs(tm, d, seq_len):
    per = tm // HALO
    last_blk = seq_len // HALO - 1
    return [
        pl.BlockSpec((None, HALO, d), lambda b, i: (b, jnp.maximum(i * per - 1, 0), 0)),
        pl.BlockSpec((None, tm, d), lambda b, i: (b, i, 0)),
        pl.BlockSpec((None, HALO, d), lambda b, i: (b, jnp.minimum((i + 1) * per, last_blk), 0)),
    ]


def _store_with_halo(s_ref, main, halo, tm):
    i = pl.program_id(1)
    last = pl.num_programs(1) - 1
    s_ref[0:HALO, :] = jnp.where(i > 0, halo[:HALO], 0.0)
    s_ref[HALO:HALO + tm, :] = main
    s_ref[HALO + tm:2 * HALO + tm, :] = jnp.where(i < last, halo[HALO:], 0.0)


def _conv3_from(s_ref, cw, tm):
    return (cw[0:1] * s_ref[HALO - 1:HALO - 1 + tm, :] + cw[1:2] * s_ref[HALO:HALO + tm, :]
            + cw[2:3] * s_ref[HALO + 1:HALO + 1 + tm, :])


_RESIDENT = dict(pipeline_mode=pl.Buffered(1))


def _mixer_params():
    return pltpu.CompilerParams(dimension_semantics=("arbitrary", "arbitrary"),
                                vmem_limit_bytes=V7X_VMEM_LIMIT_BYTES)


def _sc_mixer_kernel(xp_ref, x_ref, xn_ref, gain_ref, win_ref, cw_ref, wout_ref, o_ref, ch_ref, *, tm, d):
    gain = gain_ref[...]
    x = x_ref[...]
    h = _rms_rows(x, gain).astype(BF16)
    w_c = win_ref[:, d:2 * d].astype(BF16)
    w_h = win_ref[:, 2 * d:].astype(BF16)
    c_main = jnp.dot(h, w_c, preferred_element_type=F32)
    h_main = jnp.dot(h, w_h, preferred_element_type=F32)
    hh = _rms_rows(jnp.concatenate([xp_ref[...], xn_ref[...]], axis=0), gain).astype(BF16)
    c_halo = jnp.dot(hh, w_c, preferred_element_type=F32)
    h_halo = jnp.dot(hh, w_h, preferred_element_type=F32)
    _store_with_halo(ch_ref, c_main * h_main, c_halo * h_halo, tm)
    conv = _conv3_from(ch_ref, cw_ref[...], tm)
    b_main = jnp.dot(h, win_ref[:, :d].astype(BF16), preferred_element_type=F32)
    y = jnp.dot((b_main * conv).astype(BF16), wout_ref[...].astype(BF16), preferred_element_type=F32)
    o_ref[...] = x + y


def _sc_mixer_block(x, gain, w_in, conv_w, w_out, lead):
    b, l, d = x.shape
    tm = MIX_ROW_TILE
    kern = functools.partial(_sc_mixer_kernel, tm=tm, d=d)
    return pl.pallas_call(
        kern,
        grid=(b, l // tm),
        in_specs=_halo_specs(tm, d, l) + [
            pl.BlockSpec((1, d), lambda bb, i: (0, 0)),
            _stacked_weight_spec(w_in, lead),
            pl.BlockSpec((3, d), lambda bb, i: (0, 0)),
            _stacked_weight_spec(w_out, lead),
        ],
        out_specs=pl.BlockSpec((None, tm, d), lambda bb, i: (bb, i, 0)),
        out_shape=jax.ShapeDtypeStruct((b, l, d), F32),
        scratch_shapes=[pltpu.VMEM((tm + 2 * HALO, d), F32)],
        compiler_params=_mixer_params(),
        name="sc_mixer",
    )(x, x, x, gain.reshape(1, d), w_in, conv_w, w_out)


def _hy_in_kernel(xp_ref, x_ref, xn_ref, gain_ref, win_ref, bin_ref, cw_ref, cb_ref, x0_ref, vxt_ref, s_ref,
                  *, tm, d):
    gain = gain_ref[...]
    h = _rms_rows(x_ref[...], gain).astype(BF16)
    hh = _rms_rows(jnp.concatenate([xp_ref[...], xn_ref[...]], axis=0), gain).astype(BF16)
    parts = []
    for j in range(3):
        cols = slice(j * d, (j + 1) * d)
        bias = bin_ref[:, cols]
        w_cols = win_ref[:, cols].astype(BF16)
        main = jnp.dot(h, w_cols, preferred_element_type=F32) + bias
        halo = jnp.dot(hh, w_cols, preferred_element_type=F32) + bias
        _store_with_halo(s_ref, main, halo, tm)
        parts.append(_conv3_from(s_ref, cw_ref[:, cols], tm) + cb_ref[:, cols])
    x0_ref[...] = parts[0]
    vxt_ref[...] = (parts[2] * parts[1]).T


def _hy_in_block(x, gain, w_in, lead, b_in, conv_w, conv_b):
    b, l, d = x.shape
    tm = MIX_ROW_TILE
    kern = functools.partial(_hy_in_kernel, tm=tm, d=d)
    return pl.pallas_call(
        kern,
        grid=(b, l // tm),
        in_specs=_halo_specs(tm, d, l) + [
            pl.BlockSpec((1, d), lambda bb, i: (0, 0)),
            _stacked_weight_spec(w_in, lead),
            pl.BlockSpec((1, 3 * d), lambda bb, i: (0, 0)),
            pl.BlockSpec((3, 3 * d), lambda bb, i: (0, 0)),
            pl.BlockSpec((1, 3 * d), lambda bb, i: (0, 0)),
        ],
        out_specs=[pl.BlockSpec((None, tm, d), lambda bb, i: (bb, i, 0)),
                   pl.BlockSpec((None, d, tm), lambda bb, i: (bb, 0, i))],
        out_shape=[jax.ShapeDtypeStruct((b, l, d), F32), jax.ShapeDtypeStruct((b, d, l), F32)],
        scratch_shapes=[pltpu.VMEM((tm + 2 * HALO, d), F32)],
        compiler_params=_mixer_params(),
        name="hyena_in",
    )(x, x, x, gain.reshape(1, d), w_in, b_in.reshape(1, 3 * d), conv_w, conv_b.reshape(1, 3 * d))


HY_FEAT_PAD = 64
HY_FILTER_TILE = 2048


def _hy_filter_kernel(zt_ref, w1_ref, b1_ref, w2_ref, b2_ref, freq_ref, w3_ref, delta_ref, hc_ref):
    def t_dot(w, a):
        return lax.dot_general(w, a, (((0,), (0,)), ((), ())), precision=lax.Precision.HIGHEST,
                               preferred_element_type=F32)

    z = zt_ref[...]
    freq = freq_ref[...]
    h1 = jnp.sin(freq * (t_dot(w1_ref[...], z) + b1_ref[...]))
    h2 = jnp.sin(freq * (t_dot(w2_ref[...], h1) + b2_ref[...]))
    h = t_dot(w3_ref[...], h2)
    t_row = z[0:1, :]
    mask_row = z[HY_EMB:HY_EMB + 1, :]
    hc_ref[...] = h * jnp.exp(-t_row * delta_ref[...]) * mask_row


def _hy_filter_features(l):
    p = jnp.arange(2 * l)
    pos = jnp.where(p < l, p, 2 * l - p)
    valid = (p != l).astype(F32)
    pos = jnp.minimum(pos, l - 1).astype(F32)[None, :]
    t = pos / (l - 1)
    w = 2.0 * math.pi * pos / l
    f = jnp.linspace(1e-4, HY_BANDS - 1, HY_BANDS, dtype=F32)[:, None]
    z = jnp.concatenate([t, jnp.cos(f * w), -jnp.sin(f * w), valid[None, :]], axis=0)
    return jnp.pad(z, ((0, HY_FEAT_PAD - z.shape[0]), (0, 0)))


def _hy_filter_block(l, w1, b1, w2, b2, w3, freq):
    order = w1.shape[1]
    d = w3.shape[1] // 2
    tl = min(HY_FILTER_TILE, l)
    half = l // tl
    zt = _hy_filter_features(l)
    w1p = jnp.pad(w1, ((0, HY_FEAT_PAD - w1.shape[0]), (0, 0)))
    delta = jnp.abs(jnp.linspace(HY_MIN_DECAY, HY_MAX_DECAY, d, dtype=F32)).reshape(d, 1)
    col = lambda v: v.reshape(order, 1)
    full = lambda shape: pl.BlockSpec(shape, lambda i: (0,) * len(shape))
    return pl.pallas_call(
        _hy_filter_kernel,
        grid=(2 * half,),
        in_specs=[pl.BlockSpec((HY_FEAT_PAD, tl), lambda i: (0, i)),
                  full((HY_FEAT_PAD, order)), full((order, 1)), full((order, order)), full((order, 1)),
                  full((order, 1)),
                  pl.BlockSpec((order, d), lambda i: (0, i // half)),
                  full((d, 1))],
        out_specs=pl.BlockSpec((d, tl), lambda i: (0, i)),
        out_shape=jax.ShapeDtypeStruct((d, 2 * l), F32),
        compiler_params=pltpu.CompilerParams(dimension_semantics=("arbitrary",),
                                             vmem_limit_bytes=V7X_VMEM_LIMIT_BYTES),
        name="hyena_filter",
    )(zt, w1p, col(b1), w2, col(b2), col(freq), w3, delta)


HY_CONV_CH_TILE = 32


def _dft_tables(n1, n2):
    import numpy as np
    n = n1 * n2
    i1, i2 = np.arange(n1), np.arange(n2)
    a1 = 2.0 * np.pi * np.outer(i1, i1) / n1
    a2 = 2.0 * np.pi * np.outer(i2, i2) / n2
    c1, s1, c2, s2 = np.cos(a1), np.sin(a1), np.cos(a2), np.sin(a2)
    tw = 2.0 * np.pi * np.outer(i2, i1) / n
    f_real = np.concatenate([c1, -s1], axis=1)
    f_cplx = np.block([[c2, -s2], [s2, c2]])
    i_cplx = np.block([[c2, s2], [-s2, c2]])
    i_real = np.concatenate([c1, -s1], axis=0) / n
    as_bf = lambda a: jnp.asarray(a, dtype=F32).astype(BF16)
    as_f = lambda a: jnp.asarray(a, dtype=F32)
    return (as_bf(f_real), as_bf(f_cplx), as_bf(i_cplx), as_bf(i_real),
            as_f(np.cos(tw)), as_f(np.sin(tw)), as_f(np.cos(tw.T)), as_f(np.sin(tw.T)))


def _swap(a):
    return jnp.swapaxes(a, 1, 2)


def _hy_conv_kernel(ut_ref, hc_ref, dbias_ref, freal_ref, fcplx_ref, icplx_ref, ireal_ref, twc_ref, tws_ref,
                    twct_ref, twst_ref, yt_ref, hspec_ref, *, ct, n1, n2):
    rows = ct * n2
    twc = twc_ref[...]
    tws = tws_ref[...]

    def mm(a, m_ref):
        return jnp.dot(a.astype(BF16), m_ref[...], preferred_element_type=F32)

    def forward(x):
        p = mm(_swap(x).reshape(rows, n1), freal_ref)
        pr = p[:, :n1].reshape(ct, n2, n1)
        pi = p[:, n1:].reshape(ct, n2, n1)
        qr = pr * twc + pi * tws
        qi = pi * twc - pr * tws
        q = jnp.concatenate([_swap(qr).reshape(ct * n1, n2), _swap(qi).reshape(ct * n1, n2)], axis=1)
        xs = mm(q, fcplx_ref)
        return xs[:, :n2].reshape(ct, n1, n2), xs[:, n2:].reshape(ct, n1, n2)

    @pl.when(pl.program_id(1) == 0)
    def _():
        hr, hi = forward(hc_ref[...])
        hspec_ref[0] = hr
        hspec_ref[1] = hi

    u = ut_ref[...]
    xr, xi = forward(jnp.concatenate([u, jnp.zeros_like(u)], axis=1))
    hr = hspec_ref[0]
    hi = hspec_ref[1]
    yr = xr * hr - xi * hi
    yi = xr * hi + xi * hr
    r = mm(jnp.concatenate([yr.reshape(ct * n1, n2), yi.reshape(ct * n1, n2)], axis=1), icplx_ref)
    rr = r[:, :n2].reshape(ct, n1, n2)
    ri = r[:, n2:].reshape(ct, n1, n2)
    twct = twct_ref[...]
    twst = twst_ref[...]
    sr = rr * twct - ri * twst
    si = ri * twct + rr * twst
    s = jnp.concatenate([_swap(sr).reshape(rows, n1), _swap(si).reshape(rows, n1)], axis=1)
    z = mm(s, ireal_ref).reshape(ct, n2, n1)
    y = _swap(z)[:, :n1 // 2, :]
    yt_ref[...] = y + u * dbias_ref[...]


def _hy_conv_block(ut, hc, d_bias):
    b, d, l = ut.shape
    n2 = 128
    n1 = 2 * l // n2
    ct = HY_CONV_CH_TILE
    tables = _dft_tables(n1, n2)
    kern = functools.partial(_hy_conv_kernel, ct=ct, n1=n1, n2=n2)
    full = lambda a: pl.BlockSpec(a.shape, lambda c, bb: (0,) * a.ndim)
    yt = pl.pallas_call(
        kern,
        grid=(d // ct, b),
        in_specs=[pl.BlockSpec((None, ct, n1 // 2, n2), lambda c, bb: (bb, c, 0, 0)),
                  pl.BlockSpec((ct, n1, n2), lambda c, bb: (c, 0, 0)),
                  pl.BlockSpec((ct, 1, 1), lambda c, bb: (c, 0, 0))] + [full(t) for t in tables],
        out_specs=pl.BlockSpec((None, ct, n1 // 2, n2), lambda c, bb: (bb, c, 0, 0)),
        out_shape=jax.ShapeDtypeStruct((b, d, n1 // 2, n2), F32),
        scratch_shapes=[pltpu.VMEM((2, ct, n1, n2), F32)],
        compiler_params=_mixer_params(),
        name="hyena_longconv",
    )(ut.reshape(b, d, n1 // 2, n2), hc.reshape(d, n1, n2), d_bias.reshape(d, 1, 1), *tables)
    return yt


def _hyena_block(x, gain, lead, w_in, b_in, conv_w, conv_b, f_w1, f_b1, f_w2, f_b2, f_w3, f_freq, d_bias, w_out, b_out):
    l = x.shape[1]
    x0, ut = _hy_in_block(x, gain, w_in, lead, b_in, conv_w, conv_b)
    hc = _hy_filter_block(l, f_w1, f_b1, f_w2, f_b2, f_w3, f_freq)
    yt = _hy_conv_block(ut, hc, d_bias)
    return _hy_out_block(x, x0, yt, w_out, lead, b_out)


def _hy_out_kernel(x_ref, x0_ref, yt_ref, wout_ref, bout_ref, o_ref):
    slabs = jnp.swapaxes(yt_ref[...], 0, 1)
    y = jnp.concatenate([slabs[j].T for j in range(slabs.shape[0])], axis=0)
    z = jnp.dot((y * x0_ref[...]).astype(BF16), wout_ref[...].astype(BF16), preferred_element_type=F32)
    o_ref[...] = x_ref[...] + z + bout_ref[...]


def _hy_out_block(x, x0, yt, w_out, lead, b_out):
    b, l, d = x.shape
    tm = MIX_ROW_TILE
    slab = yt.shape[-1]
    row = pl.BlockSpec((None, tm, d), lambda bb, i: (bb, i, 0))
    return pl.pallas_call(
        _hy_out_kernel,
        grid=(b, l // tm),
        in_specs=[row, row, pl.BlockSpec((None, d, tm // slab, slab), lambda bb, i: (bb, 0, i, 0)),
                  _stacked_weight_spec(w_out, lead),
                  pl.BlockSpec((1, d), lambda bb, i: (0, 0))],
        out_specs=row,
        out_shape=jax.ShapeDtypeStruct((b, l, d), F32),
        compiler_params=_mixer_params(),
        name="hyena_out",
    )(x, x0, yt, w_out, b_out.reshape(1, d))


LANES = 128


def _softplus(x):
    return jnp.maximum(x, 0.0) + jnp.log1p(jnp.exp(-jnp.abs(x)))


def _gdn_in_kernel(xp_ref, x_ref, xn_ref, gain_ref, win_ref, cw_ref, alog_ref, dtb_ref, cum_ref, eye_ref,
                   q_ref, k_ref, v_ref, z_ref, kt_ref, cols_ref, rows_ref, pairs_ref, s_ref, *, tm, d, heads):
    hi = lax.Precision.HIGHEST
    gain = gain_ref[...]
    h = _rms_rows(x_ref[...], gain).astype(BF16)
    hh = _rms_rows(jnp.concatenate([xp_ref[...], xn_ref[...]], axis=0), gain).astype(BF16)
    dk = d // heads
    for j, (out_ref, scale) in enumerate(((q_ref, dk ** -0.5), (k_ref, 1.0), (v_ref, None))):
        cols = slice(j * d, (j + 1) * d)
        w_cols = win_ref[:, cols].astype(BF16)
        main = jnp.dot(h, w_cols, preferred_element_type=F32)
        halo = jnp.dot(hh, w_cols, preferred_element_type=F32)
        _store_with_halo(s_ref, main, halo, tm)
        c = _conv3_from(s_ref, cw_ref[:, cols], tm)
        c = c * jax.nn.sigmoid(c)
        if scale is None:
            out_ref[...] = c.astype(BF16)
        else:
            normed = []
            for hd in range(heads):
                lanes = slice(hd * dk, (hd + 1) * dk)
                ch = c[:, lanes]
                ss = jnp.sum(ch * ch, axis=-1, keepdims=True)
                normed.append(ch * (lax.rsqrt(ss + 1e-6) * scale))
                out_ref[:, lanes] = normed[-1].astype(BF16)
            if j == 1:
                kt = jnp.concatenate(normed, axis=1).T
                n_chunk, _, c = kt_ref.shape
                for ci in range(n_chunk):
                    kt_ref[ci] = kt[:, ci * c:(ci + 1) * c]
    z_ref[...] = jnp.dot(h, win_ref[:, 3 * d:4 * d].astype(BF16), preferred_element_type=F32).astype(BF16)
    ab = jnp.dot(h, win_ref[:, 4 * d:].astype(BF16), preferred_element_type=F32)
    nh2 = 2 * heads
    g = -jnp.exp(alog_ref[...]) * _softplus(ab[:, :nh2] + dtb_ref[...])
    beta = jax.nn.sigmoid(ab[:, nh2:2 * nh2])
    g_terms = _bf16_terms(g)
    cumsum = lambda mask: sum(jnp.dot(mask, term, preferred_element_type=F32) for term in g_terms)
    gc_f = cumsum(cum_ref[0])
    gc_b = cumsum(cum_ref[1])
    chunk = tm // rows_ref.shape[1]
    for dr, gc in enumerate((gc_f, gc_b)):
        own = slice(dr * heads, (dr + 1) * heads)
        cols = jnp.concatenate([gc[:, own], beta[:, own]], axis=1)
        cols_ref[dr] = cols
        rows = lax.dot_general(eye_ref[...], cols, (((1,), (1,)), ((), ())), precision=hi,
                               preferred_element_type=F32)
        for ci in range(tm // chunk):
            chunk_rows = rows[:, ci * chunk:(ci + 1) * chunk]
            rows_ref[dr, ci] = chunk_rows
            pairs_ref[dr, ci] = jnp.concatenate(
                [jnp.concatenate([chunk_rows[r:r + 1], chunk_rows[r + 1:r + 2]], axis=1)
                 for r in range(0, nh2, 2)], axis=0)


def _chunk_cumsum_masks(tm, chunk):
    import numpy as np
    r = np.arange(tm)
    same = (r[:, None] // chunk) == (r[None, :] // chunk)
    lower = same & (r[None, :] <= r[:, None])
    upper = same & (r[None, :] >= r[:, None])
    return jnp.asarray(np.stack([lower, upper]).astype(np.float32)).astype(BF16)


def _gdn_in_block(x, gain, w_in, lead, conv_w, a_log, dt_bias):
    b, l, d = x.shape
    heads = GD_HEADS
    nh2 = 2 * heads
    tm = GD_ROW_TILE
    kern = functools.partial(_gdn_in_kernel, tm=tm, d=d, heads=heads)
    row = pl.BlockSpec((None, tm, d), lambda bb, i: (bb, i, 0))
    wide = jax.ShapeDtypeStruct((b, l, d), BF16)
    c = GD_CHUNK
    return pl.pallas_call(
        kern,
        grid=(b, l // tm),
        in_specs=_halo_specs(tm, d, l) + [
            pl.BlockSpec((1, d), lambda bb, i: (0, 0)),
            _stacked_weight_spec(w_in, lead),
            pl.BlockSpec((3, 3 * d), lambda bb, i: (0, 0)),
            pl.BlockSpec((1, nh2), lambda bb, i: (0, 0)),
            pl.BlockSpec((1, nh2), lambda bb, i: (0, 0)),
            pl.BlockSpec((2, tm, tm), lambda bb, i: (0, 0, 0), **_RESIDENT),
            pl.BlockSpec((nh2, nh2), lambda bb, i: (0, 0)),
        ],
        out_specs=[row, row, row, row,
                   pl.BlockSpec((None, tm // c, d, c), lambda bb, i: (bb, i, 0, 0)),
                   pl.BlockSpec((2, None, tm, nh2), lambda bb, i: (0, bb, i, 0)),
                   pl.BlockSpec((2, None, tm // c, nh2, c), lambda bb, i: (0, bb, i, 0, 0)),
                   pl.BlockSpec((2, None, tm // c, heads, 2 * c), lambda bb, i: (0, bb, i, 0, 0))],
        out_shape=[wide, wide, wide, wide, jax.ShapeDtypeStruct((b, l // c, d, c), F32),
                   jax.ShapeDtypeStruct((2, b, l, nh2), F32),
                   jax.ShapeDtypeStruct((2, b, l // c, nh2, c), F32),
                   jax.ShapeDtypeStruct((2, b, l // c, heads, 2 * c), F32)],
        scratch_shapes=[pltpu.VMEM((tm + 2 * HALO, d), F32)],
        compiler_params=_mixer_params(),
        name="gdn_in",
    )(x, x, x, gain.reshape(1, d), w_in, conv_w, a_log.reshape(1, nh2), dt_bias.reshape(1, nh2),
      _chunk_cumsum_masks(tm, GD_CHUNK), jnp.eye(nh2, dtype=F32))


GD_GROUP = 8


def _decay(gc_col, gc_row, after_or_same):
    return jnp.exp(jnp.where(after_or_same, gc_col - gc_row, -jnp.inf))


def _spread_exact(cols, sel):
    hi, mid, lo = (jnp.dot(term, sel, preferred_element_type=F32) for term in _bf16_terms(cols))
    return (hi + mid) + lo


def _head_pair_block_diag(left, right):
    zero = jnp.zeros_like(left)
    return jnp.concatenate([jnp.concatenate([left, zero], axis=1),
                            jnp.concatenate([zero, right], axis=1)], axis=0)


def _gdn_prep_kernel(k_ref, cols_ref, pairs_ref, selg_ref, selb_ref, a_ref, *, heads, dk, c, group):
    direction = pl.program_id(0)
    row = lax.broadcasted_iota(jnp.int32, (c, 2 * c), 0)
    col = lax.broadcasted_iota(jnp.int32, (c, 2 * c), 1) % c
    delta = (row - col) * (1 - 2 * direction)
    cols = cols_ref[...]
    g_pairs = _spread_exact(cols, selg_ref[...])
    b_pairs = _spread_exact(cols, selb_ref[...])
    n_pair = heads // 2
    for ci in range(group):
        rows = slice(ci * c, (ci + 1) * c)
        for m in range(n_pair):
            lanes = slice(2 * c * m, 2 * c * (m + 1))
            kp = k_ref[rows, 2 * dk * m:2 * dk * (m + 1)]
            gram = lax.dot_general(kp, _head_pair_block_diag(kp[:, :dk], kp[:, dk:]),
                                   (((1,), (1,)), ((), ())), preferred_element_type=F32)
            dec = jnp.exp(jnp.where(delta > 0, g_pairs[rows, lanes] - pairs_ref[ci, m:m + 1, :], -jnp.inf))
            a_ref[:, ci * n_pair + m, :] = b_pairs[rows, lanes] * gram * dec


def _pair_selectors(heads, c):
    eye = jnp.eye(2 * heads, heads, dtype=BF16)
    return jnp.repeat(eye, c, axis=1), jnp.repeat(jnp.roll(eye, heads, axis=0), c, axis=1)


def _gdn_prep_block(k, cols, pairs):
    b, l, d = k.shape
    heads, c, group = GD_HEADS, GD_CHUNK, GD_GROUP
    rows = c * group
    ng = l // rows
    kern = functools.partial(_gdn_prep_kernel, heads=heads, dk=d // heads, c=c, group=group)
    per = group * heads // 2
    full = lambda a: pl.BlockSpec(a.shape, lambda dr, bb, g: (0,) * a.ndim)
    selg, selb = _pair_selectors(heads, c)
    return pl.pallas_call(
        kern,
        grid=(2, b, ng),
        in_specs=[pl.BlockSpec((None, rows, d), lambda dr, bb, g: (bb, g, 0)),
                  pl.BlockSpec((None, None, rows, 2 * heads), lambda dr, bb, g: (dr, bb, g, 0)),
                  pl.BlockSpec((None, None, group, heads, 2 * c), lambda dr, bb, g: (dr, bb, g, 0, 0)),
                  full(selg), full(selb)],
        out_specs=pl.BlockSpec((None, c, per, 2 * c), lambda dr, bb, g: (dr, 0, bb * ng + g, 0)),
        out_shape=jax.ShapeDtypeStruct((2, c, b * ng * per, 2 * c), F32),
        compiler_params=pltpu.CompilerParams(dimension_semantics=("arbitrary",) * 3,
                                             vmem_limit_bytes=V7X_VMEM_LIMIT_BYTES),
        name="gdn_prep",
    )(k, cols, pairs, selg, selb)


SUBLANES = 8


def _gdn_tsolve_kernel(a_ref, t_ref, a3_ref, t3_ref, *, c, upper):
    p = a_ref.shape[1]
    for i in range(c):
        slab = a_ref[i].T
        a3_ref[0, i] = slab[:c]
        a3_ref[1, i] = slab[c:]
    t3_ref[...] = jnp.zeros_like(t3_ref)
    ntile = c // SUBLANES
    sub = lax.broadcasted_iota(jnp.int32, (SUBLANES, p), 0)
    order = range(c - 1, -1, -1) if upper else range(c)
    for i in order:
        solved = range(i + 1, c) if upper else range(i)
        lo, hi = (i // SUBLANES, ntile) if upper else (0, i // SUBLANES + 1)
        ti = i // SUBLANES
        for half in range(2):
            acc = [jnp.zeros((SUBLANES, p), F32) for _ in range(lo, hi)]
            for j in solved:
                jlo, jhi = (j // SUBLANES, ntile) if upper else (0, j // SUBLANES + 1)
                coef = a3_ref[half, i, j:j + 1, :]
                tj = t3_ref[half, j, jlo * SUBLANES:jhi * SUBLANES, :]
                for t in range(jlo, jhi):
                    acc[t - lo] = acc[t - lo] - coef * tj[(t - jlo) * SUBLANES:(t - jlo + 1) * SUBLANES]
            acc[ti - lo] = acc[ti - lo] + jnp.where(sub == i % SUBLANES, 1.0, 0.0)
            t3_ref[half, i, lo * SUBLANES:hi * SUBLANES, :] = jnp.concatenate(acc, axis=0)
    for ib in range(0, c, SUBLANES):
        slabs = jnp.stack([jnp.concatenate([t3_ref[0, i], t3_ref[1, i]], axis=0).T
                           for i in range(ib, ib + SUBLANES)], axis=0)
        t_ref[:, ib:ib + SUBLANES, :] = jnp.swapaxes(slabs, 0, 1)


def _gdn_tsolve_block(a, direction):
    _, c, pairs, _ = a.shape
    blk = LANES
    upper = bool(direction)
    kern = functools.partial(_gdn_tsolve_kernel, c=c, upper=upper)
    return pl.pallas_call(
        kern,
        grid=(pairs // blk,),
        in_specs=[pl.BlockSpec((None, c, blk, 2 * c), lambda i: (direction, 0, i, 0))],
        out_specs=pl.BlockSpec((blk, c, 2 * c), lambda i: (i, 0, 0)),
        out_shape=jax.ShapeDtypeStruct((pairs, c, 2 * c), F32),
        scratch_shapes=[pltpu.VMEM((2, c, c, blk), F32), pltpu.VMEM((2, c, c, blk), F32)],
        compiler_params=pltpu.CompilerParams(dimension_semantics=("arbitrary",),
                                             vmem_limit_bytes=V7X_VMEM_LIMIT_BYTES),
        name="gdn_tsolve_bwd" if upper else "gdn_tsolve_fwd",
    )(a)


def _gdn_scan_kernel(q_ref, k_ref, v_ref, kt_ref, cols_ref, rows_ref, pairs_ref, t_ref, sel_ref, *rest,
                     heads, dk, c, group, reverse, add_other):
    other_ref = rest[0] if add_other else None
    o_ref, s_ref, g_ref, qe_ref, rhs_ref, attn_ref, u_ref, w_ref = rest[int(add_other):]

    @pl.when(pl.program_id(1) == 0)
    def _():
        s_ref[...] = jnp.zeros_like(s_ref)

    g_all = _spread_exact(cols_ref[...], sel_ref[...])
    g_ref[...] = g_all
    eg = jnp.exp(g_all)
    qe_ref[...] = (q_ref[...] * eg).astype(BF16)
    ke = k_ref[...] * eg
    for hd in range(heads):
        lanes = slice(hd * dk, (hd + 1) * dk)
        rhs_ref[:, 2 * hd * dk:(2 * hd + 1) * dk] = v_ref[:, lanes]
        rhs_ref[:, (2 * hd + 1) * dk:(2 * hd + 2) * dk] = ke[:, lanes].astype(BF16)

    row = lax.broadcasted_iota(jnp.int32, (c, c), 0)
    col = lax.broadcasted_iota(jnp.int32, (c, c), 1)
    incl = (row <= col) if reverse else (row >= col)
    head_lanes = [slice(hd * dk, (hd + 1) * dk) for hd in range(heads)]
    n_pair = heads // 2

    for ci in range(group):
        rows = slice(ci * c, (ci + 1) * c)
        for hd, lanes in enumerate(head_lanes):
            dec = _decay(g_ref[rows, hd * dk:hd * dk + c], rows_ref[ci, hd:hd + 1, :], incl)
            gram = lax.dot_general(q_ref[rows, lanes], k_ref[rows, lanes],
                                   (((1,), (1,)), ((), ())), preferred_element_type=F32)
            attn_ref[ci * heads + hd] = (gram * dec).astype(BF16)
        for m in range(n_pair):
            width = 2 * dk
            t_pair = t_ref[ci * n_pair + m] * pairs_ref[ci, n_pair + m:n_pair + m + 1, :]
            rhs_pair = _head_pair_block_diag(rhs_ref[rows, 2 * m * width:(2 * m + 1) * width],
                                             rhs_ref[rows, (2 * m + 1) * width:(2 * m + 2) * width])
            sol = jnp.dot(t_pair.astype(BF16), rhs_pair, preferred_element_type=F32)
            for par in range(2):
                lanes = head_lanes[2 * m + par]
                u_ref[rows, lanes] = sol[:, par * width:par * width + dk]
                w_ref[rows, lanes] = sol[:, par * width + dk:(par + 1) * width].astype(BF16)

    for step in range(group):
        ci = group - 1 - step if reverse else step
        rows = slice(ci * c, (ci + 1) * c)
        last = ci * c if reverse else (ci + 1) * c - 1
        states = [s_ref[hd] for hd in range(heads)]
        states_bf = [s.astype(BF16) for s in states]
        w_s = [jnp.dot(w_ref[rows, lanes], s, preferred_element_type=F32)
               for lanes, s in zip(head_lanes, states_bf)]
        q_s = [jnp.dot(qe_ref[rows, lanes], s, preferred_element_type=F32)
               for lanes, s in zip(head_lanes, states_bf)]
        for hd, lanes in enumerate(head_lanes):
            v_new_bf = (u_ref[rows, lanes] - w_s[hd]).astype(BF16)
            o = q_s[hd] + jnp.dot(attn_ref[ci * heads + hd], v_new_bf, preferred_element_type=F32)
            o_ref[rows, lanes] = o + other_ref[rows, lanes] if add_other else o
            g_tot = g_ref[last:last + 1, lanes]
            k_dec_t = (kt_ref[ci, lanes, :] * jnp.exp(g_tot[:, :c] - rows_ref[ci, hd:hd + 1, :])).astype(BF16)
            s_ref[hd] = states[hd] * jnp.exp(g_tot) + jnp.dot(k_dec_t, v_new_bf, preferred_element_type=F32)


def _gdn_scan_block(q, k, v, kt, cols, rows_t, pairs, t, reverse, other=None):
    b, l, d = q.shape
    heads, c, group = GD_HEADS, GD_CHUNK, GD_GROUP
    dk = d // heads
    rows = c * group
    ng = l // rows
    per = group * heads
    dr = int(reverse)
    add_other = other is not None
    kern = functools.partial(_gdn_scan_kernel, heads=heads, dk=dk, c=c, group=group, reverse=reverse,
                             add_other=add_other)
    visit = (lambda g: ng - 1 - g) if reverse else (lambda g: g)
    wide = pl.BlockSpec((None, rows, d), lambda bb, g: (bb, visit(g), 0))
    sel = jnp.repeat(jnp.eye(2 * heads, heads, dtype=BF16), dk, axis=1)
    return pl.pallas_call(
        kern,
        grid=(b, ng),
        in_specs=[wide, wide, wide,
                  pl.BlockSpec((None, group, d, c), lambda bb, g: (bb, visit(g), 0, 0)),
                  pl.BlockSpec((None, None, rows, 2 * heads), lambda bb, g: (dr, bb, visit(g), 0)),
                  pl.BlockSpec((None, None, group, 2 * heads, c), lambda bb, g: (dr, bb, visit(g), 0, 0)),
                  pl.BlockSpec((None, None, group, heads, 2 * c), lambda bb, g: (dr, bb, visit(g), 0, 0)),
                  pl.BlockSpec((per // 2, c, 2 * c), lambda bb, g: (bb * ng + visit(g), 0, 0)),
                  pl.BlockSpec((2 * heads, d), lambda bb, g: (0, 0))] + [wide] * add_other,
        out_specs=wide,
        out_shape=jax.ShapeDtypeStruct((b, l, d), F32),
        scratch_shapes=[pltpu.VMEM((heads, dk, dk), F32), pltpu.VMEM((rows, d), F32),
                        pltpu.VMEM((rows, d), BF16), pltpu.VMEM((rows, 2 * d), BF16),
                        pltpu.VMEM((per, c, c), BF16),
                        pltpu.VMEM((rows, d), F32), pltpu.VMEM((rows, d), BF16)],
        compiler_params=_mixer_params(),
        name="gdn_scan_bwd" if reverse else "gdn_scan_fwd",
    )(q, k, v, kt, cols, rows_t, pairs, t, sel, *([other] if add_other else []))


def _gdn_out_kernel(x_ref, o_ref, z_ref, ng_ref, wout_ref, y_ref, g_ref, *, heads, dk):
    o = o_ref[...]
    z = z_ref[...].astype(F32)
    for hd in range(heads):
        lanes = slice(hd * dk, (hd + 1) * dk)
        oh = o[:, lanes]
        ms = jnp.mean(oh * oh, axis=-1, keepdims=True)
        zh = z[:, lanes]
        g_ref[:, lanes] = (oh * lax.rsqrt(ms + NORM_EPS) * ng_ref[...] * (zh * jax.nn.sigmoid(zh))).astype(BF16)
    y_ref[...] = x_ref[...] + jnp.dot(g_ref[...], wout_ref[...].astype(BF16), preferred_element_type=F32)


def _gdn_out_block(x, o, z, norm_g, w_out, lead):
    b, l, d = x.shape
    heads = GD_HEADS
    dk = d // heads
    tm = GD_ROW_TILE
    row = pl.BlockSpec((None, tm, d), lambda bb, i: (bb, i, 0))
    kern = functools.partial(_gdn_out_kernel, heads=heads, dk=dk)
    return pl.pallas_call(
        kern,
        grid=(b, l // tm),
        in_specs=[row, row, row,
                  pl.BlockSpec((1, dk), lambda bb, i: (0, 0)),
                  _stacked_weight_spec(w_out, lead)],
        out_specs=row,
        out_shape=jax.ShapeDtypeStruct((b, l, d), F32),
        scratch_shapes=[pltpu.VMEM((tm, d), BF16)],
        compiler_params=_mixer_params(),
        name="gdn_out",
    )(x, o, z, norm_g.reshape(1, dk), w_out)


def _gdn_block(x, gain, w_in, lead, conv_w, a_log, dt_bias, norm_g, w_out):
    q, k, v, z, kt, cols, rows_t, pairs = _gdn_in_block(x, gain, w_in, lead, conv_w, a_log, dt_bias)
    a = _gdn_prep_block(k, cols, pairs)
    o = None
    for dr in range(2):
        o = _gdn_scan_block(q, k, v, kt, cols, rows_t, pairs, _gdn_tsolve_block(a, dr), bool(dr), other=o)
    return _gdn_out_block(x, o, z, norm_g, w_out, lead)


def kernel(x, norms, final_norm, ffn_w_in, ffn_w_out, sc_w_in, sc_conv, sc_w_out, hy_w_in, hy_b_in, hy_conv, hy_conv_b, hy_f_w1, hy_f_b1, hy_f_w2, hy_f_b2, hy_f_w3, hy_f_freq, hy_d, hy_w_out, hy_b_out, gd_w_in, gd_conv, gd_a_log, gd_dt_bias, gd_norm, gd_w_out):
    b, l, d = x.shape
    depth = norms.shape[0]

    def ffn(xx, i, k):
        final = final_norm if (i == depth - 1 and k == 1) else None
        return _ffn_block(xx.reshape(b * l, d), norms[i, 2 * k], ffn_w_in, ffn_w_out, (i, k),
                          final).reshape(b, l, d)

    for i in range(depth):
        m, j = i % N_MIXERS, i // N_MIXERS
        x = ffn(x, i, 0)
        if m == 0:
            x = _sc_mixer_block(x, norms[i, 1], sc_w_in, sc_conv[j], sc_w_out, (j,))
        elif m == 1:
            x = _hyena_block(x, norms[i, 1], (j,), hy_w_in, hy_b_in[j], hy_conv[j], hy_conv_b[j],
                             hy_f_w1[j], hy_f_b1[j], hy_f_w2[j], hy_f_b2[j], hy_f_w3[j],
                             hy_f_freq[j], hy_d[j], hy_w_out, hy_b_out[j])
        else:
            x = _gdn_block(x, norms[i, 1], gd_w_in, (j,), gd_conv[j], gd_a_log[j], gd_dt_bias[j],
                           gd_norm[j], gd_w_out)
        x = ffn(x, i, 1)
    return x
```
